```python
import jax, jax.numpy as jnp
from jax import lax
import numpy as np

D_MODEL = 1024
BATCH = 16
SEQ = 2048
DEPTH = 4

GRID_W = 64
CTX_LEN = 256
N_MIXERS = 3
ROPE_THETA = 10000.0
EPS = 1e-6
Q_BLOCK = 128
NEG_INF = -1e30

A_HEADS = 16
A_KV_HEADS = 4
A_HEAD_DIM = 64
B_HEADS = 16
B_Q_RANK = 384
B_KV_RANK = 256
B_NOPE_DIM = 64
B_ROPE_DIM = 32
B_V_DIM = 64
C_HEADS = 16
C_HEAD_DIM = 64
NA_ROWS = 8
NA_COLS = 16
N_EXPERTS = 16
EXPERT_FF = 1024
EC_CAPACITY_FACTOR = 2

N_A = (DEPTH + 2) // 3
N_B = (DEPTH + 1) // 3
N_C = DEPTH // 3

kernel_name = "hybrid_diffusion_gqa_mla_natten_ecmoe"


def rms_norm(x, w):
    xf = x.astype(jnp.float32)
    y = xf * lax.rsqrt(jnp.mean(xf * xf, axis=-1, keepdims=True) + EPS)
    return (y * w.astype(jnp.float32)).astype(x.dtype)


def modulate(h, shift, scale):
    return h * (1.0 + scale[:, None]) + shift[:, None]


def axial_rope_tables(n_tokens, rot_dim, dtype):
    n_freq = rot_dim // 4
    inv = jnp.float32(ROPE_THETA) ** (-jnp.arange(n_freq, dtype=jnp.float32) / n_freq)
    t = jnp.arange(n_tokens, dtype=jnp.int32)
    row = (t // GRID_W).astype(jnp.float32)
    col = (t % GRID_W).astype(jnp.float32)
    ang = jnp.concatenate([row[:, None] * inv, col[:, None] * inv], axis=-1)
    return jnp.cos(ang).astype(dtype), jnp.sin(ang).astype(dtype)


def apply_rope(x, cos, sin):
    half = x.shape[-1] // 2
    x1, x2 = x[..., :half], x[..., half:]
    shp = (1, x.shape[1]) + (1,) * (x.ndim - 3) + (half,)
    c, s = cos.reshape(shp), sin.reshape(shp)
    return jnp.concatenate([x1 * c - x2 * s, x1 * s + x2 * c], axis=-1)


def attend(q, k, v):
    scale = q.shape[-1] ** -0.5
    s = jnp.einsum('bqhgd,bkhd->bhgqk', q, k, preferred_element_type=jnp.float32) * scale
    p = jax.nn.softmax(s, axis=-1).astype(v.dtype)
    return jnp.einsum('bhgqk,bkhd->bqhgd', p, v)


def blocked_attend(q, k, v):
    B, T = q.shape[:2]
    nb = T // Q_BLOCK
    qb = jnp.moveaxis(q.reshape((B, nb, Q_BLOCK) + q.shape[2:]), 1, 0)
    ob = lax.map(lambda qi: attend(qi, k, v), qb)
    return jnp.moveaxis(ob, 0, 1).reshape((B, T) + ob.shape[3:])


def gqa_axial_mixer(h_lat, h_ctx, w_qkv, q_norm, k_norm, w_o, need_ctx):
    G = A_HEADS // A_KV_HEADS
    nq = A_HEADS * A_HEAD_DIM
    nk = A_KV_HEADS * A_HEAD_DIM

    def project(h):
        B, T, _ = h.shape
        qkv = h @ w_qkv
        q = qkv[..., :nq].reshape(B, T, A_KV_HEADS, G, A_HEAD_DIM)
        k = qkv[..., nq:nq + nk].reshape(B, T, A_KV_HEADS, A_HEAD_DIM)
        v = qkv[..., nq + nk:].reshape(B, T, A_KV_HEADS, A_HEAD_DIM)
        return rms_norm(q, q_norm), rms_norm(k, k_norm), v

    B, T = h_lat.shape[:2]
    q_l, k_l, v_l = project(h_lat)
    q_c, k_c, v_c = project(h_ctx)
    cos, sin = axial_rope_tables(T, A_HEAD_DIM, h_lat.dtype)
    q_l = apply_rope(q_l, cos, sin)
    k_l = apply_rope(k_l, cos, sin)
    k_all = jnp.concatenate([k_c, k_l], axis=1)
    v_all = jnp.concatenate([v_c, v_l], axis=1)
    out_l = blocked_attend(q_l, k_all, v_all).reshape(B, T, nq) @ w_o
    out_c = None
    if need_ctx:
        out_c = attend(q_c, k_c, v_c).reshape(B, h_ctx.shape[1], nq) @ w_o
    return out_l, out_c


def mla_mixer(h_lat, h_ctx, w_dq, q_lat_norm, w_uq, w_dkv, kv_lat_norm, w_ukv, q_norm, k_norm, w_o, need_ctx):
    qd = B_NOPE_DIM + B_ROPE_DIM

    def project(h):
        B, T, _ = h.shape
        cq = rms_norm(h @ w_dq, q_lat_norm)
        q = (cq @ w_uq).reshape(B, T, B_HEADS, qd)
        kv_a = h @ w_dkv
        ckv = rms_norm(kv_a[..., :B_KV_RANK], kv_lat_norm)
        k_rope = kv_a[..., B_KV_RANK:]
        kv = (ckv @ w_ukv).reshape(B, T, B_HEADS, B_NOPE_DIM + B_V_DIM)
        q_nope = rms_norm(q[..., :B_NOPE_DIM], q_norm[:B_NOPE_DIM])
        q_rope = rms_norm(q[..., B_NOPE_DIM:], q_norm[B_NOPE_DIM:])
        k_nope = rms_norm(kv[..., :B_NOPE_DIM], k_norm[:B_NOPE_DIM])
        k_rope = rms_norm(k_rope, k_norm[B_NOPE_DIM:])
        return q_nope, q_rope, k_nope, k_rope, kv[..., B_NOPE_DIM:]

    def assemble(q_nope, q_rope, k_nope, k_rope, v):
        q = jnp.concatenate([q_nope, q_rope], axis=-1)[:, :, :, None, :]
        k_rope_h = jnp.broadcast_to(k_rope[:, :, None, :], k_nope.shape[:3] + (B_ROPE_DIM,))
        k = jnp.concatenate([k_nope, k_rope_h], axis=-1)
        return q, k, v

    B, T = h_lat.shape[:2]
    q_nope_l, q_rope_l, k_nope_l, k_rope_l, v_l = project(h_lat)
    cos, sin = axial_rope_tables(T, B_ROPE_DIM, h_lat.dtype)
    q_l, k_l, v_l = assemble(q_nope_l, apply_rope(q_rope_l, cos, sin), k_nope_l, apply_rope(k_rope_l, cos, sin), v_l)
    q_c, k_c, v_c = assemble(*project(h_ctx))
    k_all = jnp.concatenate([k_c, k_l], axis=1)
    v_all = jnp.concatenate([v_c, v_l], axis=1)
    out_l = blocked_attend(q_l, k_all, v_all).reshape(B, T, B_HEADS * B_V_DIM) @ w_o
    out_c = None
    if need_ctx:
        out_c = attend(q_c, k_c, v_c).reshape(B, h_ctx.shape[1], B_HEADS * B_V_DIM) @ w_o
    return out_l, out_c


def neighbourhood_mixer(h_lat, h_ctx, w_qkv, q_norm, k_norm, rpb, w_o, need_ctx):
    hd = C_HEADS * C_HEAD_DIM

    def project(h):
        B, T, _ = h.shape
        qkv = h @ w_qkv
        q = qkv[..., :hd].reshape(B, T, C_HEADS, C_HEAD_DIM)
        k = qkv[..., hd:2 * hd].reshape(B, T, C_HEADS, C_HEAD_DIM)
        v = qkv[..., 2 * hd:].reshape(B, T, C_HEADS, C_HEAD_DIM)
        return rms_norm(q, q_norm), rms_norm(k, k_norm), v

    B, T = h_lat.shape[:2]
    q_l, k_l, v_l = project(h_lat)
    q_c, k_c, v_c = project(h_ctx)
    rows = T // GRID_W
    kr = min(NA_ROWS, rows)
    kc = min(NA_COLS, GRID_W)
    scale = C_HEAD_DIM ** -0.5
    k_grid = k_l.reshape(B, rows, GRID_W, C_HEADS, C_HEAD_DIM)
    v_grid = v_l.reshape(B, rows, GRID_W, C_HEADS, C_HEAD_DIM)
    q_rows = jnp.moveaxis(q_l.reshape(B, rows, GRID_W, C_HEADS, C_HEAD_DIM), 1, 0)
    col = jnp.arange(GRID_W)
    col_start = jnp.clip(col - kc // 2, 0, GRID_W - kc)
    col_mask = (col[None, :] >= col_start[:, None]) & (col[None, :] < col_start[:, None] + kc)
    win_mask = jnp.tile(col_mask, (1, kr))
    dc_idx = jnp.clip(col[None, :] - col[:, None] + NA_COLS - 1, 0, 2 * NA_COLS - 2)
    rpb_f = rpb.astype(jnp.float32)
    n_win = kr * GRID_W

    def row_fn(args):
        r, q_row = args
        rs = jnp.clip(r - kr // 2, 0, rows - kr)
        k_band = lax.dynamic_slice_in_dim(k_grid, rs, kr, axis=1).reshape(B, n_win, C_HEADS, C_HEAD_DIM)
        v_band = lax.dynamic_slice_in_dim(v_grid, rs, kr, axis=1).reshape(B, n_win, C_HEADS, C_HEAD_DIM)
        dr_idx = rs + jnp.arange(kr) - r + NA_ROWS - 1
        bias = rpb_f[:, dr_idx[None, :, None], dc_idx[:, None, :]].reshape(C_HEADS, GRID_W, n_win)
        s_win = jnp.einsum('bqhd,bkhd->bhqk', q_row, k_band, preferred_element_type=jnp.float32) * scale + bias[None]
        s_win = jnp.where(win_mask[None, None], s_win, NEG_INF)
        s_ctx = jnp.einsum('bqhd,bkhd->bhqk', q_row, k_c, preferred_element_type=jnp.float32) * scale
        p = jax.nn.softmax(jnp.concatenate([s_win, s_ctx], axis=-1), axis=-1).astype(v_l.dtype)
        return (jnp.einsum('bhqk,bkhd->bqhd', p[..., :n_win], v_band)
                + jnp.einsum('bhqk,bkhd->bqhd', p[..., n_win:], v_c))

    o_l = lax.map(row_fn, (jnp.arange(rows, dtype=jnp.int32), q_rows))
    out_l = jnp.moveaxis(o_l, 0, 1).reshape(B, T, hd) @ w_o
    out_c = None
    if need_ctx:
        out_c = attend(q_c[:, :, :, None, :], k_c, v_c).reshape(B, h_ctx.shape[1], hd) @ w_o
    return out_l, out_c


def expert_choice_ffn(h, router, w_gate, w_up, w_down):
    B, T, _ = h.shape
    cap = max(1, min(T, EC_CAPACITY_FACTOR * T // N_EXPERTS))
    aff = jax.nn.softmax((h @ router).astype(jnp.float32), axis=-1)
    gate, idx = lax.top_k(jnp.swapaxes(aff, 1, 2), cap)
    bidx = jnp.arange(B)[:, None, None]
    xg = h[bidx, idx]
    hid = jax.nn.silu(jnp.einsum('becd,edf->becf', xg, w_gate)) * jnp.einsum('becd,edf->becf', xg, w_up)
    y = jnp.einsum('becf,efd->becd', hid, w_down) * gate[..., None].astype(h.dtype)
    return jnp.zeros_like(h).at[bidx, idx].add(y)


def setup_inputs(seed: int = 0) -> dict:
    key = jax.random.key(seed)
    ks = iter(jax.random.split(key, 40))
    D = D_MODEL

    def nrm(shape, fan_in, scale=1.0):
        return jax.random.normal(next(ks), shape, jnp.float32) * (scale * fan_in ** -0.5)

    def gain(shape):
        return 1.0 + 0.1 * jax.random.normal(next(ks), shape, jnp.float32)

    return {
        "x": jax.random.normal(next(ks), (BATCH, SEQ, D), jnp.float32),
        "c": jax.random.normal(next(ks), (BATCH, D), jnp.float32),
        "ctx": jax.random.normal(next(ks), (BATCH, CTX_LEN, D), jnp.float32),
        "c_ctx": jax.random.normal(next(ks), (D,), jnp.float32),
        "ada_w": nrm((DEPTH, D, 6 * D), D, 0.5),
        "ada_b": 0.02 * jax.random.normal(next(ks), (DEPTH, 6 * D), jnp.float32),
        "norm1_w": gain((DEPTH, D)),
        "norm2_w": gain((DEPTH, D)),
        "a_wqkv": nrm((N_A, D, (A_HEADS + 2 * A_KV_HEADS) * A_HEAD_DIM), D),
        "a_qnorm": gain((N_A, A_HEAD_DIM)),
        "a_knorm": gain((N_A, A_HEAD_DIM)),
        "a_wo": nrm((N_A, A_HEADS * A_HEAD_DIM, D), A_HEADS * A_HEAD_DIM),
        "b_wdq": nrm((N_B, D, B_Q_RANK), D),
        "b_qnorm_lat": gain((N_B, B_Q_RANK)),
        "b_wuq": nrm((N_B, B_Q_RANK, B_HEADS * (B_NOPE_DIM + B_ROPE_DIM)), B_Q_RANK),
        "b_wdkv": nrm((N_B, D, B_KV_RANK + B_ROPE_DIM), D),
        "b_kvnorm_lat": gain((N_B, B_KV_RANK)),
        "b_wukv": nrm((N_B, B_KV_RANK, B_HEADS * (B_NOPE_DIM + B_V_DIM)), B_KV_RANK),
        "b_qnorm": gain((N_B, B_NOPE_DIM + B_ROPE_DIM)),
        "b_knorm": gain((N_B, B_NOPE_DIM + B_ROPE_DIM)),
        "b_wo": nrm((N_B, B_HEADS * B_V_DIM, D), B_HEADS * B_V_DIM),
        "c_wqkv": nrm((N_C, D, 3 * C_HEADS * C_HEAD_DIM), D),
        "c_qnorm": gain((N_C, C_HEAD_DIM)),
        "c_knorm": gain((N_C, C_HEAD_DIM)),
        "c_rpb": 0.1 * jax.random.normal(next(ks), (N_C, C_HEADS, 2 * NA_ROWS - 1, 2 * NA_COLS - 1), jnp.float32),
        "c_wo": nrm((N_C, C_HEADS * C_HEAD_DIM, D), C_HEADS * C_HEAD_DIM),
        "moe_router": nrm((DEPTH, D, N_EXPERTS), D),
        "moe_wg": nrm((DEPTH, N_EXPERTS, D, EXPERT_FF), D),
        "moe_wu": nrm((DEPTH, N_EXPERTS, D, EXPERT_FF), D),
        "moe_wd": nrm((DEPTH, N_EXPERTS, EXPERT_FF, D), EXPERT_FF),
    }


def reference(x, c, ctx, c_ctx, ada_w, ada_b, norm1_w, norm2_w,
              a_wqkv, a_qnorm, a_knorm, a_wo,
              b_wdq, b_qnorm_lat, b_wuq, b_wdkv, b_kvnorm_lat, b_wukv, b_qnorm, b_knorm, b_wo,
              c_wqkv, c_qnorm, c_knorm, c_rpb, c_wo,
              moe_router, moe_wg, moe_wu, moe_wd):
    s_lat = jax.nn.silu(c)
    s_ctx = jax.nn.silu(c_ctx)[None]
    for i in range(DEPTH):
        need_ctx = i < DEPTH - 1
        m_l = jnp.split(s_lat @ ada_w[i] + ada_b[i], 6, axis=-1)
        m_c = jnp.split(s_ctx @ ada_w[i] + ada_b[i], 6, axis=-1)
        h_l = modulate(rms_norm(x, norm1_w[i]), m_l[0], m_l[1])
        h_c = modulate(rms_norm(ctx, norm1_w[i]), m_c[0], m_c[1])
        kind, j = i % N_MIXERS, i // N_MIXERS
        if kind == 0:
            out_l, out_c = gqa_axial_mixer(h_l, h_c, a_wqkv[j], a_qnorm[j], a_knorm[j], a_wo[j], need_ctx)
        elif kind == 1:
            out_l, out_c = mla_mixer(h_l, h_c, b_wdq[j], b_qnorm_lat[j], b_wuq[j], b_wdkv[j], b_kvnorm_lat[j],
                                     b_wukv[j], b_qnorm[j], b_knorm[j], b_wo[j], need_ctx)
        else:
            out_l, out_c = neighbourhood_mixer(h_l, h_c, c_wqkv[j], c_qnorm[j], c_knorm[j], c_rpb[j], c_wo[j], need_ctx)
        x = x + m_l[2][:, None] * out_l
        x = x + m_l[5][:, None] * expert_choice_ffn(modulate(rms_norm(x, norm2_w[i]), m_l[3], m_l[4]),
                                                    moe_router[i], moe_wg[i], moe_wu[i], moe_wd[i])
        if need_ctx:
            ctx = ctx + m_c[2][:, None] * out_c
            ctx = ctx + m_c[5][:, None] * expert_choice_ffn(modulate(rms_norm(ctx, norm2_w[i]), m_c[3], m_c[4]),
                                                            moe_router[i], moe_wg[i], moe_wu[i], moe_wd[i])
    return x
```

```python
import functools

import numpy as np
import jax
import jax.numpy as jnp
from jax import lax
from jax.experimental import pallas as pl
from jax.experimental.pallas import tpu as pltpu

F32 = jnp.float32
BF16 = jnp.bfloat16

D = 1024
SEQ = 2048
CTX = 256
NT = CTX + SEQ
DEPTH = 4
GRID_W = 64
ROPE_THETA = 10000.0
EPS = 1e-6
NEG_INF = -1e30

TM = 256
NTILES = NT // TM
LAT_TILES = SEQ // TM

A_HEADS, A_KV_HEADS, A_HEAD_DIM = 16, 4, 64
B_HEADS, B_Q_RANK, B_KV_RANK, B_NOPE, B_ROPE, B_V = 16, 384, 256, 64, 32, 64
C_HEADS, C_HEAD_DIM, NA_ROWS, NA_COLS = 16, 64, 8, 16
N_EXPERTS, EXPERT_FF = 16, 1024
CAP_LAT = 2 * SEQ // N_EXPERTS
CAP_CTX = 2 * CTX // N_EXPERTS
LANES = 128
BAND_TILES = 3

VMEM_LIMIT = 56 * 1024 * 1024


def _cparams(n_axes):
    return pltpu.CompilerParams(dimension_semantics=("arbitrary",) * n_axes, vmem_limit_bytes=VMEM_LIMIT)


def _dot(a, b):
    return jnp.dot(a, b, preferred_element_type=F32)


def _dot_t(a, b):
    return lax.dot_general(a, b, (((1,), (1,)), ((), ())), preferred_element_type=F32)


def _silu(x):
    return x / (1.0 + jnp.exp(-x))


def _prenorm(x, gain, shift, scale):
    ms = jnp.mean(x * x, axis=-1, keepdims=True)
    return (x * lax.rsqrt(ms + EPS) * gain) * (1.0 + scale) + shift


def _seg_rms(z, bd, inv_n):
    z2 = z * z
    hi = z2.astype(BF16)
    lo = (z2 - hi.astype(F32)).astype(BF16)
    ss = _dot(hi, bd) + _dot(lo, bd)
    return lax.rsqrt(ss * inv_n + EPS)


def _rope(z, cosw, sinw, first_half, half):
    w = z.shape[-1]
    up = pltpu.roll(z, w - half, axis=1)
    dn = pltpu.roll(z, half, axis=1)
    return z * cosw + jnp.where(first_half, up, dn) * sinw


def _ada_kernel(c_ref, w_ref, b_ref, o_ref):
    s = _silu(c_ref[...]).astype(BF16)
    o_ref[0] = _dot(s, w_ref[0].astype(BF16)) + b_ref[0]


def _ada_all(cvec, ada_w, ada_b):
    rows = cvec.shape[0]
    tn = 512
    return pl.pallas_call(
        _ada_kernel,
        grid=(DEPTH, 6 * D // tn),
        in_specs=[pl.BlockSpec((rows, D), lambda i, j: (0, 0)),
                  pl.BlockSpec((1, D, tn), lambda i, j: (i, 0, j)),
                  pl.BlockSpec((1, 1, tn), lambda i, j: (i, 0, j))],
        out_specs=pl.BlockSpec((1, rows, tn), lambda i, j: (i, 0, j)),
        out_shape=jax.ShapeDtypeStruct((DEPTH, rows, 6 * D), F32),
        compiler_params=_cparams(2), name="adaln",
    )(cvec, ada_w, ada_b.reshape(DEPTH, 1, 6 * D))


def _mod_spec():
    return pl.BlockSpec((1, 1, 6, D), lambda b, t: (b, jnp.minimum(t, 1), 0, 0))


def _row_spec(width):
    return pl.BlockSpec((1, TM, width), lambda b, t: (b, t, 0))


def _const_spec(shape):
    return pl.BlockSpec(shape, lambda b, t: (0,) * len(shape))


def _tab_spec(width):
    return pl.BlockSpec((TM, width), lambda b, t: (t, 0))


def _qkv_kernel(x_ref, mod_ref, g1_ref, w_ref, bd_ref, gq_ref, gk_ref, cos_ref, sin_ref,
                q_ref, k_ref, v_ref, *, nq, nk, rope):
    mod = mod_ref[0, 0]
    h = _prenorm(x_ref[0], g1_ref[...], mod[0:1], mod[1:2]).astype(BF16)
    acc = _dot(h, w_ref[...])
    bd = bd_ref[...]
    lane = lax.broadcasted_iota(jnp.int32, (1, 256), 1)
    first_half = (lane % 64) < 32
    cosw, sinw = cos_ref[...], sin_ref[...]

    def finish(z, gain):
        z = z * _seg_rms(z, bd, 1.0 / 64) * gain
        if rope:
            z = _rope(z, cosw, sinw, first_half, 32)
        return z.astype(BF16)

    for j in range(nq // 256):
        q_ref[0, :, j * 256:(j + 1) * 256] = finish(acc[:, j * 256:(j + 1) * 256], gq_ref[...])
    for j in range(nk // 256):
        k_ref[0, :, j * 256:(j + 1) * 256] = finish(acc[:, nq + j * 256:nq + (j + 1) * 256], gk_ref[...])
    v_ref[0] = acc[:, nq + nk:].astype(BF16)


def _qkv_project(x, modtab, g1, w, bd, gq, gk, cosw, sinw, nq, nk, rope):
    bsz = x.shape[0]
    return pl.pallas_call(
        functools.partial(_qkv_kernel, nq=nq, nk=nk, rope=rope),
        grid=(bsz, NTILES),
        in_specs=[_row_spec(D), _mod_spec(), _const_spec((1, D)), _const_spec((D, nq + 2 * nk)),
                  _const_spec((256, 256)), _const_spec((1, 256)), _const_spec((1, 256)),
                  _tab_spec(256), _tab_spec(256)],
        out_specs=[_row_spec(nq), _row_spec(nk), _row_spec(nk)],
        out_shape=[jax.ShapeDtypeStruct((bsz, NT, nq), BF16), jax.ShapeDtypeStruct((bsz, NT, nk), BF16),
                   jax.ShapeDtypeStruct((bsz, NT, nk), BF16)],
        compiler_params=_cparams(2), name="qkv_project",
    )(x, modtab, g1, w, bd, gq, gk, cosw, sinw)


def _mla_kernel(x_ref, mod_ref, g1_ref, w1_ref, gql_ref, gkl_ref, gkr_ref, kcos_ref, ksin_ref,
                wuq_ref, wk_ref, wv_ref, bd_ref, qgain_ref, qinv_ref, qcos_ref, qsin_ref, kgain_ref, kmask_ref,
                q_ref, k_ref, v_ref):
    mod = mod_ref[0, 0]
    h = _prenorm(x_ref[0], g1_ref[...], mod[0:1], mod[1:2]).astype(BF16)
    a = _dot(h, w1_ref[...])
    cq = a[:, :B_Q_RANK]
    cq = (cq * lax.rsqrt(jnp.mean(cq * cq, axis=-1, keepdims=True) + EPS) * gql_ref[...]).astype(BF16)
    ckv = a[:, B_Q_RANK:B_Q_RANK + B_KV_RANK]
    ckv = (ckv * lax.rsqrt(jnp.mean(ckv * ckv, axis=-1, keepdims=True) + EPS) * gkl_ref[...]).astype(BF16)
    kr = a[:, B_Q_RANK + B_KV_RANK:]
    kr = kr * lax.rsqrt(jnp.sum(kr * kr, axis=-1, keepdims=True) * (1.0 / B_ROPE) + EPS) * gkr_ref[...]
    lane128 = lax.broadcasted_iota(jnp.int32, (1, LANES), 1)
    kr = _rope(kr, kcos_ref[...], ksin_ref[...], (lane128 % 32) < 16, 16).astype(BF16)

    bd = bd_ref[...]
    lane = lax.broadcasted_iota(jnp.int32, (1, 256), 1)
    first_half = (lane % 32) < 16
    qcos, qsin = qcos_ref[...], qsin_ref[...]
    q2 = _dot(cq, wuq_ref[...])
    k2 = _dot(jnp.concatenate([ckv, kr], axis=1), wk_ref[...])
    kmask = kmask_ref[...] > 0.0
    for j in range(q2.shape[1] // 256):
        sl = slice(j * 256, (j + 1) * 256)
        z = q2[:, sl]
        z = z * _seg_rms(z, bd, qinv_ref[...]) * qgain_ref[...]
        q_ref[0, :, sl] = _rope(z, qcos, qsin, first_half, 16).astype(BF16)
        z = k2[:, sl]
        k_ref[0, :, sl] = (z * jnp.where(kmask, _seg_rms(z, bd, 1.0 / 64) * kgain_ref[...], 1.0)).astype(BF16)
    v_ref[0] = _dot(ckv, wv_ref[...]).astype(BF16)


def _mla_project(x, modtab, g1, p):
    bsz = x.shape[0]
    hq = B_HEADS * LANES
    return pl.pallas_call(
        _mla_kernel,
        grid=(bsz, NTILES),
        in_specs=[_row_spec(D), _mod_spec(), _const_spec((1, D)), _const_spec((D, 768)),
                  _const_spec((1, B_Q_RANK)), _const_spec((1, B_KV_RANK)), _const_spec((1, LANES)),
                  _tab_spec(LANES), _tab_spec(LANES),
                  _const_spec((B_Q_RANK, hq)), _const_spec((B_KV_RANK + LANES, hq)),
                  _const_spec((B_KV_RANK, B_HEADS * B_V)), _const_spec((256, 256)),
                  _const_spec((1, 256)), _const_spec((1, 256)), _tab_spec(256), _tab_spec(256),
                  _const_spec((1, 256)), _const_spec((1, 256))],
        out_specs=[_row_spec(hq), _row_spec(hq), _row_spec(B_HEADS * B_V)],
        out_shape=[jax.ShapeDtypeStruct((bsz, NT, hq), BF16), jax.ShapeDtypeStruct((bsz, NT, hq), BF16),
                   jax.ShapeDtypeStruct((bsz, NT, B_HEADS * B_V), BF16)],
        compiler_params=_cparams(2), name="mla_project",
    )(x, modtab, g1, p["w1"], p["gql"], p["gkl"], p["gkr"], p["kcos"], p["ksin"], p["wuq"], p["wk"], p["wv"],
      p["bd"], p["qgain"], p["qinv"], p["qcos"], p["qsin"], p["kgain"], p["kmask"])


def _softmax_pv(sa, sb, v):
    rows = sa.shape[0]
    s = jnp.concatenate([sa, sb], axis=0)
    m = jnp.max(s, axis=-1, keepdims=True)
    p = jnp.exp(s - m)
    l = jnp.sum(p, axis=-1, keepdims=True)
    o = _dot(p.astype(BF16), v) / l
    lane = lax.broadcasted_iota(jnp.int32, (1, LANES), 1)
    return jnp.where(lane < 64, o[:rows], o[rows:])


def _split_heads(q):
    lane = lax.broadcasted_iota(jnp.int32, (1, LANES), 1)
    zero = jnp.zeros_like(q)
    return jnp.where(lane < 64, q, zero), jnp.where(lane >= 64, q, zero)


def _pair_attn_kernel(q_ref, k_ref, v_ref, o_ref, *, wide):
    t = pl.program_id(2)

    def run(nkeys):
        q = q_ref[0]
        k = k_ref[0, :nkeys]
        if wide:
            sa, sb = _dot_t(q[:, :LANES], k[:, :LANES]), _dot_t(q[:, LANES:], k[:, LANES:])
        else:
            qa, qb = _split_heads(q)
            sa, sb = _dot_t(qa, k), _dot_t(qb, k)
        o_ref[0] = _softmax_pv(sa, sb, v_ref[0, :nkeys]).astype(BF16)

    @pl.when(t == 0)
    def _():
        run(CTX)

    @pl.when(t > 0)
    def _():
        run(NT)


def _pair_attention(q, k, v, *, wide, q_pairs_per_kv):
    bsz = q.shape[0]
    qw = 256 if wide else LANES
    n_pairs = q.shape[2] // qw
    kv_idx = lambda b, p, t: (b, 0, p // q_pairs_per_kv)
    return pl.pallas_call(
        functools.partial(_pair_attn_kernel, wide=wide),
        grid=(bsz, n_pairs, NTILES),
        in_specs=[pl.BlockSpec((1, TM, qw), lambda b, p, t: (b, t, p)),
                  pl.BlockSpec((1, NT, qw), kv_idx),
                  pl.BlockSpec((1, NT, LANES), kv_idx)],
        out_specs=pl.BlockSpec((1, TM, LANES), lambda b, p, t: (b, t, p)),
        out_shape=jax.ShapeDtypeStruct((bsz, NT, n_pairs * LANES), BF16),
        compiler_params=_cparams(3), name="pair_attention",
    )(q, k, v)


def _na_attn_kernel(q_ref, kc_ref, k0_ref, k1_ref, k2_ref, vc_ref, v0_ref, v1_ref, v2_ref, bias_ref, o_ref):
    t = pl.program_id(1)
    qa, qb = _split_heads(q_ref[0])

    @pl.when(t == 0)
    def _():
        k = kc_ref[0]
        o_ref[0] = _softmax_pv(_dot_t(qa, k), _dot_t(qb, k), vc_ref[0]).astype(BF16)

    @pl.when(t > 0)
    def _():
        k = jnp.concatenate([k0_ref[0], k1_ref[0], k2_ref[0], kc_ref[0]], axis=0)
        v = jnp.concatenate([v0_ref[0], v1_ref[0], v2_ref[0], vc_ref[0]], axis=0)
        sa = _dot_t(qa, k) + bias_ref[0, 0]
        sb = _dot_t(qb, k) + bias_ref[0, 1]
        o_ref[0] = _softmax_pv(sa, sb, v).astype(BF16)


def _na_band_tile(t):
    return 1 + jnp.clip(t - 2, 0, LAT_TILES - BAND_TILES)


def _na_attention(q, k, v, bias):
    bsz = q.shape[0]
    n_pairs = C_HEADS // 2
    qspec = pl.BlockSpec((1, TM, LANES), lambda p, t, b: (b, t, p))
    cspec = pl.BlockSpec((1, TM, LANES), lambda p, t, b: (b, 0, p))
    bands = [pl.BlockSpec((1, TM, LANES), functools.partial(lambda p, t, b, i: (b, _na_band_tile(t) + i, p), i=i))
             for i in range(BAND_TILES)]
    variant = lambda t: jnp.where(t <= 1, 0, jnp.where(t == LAT_TILES, 2, 1))
    return pl.pallas_call(
        _na_attn_kernel,
        grid=(n_pairs, NTILES, bsz),
        in_specs=[qspec, cspec] + bands + [cspec] + bands
                 + [pl.BlockSpec((1, 2, TM, (BAND_TILES + 1) * TM), lambda p, t, b: (variant(t), p, 0, 0))],
        out_specs=qspec,
        out_shape=jax.ShapeDtypeStruct((bsz, NT, n_pairs * LANES), BF16),
        compiler_params=_cparams(3), name="na_attention",
    )(q, k, k, k, k, v, v, v, v, bias)


def _out_router_kernel(o_ref, wo_ref, x_ref, mod_ref, g2_ref, r_ref, xo_ref, h_ref, aff_ref):
    mod = mod_ref[0, 0]
    x = x_ref[0] + mod[2:3] * _dot(o_ref[0], wo_ref[...])
    xo_ref[0] = x
    h = _prenorm(x, g2_ref[...], mod[3:4], mod[4:5]).astype(BF16)
    h_ref[0] = h
    logits = _dot(h, r_ref[...])
    lane = lax.broadcasted_iota(jnp.int32, (1, LANES), 1)
    logits = jnp.where(lane < N_EXPERTS, logits, NEG_INF)
    e = jnp.exp(logits - jnp.max(logits, axis=-1, keepdims=True))
    aff_ref[0] = e / jnp.sum(e, axis=-1, keepdims=True)


def _out_router(o, wo, x, modtab, g2, router):
    bsz = x.shape[0]
    return pl.pallas_call(
        _out_router_kernel,
        grid=(bsz, NTILES),
        in_specs=[_row_spec(D), _const_spec((D, D)), _row_spec(D), _mod_spec(), _const_spec((1, D)),
                  _const_spec((D, LANES))],
        out_specs=[_row_spec(D), _row_spec(D), _row_spec(LANES)],
        out_shape=[jax.ShapeDtypeStruct((bsz, NT, D), F32), jax.ShapeDtypeStruct((bsz, NT, D), BF16),
                   jax.ShapeDtypeStruct((bsz, NT, LANES), F32)],
        input_output_aliases={2: 0},
        compiler_params=_cparams(2), name="out_router",
    )(o, wo, x, modtab, g2, router)


def _excl_cumsum_rows(m, tri):
    out = []
    offset = jnp.zeros((1, LANES), F32)
    for i in range(m.shape[0] // TM):
        blk = m[i * TM:(i + 1) * TM]
        out.append(_dot(tri, blk.astype(BF16)) + offset)
        offset = offset + jnp.sum(blk, axis=0, keepdims=True)
    return jnp.concatenate(out, axis=0) if len(out) > 1 else out[0]


def _topk_slots(aff, cap, tri):
    bits = pltpu.bitcast(aff, jnp.int32)

    def step(i, thr):
        cand = thr | jnp.left_shift(jnp.int32(1), 30 - i)
        cnt = jnp.sum(jnp.where(bits >= cand, 1.0, 0.0), axis=0, keepdims=True)
        return jnp.where(cnt >= cap, cand, thr)

    thr = lax.fori_loop(0, 31, step, jnp.zeros((1, LANES), jnp.int32))
    gt = bits > thr
    eq = bits == thr
    need = cap - jnp.sum(jnp.where(gt, 1.0, 0.0), axis=0, keepdims=True)
    eq_rank = _excl_cumsum_rows(jnp.where(eq, 1.0, 0.0), tri)
    sel = gt | (eq & (eq_rank < need))
    slot = _excl_cumsum_rows(jnp.where(sel, 1.0, 0.0), tri)
    return jnp.where(sel, slot, -1.0).astype(jnp.int32)


def _topk_kernel(aff_ref, slot_ref):
    r = lax.broadcasted_iota(jnp.int32, (TM, TM), 0)
    c = lax.broadcasted_iota(jnp.int32, (TM, TM), 1)
    tri = jnp.where(c < r, 1.0, 0.0).astype(BF16)
    slot_ref[0, :CTX] = _topk_slots(aff_ref[0, :CTX], CAP_CTX, tri)
    slot_ref[0, CTX:] = _topk_slots(aff_ref[0, CTX:], CAP_LAT, tri)


def _topk(aff):
    bsz = aff.shape[0]
    spec = pl.BlockSpec((1, NT, LANES), lambda b: (b, 0, 0))
    return pl.pallas_call(
        _topk_kernel, grid=(bsz,), in_specs=[spec], out_specs=spec,
        out_shape=jax.ShapeDtypeStruct((bsz, NT, LANES), jnp.int32),
        compiler_params=_cparams(1), name="expert_topk",
    )(aff)


def _gather_rows(slot_row, aff_row, h, cap):
    n = slot_row.shape[1]
    onehot = lax.broadcasted_iota(jnp.int32, (cap, n), 0) == slot_row
    xg = _dot(jnp.where(onehot, 1.0, 0.0).astype(BF16), h).astype(BF16)
    gate = jnp.sum(jnp.where(onehot, aff_row, 0.0), axis=1, keepdims=True)
    return xg, gate


def _expert_kernel(h_ref, slot_ref, aff_ref, wg_ref, wu_ref, wd_ref, *out_refs, with_ctx):
    slot, aff = slot_ref[0, 0], aff_ref[0, 0]
    xg, gate = _gather_rows(slot[:, CTX:], aff[:, CTX:], h_ref[0, CTX:], CAP_LAT)
    if with_ctx:
        xc, gc = _gather_rows(slot[:, :CTX], aff[:, :CTX], h_ref[0, :CTX], CAP_CTX)
        xg = jnp.concatenate([xg, xc], axis=0)
        gate = jnp.concatenate([gate, gc], axis=0)
    hid = (_silu(_dot(xg, wg_ref[0])) * _dot(xg, wu_ref[0])).astype(BF16)
    y = _dot(hid, wd_ref[0]) * gate
    out_refs[0][0] = y[:CAP_LAT].astype(BF16)
    if with_ctx:
        out_refs[1][0] = y[CAP_LAT:].astype(BF16)


def _experts(h, slot_t, aff_t, wg, wu, wd, with_ctx):
    bsz = h.shape[0]
    wspec = pl.BlockSpec((1, D, EXPERT_FF), lambda e, b: (e, 0, 0))
    rspec = pl.BlockSpec((1, 1, 1, NT), lambda e, b: (b, e, 0, 0))
    out_specs = [pl.BlockSpec((1, CAP_LAT, D), lambda e, b: (b, e, 0))]
    out_shape = [jax.ShapeDtypeStruct((bsz, N_EXPERTS * CAP_LAT, D), BF16)]
    if with_ctx:
        out_specs.append(pl.BlockSpec((1, CAP_CTX, D), lambda e, b: (b, e, 0)))
        out_shape.append(jax.ShapeDtypeStruct((bsz, N_EXPERTS * CAP_CTX, D), BF16))
    return pl.pallas_call(
        functools.partial(_expert_kernel, with_ctx=with_ctx),
        grid=(N_EXPERTS, bsz),
        in_specs=[pl.BlockSpec((1, NT, D), lambda e, b: (b, 0, 0)), rspec, rspec, wspec, wspec,
                  pl.BlockSpec((1, EXPERT_FF, D), lambda e, b: (e, 0, 0))],
        out_specs=out_specs, out_shape=out_shape,
        compiler_params=_cparams(2), name="experts",
    )(h, slot_t, aff_t, wg, wu, wd)


def _combine_lat(slot, y, x, gate):
    cols = lax.broadcasted_iota(jnp.int32, (TM, CAP_LAT), 1)
    onehot = jnp.concatenate(
        [jnp.where(slot[:, e:e + 1] == cols, 1.0, 0.0).astype(BF16) for e in range(N_EXPERTS)], axis=1)
    return x + gate * _dot(onehot, y)


def _combine_kernel(slot_ref, yl_ref, yc_ref, x_ref, mod_ref, xo_ref):
    t = pl.program_id(1)
    gate = mod_ref[0, 0][5:6]

    @pl.when(t == 0)
    def _():
        slot = slot_ref[0]
        cols = lax.broadcasted_iota(jnp.int32, (TM, N_EXPERTS * CAP_CTX), 1)
        hit = cols < 0
        for e in range(N_EXPERTS):
            s = slot[:, e:e + 1]
            hit = hit | ((s >= 0) & (s + e * CAP_CTX == cols))
        xo_ref[0] = x_ref[0] + gate * _dot(jnp.where(hit, 1.0, 0.0).astype(BF16), yc_ref[0])

    @pl.when(t > 0)
    def _():
        xo_ref[0] = _combine_lat(slot_ref[0], yl_ref[0], x_ref[0], gate)


def _combine(slot, yl, yc, x, modtab):
    bsz = x.shape[0]
    return pl.pallas_call(
        _combine_kernel,
        grid=(bsz, NTILES),
        in_specs=[_row_spec(LANES), pl.BlockSpec((1, N_EXPERTS * CAP_LAT, D), lambda b, t: (b, 0, 0)),
                  pl.BlockSpec((1, N_EXPERTS * CAP_CTX, D), lambda b, t: (b, 0, 0)), _row_spec(D), _mod_spec()],
        out_specs=_row_spec(D),
        out_shape=jax.ShapeDtypeStruct((bsz, NT, D), F32),
        input_output_aliases={3: 0},
        compiler_params=_cparams(2), name="moe_combine",
    )(slot, yl, yc, x, modtab)


def _combine_last_kernel(slot_ref, yl_ref, x_ref, mod_ref, xo_ref):
    xo_ref[0] = _combine_lat(slot_ref[0], yl_ref[0], x_ref[0], mod_ref[0, 0][5:6])


def _combine_last(slot, yl, x, modtab):
    bsz = x.shape[0]
    lat = lambda width: pl.BlockSpec((1, TM, width), lambda b, t: (b, t + 1, 0))
    return pl.pallas_call(
        _combine_last_kernel,
        grid=(bsz, LAT_TILES),
        in_specs=[lat(LANES), pl.BlockSpec((1, N_EXPERTS * CAP_LAT, D), lambda b, t: (b, 0, 0)), lat(D),
                  pl.BlockSpec((1, 1, 6, D), lambda b, t: (b, 1, 0, 0))],
        out_specs=pl.BlockSpec((1, TM, D), lambda b, t: (b, t, 0)),
        out_shape=jax.ShapeDtypeStruct((bsz, SEQ, D), F32),
        compiler_params=_cparams(2), name="moe_combine_last",
    )(slot, yl, x, modtab)


def _rope_tables(rot_dim, width, lane_lo, lane_hi):
    n_freq = rot_dim // 4
    half = rot_dim // 2
    inv = jnp.float32(ROPE_THETA) ** (-jnp.arange(n_freq, dtype=F32) / n_freq)
    t = jnp.arange(SEQ, dtype=jnp.int32)
    row = (t // GRID_W).astype(F32)
    col = (t % GRID_W).astype(F32)
    ang = jnp.concatenate([row[:, None] * inv, col[:, None] * inv], axis=-1)
    cos, sin = jnp.cos(ang), jnp.sin(ang)
    lane = np.arange(width)
    inside = (lane % LANES >= lane_lo) & (lane % LANES < lane_hi)
    idx = lane % half
    sign = np.where(lane % rot_dim < half, -1.0, 1.0).astype(np.float32)
    cosw = jnp.where(inside[None, :], cos[:, idx], 1.0)
    sinw = jnp.where(inside[None, :], sin[:, idx] * sign[None, :], 0.0)
    ones = jnp.ones((CTX, width), F32)
    return jnp.concatenate([ones, cosw], axis=0), jnp.concatenate([0.0 * ones, sinw], axis=0)


def _block_diag_ones():
    i = np.arange(256)
    return jnp.asarray((i[:, None] // 64) == (i[None, :] // 64), dtype=BF16)


_GQA_ORDER = np.array([8 * kp + 4 * odd + i for kp in range(2) for i in range(4) for odd in range(2)])


def _na_bias(rpb):
    out = []
    rows = SEQ // GRID_W
    for r0, bs in ((0, 0), (8, 4), (rows - 4, rows - 12)):
        qi = np.arange(TM)
        kj = np.arange(BAND_TILES * TM)
        r, c = r0 + qi // GRID_W, qi % GRID_W
        kr, kc = bs + kj // GRID_W, kj % GRID_W
        rs = np.clip(r - NA_ROWS // 2, 0, rows - NA_ROWS)
        cs = np.clip(c - NA_COLS // 2, 0, GRID_W - NA_COLS)
        ok = ((kr[None, :] >= rs[:, None]) & (kr[None, :] < rs[:, None] + NA_ROWS)
              & (kc[None, :] >= cs[:, None]) & (kc[None, :] < cs[:, None] + NA_COLS))
        dr = np.clip(kr[None, :] - r[:, None] + NA_ROWS - 1, 0, 2 * NA_ROWS - 2)
        dc = np.clip(kc[None, :] - c[:, None] + NA_COLS - 1, 0, 2 * NA_COLS - 2)
        band = jnp.where(ok[None], rpb.astype(F32)[:, dr, dc], NEG_INF)
        out.append(jnp.concatenate([band, jnp.zeros((C_HEADS, TM, TM), F32)], axis=-1))
    return jnp.stack(out)


def _gqa_params(w_qkv, qn, kn, wo):
    nq, nk = A_HEADS * A_HEAD_DIM, A_KV_HEADS * A_HEAD_DIM
    wq = w_qkv[:, :nq].reshape(D, A_HEADS, A_HEAD_DIM)[:, _GQA_ORDER].reshape(D, nq)
    w = jnp.concatenate([wq, w_qkv[:, nq:]], axis=1).astype(BF16)
    wo_p = wo.reshape(A_HEADS, A_HEAD_DIM, D)[_GQA_ORDER].reshape(nq, D).astype(BF16)
    scale = A_HEAD_DIM ** -0.5
    return w, (jnp.tile(qn, 4) * scale)[None], jnp.tile(kn, 4)[None], wo_p


def _mla_params(w_dq, qn_lat, w_uq, w_dkv, kvn_lat, w_ukv, qn, kn, cossin_q, cossin_k):
    qd = B_NOPE + B_ROPE
    w1 = jnp.concatenate([w_dq, w_dkv, jnp.zeros((D, 768 - B_Q_RANK - B_KV_RANK - B_ROPE), F32)], axis=1)
    pad_q = jnp.zeros((B_Q_RANK, B_HEADS, LANES - qd), F32)
    wuq = jnp.concatenate([w_uq.reshape(B_Q_RANK, B_HEADS, qd), pad_q], axis=2).reshape(B_Q_RANK, B_HEADS * LANES)
    ukv = w_ukv.reshape(B_KV_RANK, B_HEADS, B_NOPE + B_V)
    wk_top = jnp.concatenate([ukv[:, :, :B_NOPE], jnp.zeros((B_KV_RANK, B_HEADS, LANES - B_NOPE), F32)], axis=2)
    route = np.zeros((LANES, B_HEADS, LANES), np.float32)
    route[np.arange(B_ROPE), :, B_NOPE + np.arange(B_ROPE)] = 1.0
    wk = jnp.concatenate([wk_top.reshape(B_KV_RANK, -1), jnp.asarray(route).reshape(LANES, -1)], axis=0)
    wv = ukv[:, :, B_NOPE:].reshape(B_KV_RANK, B_HEADS * B_V)
    scale = qd ** -0.5
    zpad = jnp.zeros((LANES - qd,), F32)
    qgain = jnp.tile(jnp.concatenate([qn * scale, zpad]), 2)[None]
    qinv = jnp.tile(jnp.concatenate([jnp.full((B_NOPE,), 1.0 / B_NOPE), jnp.full((LANES - B_NOPE,), 1.0 / B_ROPE)]), 2)
    kgain = jnp.tile(jnp.concatenate([kn[:B_NOPE], jnp.zeros((LANES - B_NOPE,), F32)]), 2)[None]
    kmask = jnp.tile(jnp.concatenate([jnp.ones((B_NOPE,), F32), jnp.zeros((LANES - B_NOPE,), F32)]), 2)[None]
    gkr = jnp.concatenate([kn[B_NOPE:], jnp.zeros((LANES - B_ROPE,), F32)])[None]
    return dict(w1=w1.astype(BF16), gql=qn_lat[None], gkl=kvn_lat[None], gkr=gkr, kcos=cossin_k[0], ksin=cossin_k[1],
                wuq=wuq.astype(BF16), wk=wk.astype(BF16), wv=wv.astype(BF16), bd=_block_diag_ones(),
                qgain=qgain, qinv=qinv[None].astype(F32), qcos=cossin_q[0], qsin=cossin_q[1], kgain=kgain, kmask=kmask)


def kernel(x, c, ctx, c_ctx, ada_w, ada_b, norm1_w, norm2_w, a_wqkv, a_qnorm, a_knorm, a_wo, b_wdq, b_qnorm_lat, b_wuq, b_wdkv, b_kvnorm_lat, b_wukv, b_qnorm, b_knorm, b_wo, c_wqkv, c_qnorm, c_knorm, c_rpb, c_wo, moe_router, moe_wg, moe_wu, moe_wd):
    bsz = x.shape[0]
    assert x.shape[1:] == (SEQ, D) and ctx.shape[1:] == (CTX, D)
    mod_rows = -(-(bsz + 1) // 16) * 16
    cvec = jnp.concatenate([c, c_ctx[None], jnp.zeros((mod_rows - bsz - 1, D), F32)], axis=0)
    mods = _ada_all(cvec, ada_w, ada_b)
    xs = jnp.concatenate([ctx, x], axis=1)

    bd = _block_diag_ones()
    rope64 = _rope_tables(A_HEAD_DIM, 256, 0, LANES)
    rope_mla_q = _rope_tables(B_ROPE, 256, B_NOPE, B_NOPE + B_ROPE)
    rope_mla_k = _rope_tables(B_ROPE, LANES, 0, B_ROPE)
    router = jnp.pad(moe_router, ((0, 0), (0, 0), (0, LANES - N_EXPERTS))).astype(BF16)

    for i in range(DEPTH):
        last = i == DEPTH - 1
        kind, j = i % 3, i // 3
        m = mods[i]
        m_lat = m[:bsz].reshape(bsz, 6, D)
        m_ctx = jnp.broadcast_to(m[bsz].reshape(1, 6, D), (bsz, 6, D))
        modtab = jnp.stack([m_ctx, m_lat], axis=1)
        g1, g2 = norm1_w[i][None], norm2_w[i][None]

        if kind == 0:
            w, gq, gk, wo = _gqa_params(a_wqkv[j], a_qnorm[j], a_knorm[j], a_wo[j])
            q, k, v = _qkv_project(xs, modtab, g1, w, bd, gq, gk, rope64[0], rope64[1],
                                   A_HEADS * A_HEAD_DIM, A_KV_HEADS * A_HEAD_DIM, True)
            o = _pair_attention(q, k, v, wide=False, q_pairs_per_kv=4)
        elif kind == 1:
            p = _mla_params(b_wdq[j], b_qnorm_lat[j], b_wuq[j], b_wdkv[j], b_kvnorm_lat[j], b_wukv[j],
                            b_qnorm[j], b_knorm[j], rope_mla_q, rope_mla_k)
            q, k, v = _mla_project(xs, modtab, g1, p)
            o = _pair_attention(q, k, v, wide=True, q_pairs_per_kv=1)
            wo = b_wo[j].astype(BF16)
        else:
            hd = C_HEADS * C_HEAD_DIM
            gq = (jnp.tile(c_qnorm[j], 4) * C_HEAD_DIM ** -0.5)[None]
            gk = jnp.tile(c_knorm[j], 4)[None]
            q, k, v = _qkv_project(xs, modtab, g1, c_wqkv[j].astype(BF16), bd, gq, gk, rope64[0], rope64[1],
                                   hd, hd, False)
            o = _na_attention(q, k, v, _na_bias(c_rpb[j]))
            wo = c_wo[j].astype(BF16)

        xs, h, aff = _out_router(o, wo, xs, modtab, g2, router[i])
        slot = _topk(aff)
        slot_t = jnp.swapaxes(slot[:, :, :N_EXPERTS], 1, 2)[:, :, None, :]
        aff_t = jnp.swapaxes(aff[:, :, :N_EXPERTS], 1, 2)[:, :, None, :]
        ys = _experts(h, slot_t, aff_t, moe_wg[i].astype(BF16), moe_wu[i].astype(BF16), moe_wd[i].astype(BF16),
                      with_ctx=not last)
        if last:
            return _combine_last(slot, ys[0], xs, modtab)
        xs = _combine(slot, ys[0], ys[1], xs, modtab)
```

```python
import functools

import numpy as np
import jax
import jax.numpy as jnp
from jax import lax
from jax.experimental import pallas as pl
from jax.experimental.pallas import tpu as pltpu

F32 = jnp.float32
BF16 = jnp.bfloat16

D = 1024
SEQ = 2048
CTX = 256
NT = CTX + SEQ
DEPTH = 4
GRID_W = 64
ROPE_THETA = 10000.0
EPS = 1e-6
NEG_INF = -1e30
LOG2E = 1.4426950408889634

TM = 256
NTILES = NT // TM
LAT_TILES = SEQ // TM

A_HEADS, A_KV_HEADS, A_HEAD_DIM = 16, 4, 64
B_HEADS, B_Q_RANK, B_KV_RANK, B_NOPE, B_ROPE, B_V = 16, 384, 256, 64, 32, 64
C_HEADS, C_HEAD_DIM, NA_ROWS, NA_COLS = 16, 64, 8, 16
N_EXPERTS, EXPERT_FF = 16, 1024
CAP_LAT = 2 * SEQ // N_EXPERTS
CAP_CTX = 2 * CTX // N_EXPERTS
LANES = 128
BAND_TILES = 3

VMEM_LIMIT = 56 * 1024 * 1024


def _cparams(n_axes):
    return pltpu.CompilerParams(dimension_semantics=("arbitrary",) * n_axes, vmem_limit_bytes=VMEM_LIMIT)


def _dot(a, b):
    return jnp.dot(a, b, preferred_element_type=F32)


def _dot_t(a, b):
    return lax.dot_general(a, b, (((1,), (1,)), ((), ())), preferred_element_type=F32)


def _silu(x):
    return x / (1.0 + jnp.exp(-x))


def _prenorm(x, gain, shift, scale):
    ms = jnp.mean(x * x, axis=-1, keepdims=True)
    return (x * lax.rsqrt(ms + EPS) * gain) * (1.0 + scale) + shift


def _seg_rms(z, bd, inv_n):
    z2 = z * z
    hi = z2.astype(BF16)
    lo = (z2 - hi.astype(F32)).astype(BF16)
    ss = _dot(hi, bd) + _dot(lo, bd)
    return lax.rsqrt(ss * inv_n + EPS)


def _rope(z, cosw, sinw, first_half, half):
    w = z.shape[-1]
    up = pltpu.roll(z, w - half, axis=1)
    dn = pltpu.roll(z, half, axis=1)
    return z * cosw + jnp.where(first_half, up, dn) * sinw


def _ada_kernel(c_ref, w_ref, b_ref, o_ref):
    s = _silu(c_ref[...]).astype(BF16)
    o_ref[0] = _dot(s, w_ref[0].astype(BF16)) + b_ref[0]


def _ada_all(cvec, ada_w, ada_b):
    rows = cvec.shape[0]
    tn = 512
    return pl.pallas_call(
        _ada_kernel,
        grid=(DEPTH, 6 * D // tn),
        in_specs=[pl.BlockSpec((rows, D), lambda i, j: (0, 0)),
                  pl.BlockSpec((1, D, tn), lambda i, j: (i, 0, j)),
                  pl.BlockSpec((1, 1, tn), lambda i, j: (i, 0, j))],
        out_specs=pl.BlockSpec((1, rows, tn), lambda i, j: (i, 0, j)),
        out_shape=jax.ShapeDtypeStruct((DEPTH, rows, 6 * D), F32),
        compiler_params=_cparams(2), name="adaln",
    )(cvec, ada_w, ada_b.reshape(DEPTH, 1, 6 * D))


def _mod_spec():
    return pl.BlockSpec((1, 1, 6, D), lambda b, t: (b, jnp.minimum(t, 1), 0, 0))


def _row_spec(width):
    return pl.BlockSpec((1, TM, width), lambda b, t: (b, t, 0))


def _const_spec(shape):
    return pl.BlockSpec(shape, lambda b, t: (0,) * len(shape))


def _tab_spec(width):
    return pl.BlockSpec((TM, width), lambda b, t: (t, 0))


def _qkv_kernel(x_ref, mod_ref, g1_ref, w_ref, bd_ref, gq_ref, gk_ref, cos_ref, sin_ref,
                q_ref, k_ref, v_ref, *, nq, nk, rope):
    mod = mod_ref[0, 0]
    h = _prenorm(x_ref[0], g1_ref[...], mod[0:1], mod[1:2]).astype(BF16)
    acc = _dot(h, w_ref[...])
    bd = bd_ref[...]
    lane = lax.broadcasted_iota(jnp.int32, (1, 256), 1)
    first_half = (lane % 64) < 32
    cosw, sinw = cos_ref[...], sin_ref[...]

    def finish(z, gain):
        z = z * _seg_rms(z, bd, 1.0 / 64) * gain
        if rope:
            z = _rope(z, cosw, sinw, first_half, 32)
        return z.astype(BF16)

    for j in range(nq // 256):
        q_ref[0, :, j * 256:(j + 1) * 256] = finish(acc[:, j * 256:(j + 1) * 256], gq_ref[...])
    for j in range(nk // 256):
        k_ref[0, :, j * 256:(j + 1) * 256] = finish(acc[:, nq + j * 256:nq + (j + 1) * 256], gk_ref[...])
    v_ref[0] = acc[:, nq + nk:].astype(BF16)


def _qkv_project(x, modtab, g1, w, bd, gq, gk, cosw, sinw, nq, nk, rope):
    bsz = x.shape[0]
    return pl.pallas_call(
        functools.partial(_qkv_kernel, nq=nq, nk=nk, rope=rope),
        grid=(bsz, NTILES),
        in_specs=[_row_spec(D), _mod_spec(), _const_spec((1, D)), _const_spec((D, nq + 2 * nk)),
                  _const_spec((256, 256)), _const_spec((1, 256)), _const_spec((1, 256)),
                  _tab_spec(256), _tab_spec(256)],
        out_specs=[_row_spec(nq), _row_spec(nk), _row_spec(nk)],
        out_shape=[jax.ShapeDtypeStruct((bsz, NT, nq), BF16), jax.ShapeDtypeStruct((bsz, NT, nk), BF16),
                   jax.ShapeDtypeStruct((bsz, NT, nk), BF16)],
        compiler_params=_cparams(2), name="qkv_project",
    )(x, modtab, g1, w, bd, gq, gk, cosw, sinw)


def _mla_kernel(x_ref, mod_ref, g1_ref, w1_ref, gql_ref, gkl_ref, gkr_ref, kcos_ref, ksin_ref,
                wuq_ref, wk_ref, wv_ref, bd_ref, qgain_ref, qinv_ref, qcos_ref, qsin_ref, kgain_ref, kmask_ref,
                q_ref, k_ref, v_ref):
    mod = mod_ref[0, 0]
    h = _prenorm(x_ref[0], g1_ref[...], mod[0:1], mod[1:2]).astype(BF16)
    a = _dot(h, w1_ref[...])
    cq = a[:, :B_Q_RANK]
    cq = (cq * lax.rsqrt(jnp.mean(cq * cq, axis=-1, keepdims=True) + EPS) * gql_ref[...]).astype(BF16)
    ckv = a[:, B_Q_RANK:B_Q_RANK + B_KV_RANK]
    ckv = (ckv * lax.rsqrt(jnp.mean(ckv * ckv, axis=-1, keepdims=True) + EPS) * gkl_ref[...]).astype(BF16)
    kr = a[:, B_Q_RANK + B_KV_RANK:]
    kr = kr * lax.rsqrt(jnp.sum(kr * kr, axis=-1, keepdims=True) * (1.0 / B_ROPE) + EPS) * gkr_ref[...]
    lane128 = lax.broadcasted_iota(jnp.int32, (1, LANES), 1)
    kr = _rope(kr, kcos_ref[...], ksin_ref[...], (lane128 % 32) < 16, 16).astype(BF16)

    bd = bd_ref[...]
    lane = lax.broadcasted_iota(jnp.int32, (1, 256), 1)
    first_half = (lane % 32) < 16
    qcos, qsin = qcos_ref[...], qsin_ref[...]
    q2 = _dot(cq, wuq_ref[...])
    k2 = _dot(jnp.concatenate([ckv, kr], axis=1), wk_ref[...])
    kmask = kmask_ref[...] > 0.0
    for j in range(q2.shape[1] // 256):
        sl = slice(j * 256, (j + 1) * 256)
        z = q2[:, sl]
        z = z * _seg_rms(z, bd, qinv_ref[...]) * qgain_ref[...]
        q_ref[0, :, sl] = _rope(z, qcos, qsin, first_half, 16).astype(BF16)
        z = k2[:, sl]
        k_ref[0, :, sl] = (z * jnp.where(kmask, _seg_rms(z, bd, 1.0 / 64) * kgain_ref[...], 1.0)).astype(BF16)
    v_ref[0] = _dot(ckv, wv_ref[...]).astype(BF16)


def _mla_project(x, modtab, g1, p):
    bsz = x.shape[0]
    hq = B_HEADS * LANES
    return pl.pallas_call(
        _mla_kernel,
        grid=(bsz, NTILES),
        in_specs=[_row_spec(D), _mod_spec(), _const_spec((1, D)), _const_spec((D, 768)),
                  _const_spec((1, B_Q_RANK)), _const_spec((1, B_KV_RANK)), _const_spec((1, LANES)),
                  _tab_spec(LANES), _tab_spec(LANES),
                  _const_spec((B_Q_RANK, hq)), _const_spec((B_KV_RANK + LANES, hq)),
                  _const_spec((B_KV_RANK, B_HEADS * B_V)), _const_spec((256, 256)),
                  _const_spec((1, 256)), _const_spec((1, 256)), _tab_spec(256), _tab_spec(256),
                  _const_spec((1, 256)), _const_spec((1, 256))],
        out_specs=[_row_spec(hq), _row_spec(hq), _row_spec(B_HEADS * B_V)],
        out_shape=[jax.ShapeDtypeStruct((bsz, NT, hq), BF16), jax.ShapeDtypeStruct((bsz, NT, hq), BF16),
                   jax.ShapeDtypeStruct((bsz, NT, B_HEADS * B_V), BF16)],
        compiler_params=_cparams(2), name="mla_project",
    )(x, modtab, g1, p["w1"], p["gql"], p["gkl"], p["gkr"], p["kcos"], p["ksin"], p["wuq"], p["wk"], p["wv"],
      p["bd"], p["qgain"], p["qinv"], p["qcos"], p["qsin"], p["kgain"], p["kmask"])


def _softmax_pv(sa, sb, v):
    outs = []
    for s in (sa, sb):
        p = jnp.exp2(s - jnp.max(s, axis=-1, keepdims=True))
        l = jnp.sum(p, axis=-1, keepdims=True)
        outs.append(_dot(p.astype(BF16), v) / l)
    lane = lax.broadcasted_iota(jnp.int32, (1, LANES), 1)
    return jnp.where(lane < 64, outs[0], outs[1])


def _split_heads(q):
    lane = lax.broadcasted_iota(jnp.int32, (1, LANES), 1)
    zero = jnp.zeros_like(q)
    return jnp.where(lane < 64, q, zero), jnp.where(lane >= 64, q, zero)


PAIRS_PER_STEP = 2


def _pair_attn_kernel(q_ref, k_ref, v_ref, o_ref, *, wide):
    t = pl.program_id(2)

    def run(nkeys):
        for i in range(PAIRS_PER_STEP):
            if wide:
                q = q_ref[0, :, 256 * i:256 * (i + 1)]
                k = k_ref[0, :nkeys, 256 * i:256 * (i + 1)]
                sa, sb = _dot_t(q[:, :LANES], k[:, :LANES]), _dot_t(q[:, LANES:], k[:, LANES:])
                v = v_ref[0, :nkeys, LANES * i:LANES * (i + 1)]
            else:
                qa, qb = _split_heads(q_ref[0, :, LANES * i:LANES * (i + 1)])
                k = k_ref[0, :nkeys]
                sa, sb = _dot_t(qa, k), _dot_t(qb, k)
                v = v_ref[0, :nkeys]
            o_ref[0, :, LANES * i:LANES * (i + 1)] = _softmax_pv(sa, sb, v).astype(BF16)

    @pl.when(t == 0)
    def _():
        run(CTX)

    @pl.when(t > 0)
    def _():
        run(NT)


def _pair_attention(q, k, v, *, wide, q_pairs_per_kv):
    bsz = q.shape[0]
    qw = 256 if wide else LANES
    n_steps = q.shape[2] // (qw * PAIRS_PER_STEP)
    kv_pairs = PAIRS_PER_STEP if wide else 1
    kv_idx = lambda b, p, t: (b, 0, p * PAIRS_PER_STEP // (q_pairs_per_kv * kv_pairs))
    return pl.pallas_call(
        functools.partial(_pair_attn_kernel, wide=wide),
        grid=(bsz, n_steps, NTILES),
        in_specs=[pl.BlockSpec((1, TM, qw * PAIRS_PER_STEP), lambda b, p, t: (b, t, p)),
                  pl.BlockSpec((1, NT, qw * kv_pairs), kv_idx),
                  pl.BlockSpec((1, NT, LANES * kv_pairs), kv_idx)],
        out_specs=pl.BlockSpec((1, TM, LANES * PAIRS_PER_STEP), lambda b, p, t: (b, t, p)),
        out_shape=jax.ShapeDtypeStruct((bsz, NT, n_steps * PAIRS_PER_STEP * LANES), BF16),
        compiler_params=_cparams(3), name="pair_attention",
    )(q, k, v)


def _na_attn_kernel(q_ref, kc_ref, k0_ref, k1_ref, k2_ref, vc_ref, v0_ref, v1_ref, v2_ref, bias_ref, o_ref):
    t = pl.program_id(1)

    @pl.when(t == 0)
    def _():
        for i in range(PAIRS_PER_STEP):
            sl = slice(LANES * i, LANES * (i + 1))
            qa, qb = _split_heads(q_ref[0, :, sl])
            k = kc_ref[0, :, sl]
            o_ref[0, :, sl] = _softmax_pv(_dot_t(qa, k), _dot_t(qb, k), vc_ref[0, :, sl]).astype(BF16)

    @pl.when(t > 0)
    def _():
        for i in range(PAIRS_PER_STEP):
            sl = slice(LANES * i, LANES * (i + 1))
            qa, qb = _split_heads(q_ref[0, :, sl])
            k = jnp.concatenate([k0_ref[0, :, sl], k1_ref[0, :, sl], k2_ref[0, :, sl], kc_ref[0, :, sl]], axis=0)
            v = jnp.concatenate([v0_ref[0, :, sl], v1_ref[0, :, sl], v2_ref[0, :, sl], vc_ref[0, :, sl]], axis=0)
            sa = _dot_t(qa, k) + bias_ref[0, 2 * i]
            sb = _dot_t(qb, k) + bias_ref[0, 2 * i + 1]
            o_ref[0, :, sl] = _softmax_pv(sa, sb, v).astype(BF16)


def _na_band_tile(t):
    return 1 + jnp.clip(t - 2, 0, LAT_TILES - BAND_TILES)


def _na_attention(q, k, v, bias):
    bsz = q.shape[0]
    n_steps = C_HEADS // (2 * PAIRS_PER_STEP)
    width = LANES * PAIRS_PER_STEP
    qspec = pl.BlockSpec((1, TM, width), lambda p, t, b: (b, t, p))
    cspec = pl.BlockSpec((1, TM, width), lambda p, t, b: (b, 0, p))
    bands = [pl.BlockSpec((1, TM, width), functools.partial(lambda p, t, b, i: (b, _na_band_tile(t) + i, p), i=i))
             for i in range(BAND_TILES)]
    variant = lambda t: jnp.where(t <= 1, 0, jnp.where(t == LAT_TILES, 2, 1))
    return pl.pallas_call(
        _na_attn_kernel,
        grid=(n_steps, NTILES, bsz),
        in_specs=[qspec, cspec] + bands + [cspec] + bands
                 + [pl.BlockSpec((1, 2 * PAIRS_PER_STEP, TM, (BAND_TILES + 1) * TM),
                                 lambda p, t, b: (variant(t), p, 0, 0))],
        out_specs=qspec,
        out_shape=jax.ShapeDtypeStruct((bsz, NT, C_HEADS * C_HEAD_DIM), BF16),
        compiler_params=_cparams(3), name="na_attention",
    )(q, k, k, k, k, v, v, v, v, bias)


def _out_router_kernel(o_ref, wo_ref, x_ref, mod_ref, g2_ref, r_ref, xo_ref, h_ref, aff_ref):
    mod = mod_ref[0, 0]
    x = x_ref[0] + mod[2:3] * _dot(o_ref[0], wo_ref[...])
    xo_ref[0] = x
    h = _prenorm(x, g2_ref[...], mod[3:4], mod[4:5]).astype(BF16)
    h_ref[0] = h
    logits = _dot(h, r_ref[...])
    lane = lax.broadcasted_iota(jnp.int32, (1, LANES), 1)
    logits = jnp.where(lane < N_EXPERTS, logits, NEG_INF)
    e = jnp.exp(logits - jnp.max(logits, axis=-1, keepdims=True))
    aff_ref[0] = e / jnp.sum(e, axis=-1, keepdims=True)


def _out_router(o, wo, x, modtab, g2, router):
    bsz = x.shape[0]
    return pl.pallas_call(
        _out_router_kernel,
        grid=(bsz, NTILES),
        in_specs=[_row_spec(D), _const_spec((D, D)), _row_spec(D), _mod_spec(), _const_spec((1, D)),
                  _const_spec((D, LANES))],
        out_specs=[_row_spec(D), _row_spec(D), _row_spec(LANES)],
        out_shape=[jax.ShapeDtypeStruct((bsz, NT, D), F32), jax.ShapeDtypeStruct((bsz, NT, D), BF16),
                   jax.ShapeDtypeStruct((bsz, NT, LANES), F32)],
        input_output_aliases={2: 0},
        compiler_params=_cparams(2), name="out_router",
    )(o, wo, x, modtab, g2, router)


def _excl_cumsum_rows(m, tri):
    out = []
    offset = jnp.zeros((1, LANES), F32)
    for i in range(m.shape[0] // TM):
        blk = m[i * TM:(i + 1) * TM]
        out.append(_dot(tri, blk.astype(BF16)) + offset)
        offset = offset + jnp.sum(blk, axis=0, keepdims=True)
    return jnp.concatenate(out, axis=0) if len(out) > 1 else out[0]


def _topk_slots(aff, cap, tri):
    bits = pltpu.bitcast(aff, jnp.int32)

    def step(i, thr):
        cand = thr | jnp.left_shift(jnp.int32(1), 30 - i)
        cnt = jnp.sum(jnp.where(bits >= cand, 1.0, 0.0), axis=0, keepdims=True)
        return jnp.where(cnt >= cap, cand, thr)

    thr = lax.fori_loop(0, 31, step, jnp.zeros((1, LANES), jnp.int32))
    gt = bits > thr
    eq = bits == thr
    need = cap - jnp.sum(jnp.where(gt, 1.0, 0.0), axis=0, keepdims=True)
    eq_rank = _excl_cumsum_rows(jnp.where(eq, 1.0, 0.0), tri)
    sel = gt | (eq & (eq_rank < need))
    slot = _excl_cumsum_rows(jnp.where(sel, 1.0, 0.0), tri)
    return jnp.where(sel, slot, -1.0).astype(jnp.int32)


def _topk_kernel(aff_ref, slot_ref):
    r = lax.broadcasted_iota(jnp.int32, (TM, TM), 0)
    c = lax.broadcasted_iota(jnp.int32, (TM, TM), 1)
    tri = jnp.where(c < r, 1.0, 0.0).astype(BF16)
    slot_ref[0, :CTX] = _topk_slots(aff_ref[0, :CTX], CAP_CTX, tri)
    slot_ref[0, CTX:] = _topk_slots(aff_ref[0, CTX:], CAP_LAT, tri)


def _topk(aff):
    bsz = aff.shape[0]
    spec = pl.BlockSpec((1, NT, LANES), lambda b: (b, 0, 0))
    return pl.pallas_call(
        _topk_kernel, grid=(bsz,), in_specs=[spec], out_specs=spec,
        out_shape=jax.ShapeDtypeStruct((bsz, NT, LANES), jnp.int32),
        compiler_params=_cparams(1), name="expert_topk",
    )(aff)


def _gather_rows(slot_row, aff_row, h, cap):
    n = slot_row.shape[1]
    onehot = lax.broadcasted_iota(jnp.int32, (cap, n), 0) == slot_row
    xg = _dot(jnp.where(onehot, 1.0, 0.0).astype(BF16), h).astype(BF16)
    gate = jnp.sum(jnp.where(onehot, aff_row, 0.0), axis=1, keepdims=True)
    return xg, gate


def _expert_kernel(h_ref, slot_ref, aff_ref, wg_ref, wu_ref, wd_ref, *out_refs, with_ctx):
    slot, aff = slot_ref[0, 0], aff_ref[0, 0]
    xg, gate = _gather_rows(slot[:, CTX:], aff[:, CTX:], h_ref[0, CTX:], CAP_LAT)
    if with_ctx:
        xc, gc = _gather_rows(slot[:, :CTX], aff[:, :CTX], h_ref[0, :CTX], CAP_CTX)
        xg = jnp.concatenate([xg, xc], axis=0)
        gate = jnp.concatenate([gate, gc], axis=0)
    hid = (_silu(_dot(xg, wg_ref[0])) * _dot(xg, wu_ref[0])).astype(BF16)
    y = _dot(hid, wd_ref[0]) * gate
    out_refs[0][0] = y[:CAP_LAT].astype(BF16)
    if with_ctx:
        out_refs[1][0] = y[CAP_LAT:].astype(BF16)


def _experts(h, slot_t, aff_t, wg, wu, wd, with_ctx):
    bsz = h.shape[0]
    wspec = pl.BlockSpec((1, D, EXPERT_FF), lambda e, b: (e, 0, 0))
    rspec = pl.BlockSpec((1, 1, 1, NT), lambda e, b: (b, e, 0, 0))
    out_specs = [pl.BlockSpec((1, CAP_LAT, D), lambda e, b: (b, e, 0))]
    out_shape = [jax.ShapeDtypeStruct((bsz, N_EXPERTS * CAP_LAT, D), BF16)]
    if with_ctx:
        out_specs.append(pl.BlockSpec((1, CAP_CTX, D), lambda e, b: (b, e, 0)))
        out_shape.append(jax.ShapeDtypeStruct((bsz, N_EXPERTS * CAP_CTX, D), BF16))
    return pl.pallas_call(
        functools.partial(_expert_kernel, with_ctx=with_ctx),
        grid=(N_EXPERTS, bsz),
        in_specs=[pl.BlockSpec((1, NT, D), lambda e, b: (b, 0, 0)), rspec, rspec, wspec, wspec,
                  pl.BlockSpec((1, EXPERT_FF, D), lambda e, b: (e, 0, 0))],
        out_specs=out_specs, out_shape=out_shape,
        compiler_params=_cparams(2), name="experts",
    )(h, slot_t, aff_t, wg, wu, wd)


def _combine_lat(slot, y, x, gate):
    cols = lax.broadcasted_iota(jnp.int32, (TM, CAP_LAT), 1)
    onehot = jnp.concatenate(
        [jnp.where(slot[:, e:e + 1] == cols, 1.0, 0.0).astype(BF16) for e in range(N_EXPERTS)], axis=1)
    return x + gate * _dot(onehot, y)


def _combine_kernel(slot_ref, yl_ref, yc_ref, x_ref, mod_ref, xo_ref):
    t = pl.program_id(1)
    gate = mod_ref[0, 0][5:6]

    @pl.when(t == 0)
    def _():
        slot = slot_ref[0]
        cols = lax.broadcasted_iota(jnp.int32, (TM, N_EXPERTS * CAP_CTX), 1)
        hit = cols < 0
        for e in range(N_EXPERTS):
            s = slot[:, e:e + 1]
            hit = hit | ((s >= 0) & (s + e * CAP_CTX == cols))
        xo_ref[0] = x_ref[0] + gate * _dot(jnp.where(hit, 1.0, 0.0).astype(BF16), yc_ref[0])

    @pl.when(t > 0)
    def _():
        xo_ref[0] = _combine_lat(slot_ref[0], yl_ref[0], x_ref[0], gate)


def _combine(slot, yl, yc, x, modtab):
    bsz = x.shape[0]
    return pl.pallas_call(
        _combine_kernel,
        grid=(bsz, NTILES),
        in_specs=[_row_spec(LANES), pl.BlockSpec((1, N_EXPERTS * CAP_LAT, D), lambda b, t: (b, 0, 0)),
                  pl.BlockSpec((1, N_EXPERTS * CAP_CTX, D), lambda b, t: (b, 0, 0)), _row_spec(D), _mod_spec()],
        out_specs=_row_spec(D),
        out_shape=jax.ShapeDtypeStruct((bsz, NT, D), F32),
        input_output_aliases={3: 0},
        compiler_params=_cparams(2), name="moe_combine",
    )(slot, yl, yc, x, modtab)


def _combine_last_kernel(slot_ref, yl_ref, x_ref, mod_ref, xo_ref):
    xo_ref[0] = _combine_lat(slot_ref[0], yl_ref[0], x_ref[0], mod_ref[0, 0][5:6])


def _combine_last(slot, yl, x, modtab):
    bsz = x.shape[0]
    lat = lambda width: pl.BlockSpec((1, TM, width), lambda b, t: (b, t + 1, 0))
    return pl.pallas_call(
        _combine_last_kernel,
        grid=(bsz, LAT_TILES),
        in_specs=[lat(LANES), pl.BlockSpec((1, N_EXPERTS * CAP_LAT, D), lambda b, t: (b, 0, 0)), lat(D),
                  pl.BlockSpec((1, 1, 6, D), lambda b, t: (b, 1, 0, 0))],
        out_specs=pl.BlockSpec((1, TM, D), lambda b, t: (b, t, 0)),
        out_shape=jax.ShapeDtypeStruct((bsz, SEQ, D), F32),
        compiler_params=_cparams(2), name="moe_combine_last",
    )(slot, yl, x, modtab)


def _rope_tables(rot_dim, width, lane_lo, lane_hi):
    n_freq = rot_dim // 4
    half = rot_dim // 2
    inv = jnp.float32(ROPE_THETA) ** (-jnp.arange(n_freq, dtype=F32) / n_freq)
    t = jnp.arange(SEQ, dtype=jnp.int32)
    row = (t // GRID_W).astype(F32)
    col = (t % GRID_W).astype(F32)
    ang = jnp.concatenate([row[:, None] * inv, col[:, None] * inv], axis=-1)
    cos, sin = jnp.cos(ang), jnp.sin(ang)
    lane = np.arange(width)
    inside = (lane % LANES >= lane_lo) & (lane % LANES < lane_hi)
    idx = lane % half
    sign = np.where(lane % rot_dim < half, -1.0, 1.0).astype(np.float32)
    cosw = jnp.where(inside[None, :], cos[:, idx], 1.0)
    sinw = jnp.where(inside[None, :], sin[:, idx] * sign[None, :], 0.0)
    ones = jnp.ones((CTX, width), F32)
    return jnp.concatenate([ones, cosw], axis=0), jnp.concatenate([0.0 * ones, sinw], axis=0)


def _block_diag_ones():
    i = np.arange(256)
    return jnp.asarray((i[:, None] // 64) == (i[None, :] // 64), dtype=BF16)


_GQA_ORDER = np.array([8 * kp + 4 * odd + i for kp in range(2) for i in range(4) for odd in range(2)])


def _na_bias(rpb):
    out = []
    rows = SEQ // GRID_W
    qrows, krows = TM // GRID_W, BAND_TILES * TM // GRID_W
    col = np.arange(GRID_W)
    cs = np.clip(col - NA_COLS // 2, 0, GRID_W - NA_COLS)
    ok_c = (col[None, :] >= cs[:, None]) & (col[None, :] < cs[:, None] + NA_COLS)
    dc = np.clip(col[None, :] - col[:, None] + NA_COLS - 1, 0, 2 * NA_COLS - 2)
    pick_c = (dc[:, :, None] == np.arange(2 * NA_COLS - 1)).astype(np.float32)
    for r0, bs in ((0, 0), (8, 4), (rows - 4, rows - 12)):
        r, kr = r0 + np.arange(qrows), bs + np.arange(krows)
        rs = np.clip(r - NA_ROWS // 2, 0, rows - NA_ROWS)
        ok_r = (kr[None, :] >= rs[:, None]) & (kr[None, :] < rs[:, None] + NA_ROWS)
        dr = kr[None, :] - r[:, None] + NA_ROWS - 1
        pick_r = (dr[:, :, None] == np.arange(2 * NA_ROWS - 1)).astype(np.float32)
        t1 = jnp.einsum('ard,hde->hare', pick_r, rpb.astype(F32), precision=lax.Precision.HIGHEST)
        band = jnp.einsum('hare,cse->hacrs', t1, pick_c, precision=lax.Precision.HIGHEST)
        ok = ok_r[:, None, :, None] & ok_c[None, :, None, :]
        band = jnp.where(ok[None], band * LOG2E, NEG_INF).reshape(C_HEADS, TM, BAND_TILES * TM)
        out.append(jnp.concatenate([band, jnp.zeros((C_HEADS, TM, TM), F32)], axis=-1))
    return jnp.stack(out)


def _gqa_params(w_qkv, qn, kn, wo):
    nq, nk = A_HEADS * A_HEAD_DIM, A_KV_HEADS * A_HEAD_DIM
    wq = w_qkv[:, :nq].reshape(D, A_HEADS, A_HEAD_DIM)[:, _GQA_ORDER].reshape(D, nq)
    w = jnp.concatenate([wq, w_qkv[:, nq:]], axis=1).astype(BF16)
    wo_p = wo.reshape(A_HEADS, A_HEAD_DIM, D)[_GQA_ORDER].reshape(nq, D).astype(BF16)
    scale = A_HEAD_DIM ** -0.5 * LOG2E
    return w, (jnp.tile(qn, 4) * scale)[None], jnp.tile(kn, 4)[None], wo_p


def _mla_params(w_dq, qn_lat, w_uq, w_dkv, kvn_lat, w_ukv, qn, kn, cossin_q, cossin_k):
    qd = B_NOPE + B_ROPE
    w1 = jnp.concatenate([w_dq, w_dkv, jnp.zeros((D, 768 - B_Q_RANK - B_KV_RANK - B_ROPE), F32)], axis=1)
    pad_q = jnp.zeros((B_Q_RANK, B_HEADS, LANES - qd), F32)
    wuq = jnp.concatenate([w_uq.reshape(B_Q_RANK, B_HEADS, qd), pad_q], axis=2).reshape(B_Q_RANK, B_HEADS * LANES)
    ukv = w_ukv.reshape(B_KV_RANK, B_HEADS, B_NOPE + B_V)
    wk_top = jnp.concatenate([ukv[:, :, :B_NOPE], jnp.zeros((B_KV_RANK, B_HEADS, LANES - B_NOPE), F32)], axis=2)
    route = np.zeros((LANES, B_HEADS, LANES), np.float32)
    route[np.arange(B_ROPE), :, B_NOPE + np.arange(B_ROPE)] = 1.0
    wk = jnp.concatenate([wk_top.reshape(B_KV_RANK, -1), jnp.asarray(route).reshape(LANES, -1)], axis=0)
    wv = ukv[:, :, B_NOPE:].reshape(B_KV_RANK, B_HEADS * B_V)
    scale = qd ** -0.5 * LOG2E
    zpad = jnp.zeros((LANES - qd,), F32)
    qgain = jnp.tile(jnp.concatenate([qn * scale, zpad]), 2)[None]
    qinv = jnp.tile(jnp.concatenate([jnp.full((B_NOPE,), 1.0 / B_NOPE), jnp.full((LANES - B_NOPE,), 1.0 / B_ROPE)]), 2)
    kgain = jnp.tile(jnp.concatenate([kn[:B_NOPE], jnp.zeros((LANES - B_NOPE,), F32)]), 2)[None]
    kmask = jnp.tile(jnp.concatenate([jnp.ones((B_NOPE,), F32), jnp.zeros((LANES - B_NOPE,), F32)]), 2)[None]
    gkr = jnp.concatenate([kn[B_NOPE:], jnp.zeros((LANES - B_ROPE,), F32)])[None]
    return dict(w1=w1.astype(BF16), gql=qn_lat[None], gkl=kvn_lat[None], gkr=gkr, kcos=cossin_k[0], ksin=cossin_k[1],
                wuq=wuq.astype(BF16), wk=wk.astype(BF16), wv=wv.astype(BF16), bd=_block_diag_ones(),
                qgain=qgain, qinv=qinv[None].astype(F32), qcos=cossin_q[0], qsin=cossin_q[1], kgain=kgain, kmask=kmask)


def kernel(x, c, ctx, c_ctx, ada_w, ada_b, norm1_w, norm2_w, a_wqkv, a_qnorm, a_knorm, a_wo, b_wdq, b_qnorm_lat, b_wuq, b_wdkv, b_kvnorm_lat, b_wukv, b_qnorm, b_knorm, b_wo, c_wqkv, c_qnorm, c_knorm, c_rpb, c_wo, moe_router, moe_wg, moe_wu, moe_wd):
    bsz = x.shape[0]
    assert x.shape[1:] == (SEQ, D) and ctx.shape[1:] == (CTX, D)
    mod_rows = -(-(bsz + 1) // 16) * 16
    cvec = jnp.concatenate([c, c_ctx[None], jnp.zeros((mod_rows - bsz - 1, D), F32)], axis=0)
    mods = _ada_all(cvec, ada_w, ada_b)
    xs = jnp.concatenate([ctx, x], axis=1)

    bd = _block_diag_ones()
    rope64 = _rope_tables(A_HEAD_DIM, 256, 0, LANES)
    rope_mla_q = _rope_tables(B_ROPE, 256, B_NOPE, B_NOPE + B_ROPE)
    rope_mla_k = _rope_tables(B_ROPE, LANES, 0, B_ROPE)
    router = jnp.pad(moe_router, ((0, 0), (0, 0), (0, LANES - N_EXPERTS))).astype(BF16)

    for i in range(DEPTH):
        last = i == DEPTH - 1
        kind, j = i % 3, i // 3
        m = mods[i]
        m_lat = m[:bsz].reshape(bsz, 6, D)
        m_ctx = jnp.broadcast_to(m[bsz].reshape(1, 6, D), (bsz, 6, D))
        modtab = jnp.stack([m_ctx, m_lat], axis=1)
        g1, g2 = norm1_w[i][None], norm2_w[i][None]

        if kind == 0:
            w, gq, gk, wo = _gqa_params(a_wqkv[j], a_qnorm[j], a_knorm[j], a_wo[j])
            q, k, v = _qkv_project(xs, modtab, g1, w, bd, gq, gk, rope64[0], rope64[1],
                                   A_HEADS * A_HEAD_DIM, A_KV_HEADS * A_HEAD_DIM, True)
            o = _pair_attention(q, k, v, wide=False, q_pairs_per_kv=4)
        elif kind == 1:
            p = _mla_params(b_wdq[j], b_qnorm_lat[j], b_wuq[j], b_wdkv[j], b_kvnorm_lat[j], b_wukv[j],
                            b_qnorm[j], b_knorm[j], rope_mla_q, rope_mla_k)
            q, k, v = _mla_project(xs, modtab, g1, p)
            o = _pair_attention(q, k, v, wide=True, q_pairs_per_kv=1)
            wo = b_wo[j].astype(BF16)
        else:
            hd = C_HEADS * C_HEAD_DIM
            gq = (jnp.tile(c_qnorm[j], 4) * (C_HEAD_DIM ** -0.5 * LOG2E))[None]
            gk = jnp.tile(c_knorm[j], 4)[None]
            q, k, v = _qkv_project(xs, modtab, g1, c_wqkv[j].astype(BF16), bd, gq, gk, rope64[0], rope64[1],
                                   hd, hd, False)
            o = _na_attention(q, k, v, _na_bias(c_rpb[j]))
            wo = c_wo[j].astype(BF16)

        xs, h, aff = _out_router(o, wo, xs, modtab, g2, router[i])
        slot = _topk(aff)
        slot_t = jnp.swapaxes(slot[:, :, :N_EXPERTS], 1, 2)[:, :, None, :]
        aff_t = jnp.swapaxes(aff[:, :, :N_EXPERTS], 1, 2)[:, :, None, :]
        ys = _experts(h, slot_t, aff_t, moe_wg[i].astype(BF16), moe_wu[i].astype(BF16), moe_wd[i].astype(BF16),
                      with_ctx=not last)
        if last:
            return _combine_last(slot, ys[0], xs, modtab)
        xs = _combine(slot, ys[0], ys[1], xs, modtab)
```

```python
import functools

import numpy as np
import jax
import jax.numpy as jnp
from jax import lax
from jax.experimental import pallas as pl
from jax.experimental.pallas import tpu as pltpu

F32 = jnp.float32
BF16 = jnp.bfloat16

D = 1024
SEQ = 2048
CTX = 256
NT = CTX + SEQ
DEPTH = 4
GRID_W = 64
ROPE_THETA = 10000.0
EPS = 1e-6
NEG_INF = -1e30
LOG2E = 1.4426950408889634

TM = 256
NTILES = NT // TM
LAT_TILES = SEQ // TM

A_HEADS, A_KV_HEADS, A_HEAD_DIM = 16, 4, 64
B_HEADS, B_Q_RANK, B_KV_RANK, B_NOPE, B_ROPE, B_V = 16, 384, 256, 64, 32, 64
C_HEADS, C_HEAD_DIM, NA_ROWS, NA_COLS = 16, 64, 8, 16
N_EXPERTS, EXPERT_FF = 16, 1024
CAP_LAT = 2 * SEQ // N_EXPERTS
CAP_CTX = 2 * CTX // N_EXPERTS
CAP_ALL = CAP_LAT + CAP_CTX
HS_ROWS = D // 256
TILE_STRIDE = 296
LANES = 128
BAND_TILES = 3

VMEM_LIMIT = 56 * 1024 * 1024


def _cparams(n_axes):
    return pltpu.CompilerParams(dimension_semantics=("arbitrary",) * n_axes, vmem_limit_bytes=VMEM_LIMIT)


def _dot(a, b):
    return jnp.dot(a, b, preferred_element_type=F32)


def _dot_t(a, b):
    return lax.dot_general(a, b, (((1,), (1,)), ((), ())), preferred_element_type=F32)


def _silu(x):
    return x / (1.0 + jnp.exp(-x))


def _prenorm(x, gain, shift, scale):
    ms = jnp.mean(x * x, axis=-1, keepdims=True)
    return (x * lax.rsqrt(ms + EPS) * gain) * (1.0 + scale) + shift


def _seg_rms(z, bd, inv_n):
    z2 = z * z
    hi = z2.astype(BF16)
    lo = (z2 - hi.astype(F32)).astype(BF16)
    ss = _dot(hi, bd) + _dot(lo, bd)
    return lax.rsqrt(ss * inv_n + EPS)


def _rope(z, cosw, sinw, first_half, half):
    w = z.shape[-1]
    up = pltpu.roll(z, w - half, axis=1)
    dn = pltpu.roll(z, half, axis=1)
    return z * cosw + jnp.where(first_half, up, dn) * sinw


def _ada_kernel(c_ref, w_ref, b_ref, o_ref):
    s = _silu(c_ref[...]).astype(BF16)
    o_ref[0] = _dot(s, w_ref[0].astype(BF16)) + b_ref[0]


def _ada_all(cvec, ada_w, ada_b):
    rows = cvec.shape[0]
    tn = 512
    return pl.pallas_call(
        _ada_kernel,
        grid=(DEPTH, 6 * D // tn),
        in_specs=[pl.BlockSpec((rows, D), lambda i, j: (0, 0)),
                  pl.BlockSpec((1, D, tn), lambda i, j: (i, 0, j)),
                  pl.BlockSpec((1, 1, tn), lambda i, j: (i, 0, j))],
        out_specs=pl.BlockSpec((1, rows, tn), lambda i, j: (i, 0, j)),
        out_shape=jax.ShapeDtypeStruct((DEPTH, rows, 6 * D), F32),
        compiler_params=_cparams(2), name="adaln",
    )(cvec, ada_w, ada_b.reshape(DEPTH, 1, 6 * D))


def _mod_spec():
    return pl.BlockSpec((1, 1, 6, D), lambda b, t: (b, jnp.minimum(t, 1), 0, 0))


def _row_spec(width):
    return pl.BlockSpec((1, TM, width), lambda b, t: (b, t, 0))


def _const_spec(shape):
    return pl.BlockSpec(shape, lambda b, t: (0,) * len(shape))


def _tab_spec(width):
    return pl.BlockSpec((TM, width), lambda b, t: (t, 0))


def _qkv_kernel(x_ref, mod_ref, g1_ref, w_ref, bd_ref, gq_ref, gk_ref, cos_ref, sin_ref,
                q_ref, k_ref, v_ref, *, nq, nk, rope):
    mod = mod_ref[0, 0]
    h = _prenorm(x_ref[0], g1_ref[...], mod[0:1], mod[1:2]).astype(BF16)
    acc = _dot(h, w_ref[...])
    bd = bd_ref[...]
    lane = lax.broadcasted_iota(jnp.int32, (1, 256), 1)
    first_half = (lane % 64) < 32
    cosw, sinw = cos_ref[...], sin_ref[...]

    def finish(z, gain):
        z = z * _seg_rms(z, bd, 1.0 / 64) * gain
        if rope:
            z = _rope(z, cosw, sinw, first_half, 32)
        return z.astype(BF16)

    for j in range(nq // 256):
        q_ref[0, :, j * 256:(j + 1) * 256] = finish(acc[:, j * 256:(j + 1) * 256], gq_ref[...])
    for j in range(nk // 256):
        k_ref[0, :, j * 256:(j + 1) * 256] = finish(acc[:, nq + j * 256:nq + (j + 1) * 256], gk_ref[...])
    v_ref[0] = acc[:, nq + nk:].astype(BF16)


def _qkv_project(x, modtab, g1, w, bd, gq, gk, cosw, sinw, nq, nk, rope):
    bsz = x.shape[0]
    return pl.pallas_call(
        functools.partial(_qkv_kernel, nq=nq, nk=nk, rope=rope),
        grid=(bsz, NTILES),
        in_specs=[_row_spec(D), _mod_spec(), _const_spec((1, D)), _const_spec((D, nq + 2 * nk)),
                  _const_spec((256, 256)), _const_spec((1, 256)), _const_spec((1, 256)),
                  _tab_spec(256), _tab_spec(256)],
        out_specs=[_row_spec(nq), _row_spec(nk), _row_spec(nk)],
        out_shape=[jax.ShapeDtypeStruct((bsz, NT, nq), BF16), jax.ShapeDtypeStruct((bsz, NT, nk), BF16),
                   jax.ShapeDtypeStruct((bsz, NT, nk), BF16)],
        compiler_params=_cparams(2), name="qkv_project",
    )(x, modtab, g1, w, bd, gq, gk, cosw, sinw)


def _mla_kernel(x_ref, mod_ref, g1_ref, w1_ref, gql_ref, gkl_ref, gkr_ref, kcos_ref, ksin_ref,
                wuq_ref, wk_ref, wv_ref, bd_ref, qgain_ref, qinv_ref, qcos_ref, qsin_ref, kgain_ref, kmask_ref,
                q_ref, k_ref, v_ref):
    mod = mod_ref[0, 0]
    h = _prenorm(x_ref[0], g1_ref[...], mod[0:1], mod[1:2]).astype(BF16)
    a = _dot(h, w1_ref[...])
    cq = a[:, :B_Q_RANK]
    cq = (cq * lax.rsqrt(jnp.mean(cq * cq, axis=-1, keepdims=True) + EPS) * gql_ref[...]).astype(BF16)
    ckv = a[:, B_Q_RANK:B_Q_RANK + B_KV_RANK]
    ckv = (ckv * lax.rsqrt(jnp.mean(ckv * ckv, axis=-1, keepdims=True) + EPS) * gkl_ref[...]).astype(BF16)
    kr = a[:, B_Q_RANK + B_KV_RANK:]
    kr = kr * lax.rsqrt(jnp.sum(kr * kr, axis=-1, keepdims=True) * (1.0 / B_ROPE) + EPS) * gkr_ref[...]
    lane128 = lax.broadcasted_iota(jnp.int32, (1, LANES), 1)
    kr = _rope(kr, kcos_ref[...], ksin_ref[...], (lane128 % 32) < 16, 16).astype(BF16)

    bd = bd_ref[...]
    lane = lax.broadcasted_iota(jnp.int32, (1, 256), 1)
    first_half = (lane % 32) < 16
    qcos, qsin = qcos_ref[...], qsin_ref[...]
    q2 = _dot(cq, wuq_ref[...])
    k2 = _dot(jnp.concatenate([ckv, kr], axis=1), wk_ref[...])
    kmask = kmask_ref[...] > 0.0
    for j in range(q2.shape[1] // 256):
        sl = slice(j * 256, (j + 1) * 256)
        z = q2[:, sl]
        z = z * _seg_rms(z, bd, qinv_ref[...]) * qgain_ref[...]
        q_ref[0, :, sl] = _rope(z, qcos, qsin, first_half, 16).astype(BF16)
        z = k2[:, sl]
        k_ref[0, :, sl] = (z * jnp.where(kmask, _seg_rms(z, bd, 1.0 / 64) * kgain_ref[...], 1.0)).astype(BF16)
    v_ref[0] = _dot(ckv, wv_ref[...]).astype(BF16)


def _mla_project(x, modtab, g1, p):
    bsz = x.shape[0]
    hq = B_HEADS * LANES
    return pl.pallas_call(
        _mla_kernel,
        grid=(bsz, NTILES),
        in_specs=[_row_spec(D), _mod_spec(), _const_spec((1, D)), _const_spec((D, 768)),
                  _const_spec((1, B_Q_RANK)), _const_spec((1, B_KV_RANK)), _const_spec((1, LANES)),
                  _tab_spec(LANES), _tab_spec(LANES),
                  _const_spec((B_Q_RANK, hq)), _const_spec((B_KV_RANK + LANES, hq)),
                  _const_spec((B_KV_RANK, B_HEADS * B_V)), _const_spec((256, 256)),
                  _const_spec((1, 256)), _const_spec((1, 256)), _tab_spec(256), _tab_spec(256),
                  _const_spec((1, 256)), _const_spec((1, 256))],
        out_specs=[_row_spec(hq), _row_spec(hq), _row_spec(B_HEADS * B_V)],
        out_shape=[jax.ShapeDtypeStruct((bsz, NT, hq), BF16), jax.ShapeDtypeStruct((bsz, NT, hq), BF16),
                   jax.ShapeDtypeStruct((bsz, NT, B_HEADS * B_V), BF16)],
        compiler_params=_cparams(2), name="mla_project",
    )(x, modtab, g1, p["w1"], p["gql"], p["gkl"], p["gkr"], p["kcos"], p["ksin"], p["wuq"], p["wk"], p["wv"],
      p["bd"], p["qgain"], p["qinv"], p["qcos"], p["qsin"], p["kgain"], p["kmask"])


def _softmax_pv(sa, sb, v):
    outs = []
    for s in (sa, sb):
        p = jnp.exp2(s - jnp.max(s, axis=-1, keepdims=True))
        l = jnp.sum(p, axis=-1, keepdims=True)
        outs.append(_dot(p.astype(BF16), v) / l)
    lane = lax.broadcasted_iota(jnp.int32, (1, LANES), 1)
    return jnp.where(lane < 64, outs[0], outs[1])


def _split_heads(q):
    lane = lax.broadcasted_iota(jnp.int32, (1, LANES), 1)
    zero = jnp.zeros_like(q)
    return jnp.where(lane < 64, q, zero), jnp.where(lane >= 64, q, zero)


PAIRS_PER_STEP = 2


def _pair_attn_kernel(q_ref, k_ref, v_ref, o_ref, *, wide):
    t = pl.program_id(2)

    def run(nkeys):
        for i in range(PAIRS_PER_STEP):
            if wide:
                q = q_ref[0, :, 256 * i:256 * (i + 1)]
                k = k_ref[0, :nkeys, 256 * i:256 * (i + 1)]
                sa, sb = _dot_t(q[:, :LANES], k[:, :LANES]), _dot_t(q[:, LANES:], k[:, LANES:])
                v = v_ref[0, :nkeys, LANES * i:LANES * (i + 1)]
            else:
                qa, qb = _split_heads(q_ref[0, :, LANES * i:LANES * (i + 1)])
                k = k_ref[0, :nkeys]
                sa, sb = _dot_t(qa, k), _dot_t(qb, k)
                v = v_ref[0, :nkeys]
            o_ref[0, :, LANES * i:LANES * (i + 1)] = _softmax_pv(sa, sb, v).astype(BF16)

    @pl.when(t == 0)
    def _():
        run(CTX)

    @pl.when(t > 0)
    def _():
        run(NT)


def _pair_attention(q, k, v, *, wide, q_pairs_per_kv):
    bsz = q.shape[0]
    qw = 256 if wide else LANES
    n_steps = q.shape[2] // (qw * PAIRS_PER_STEP)
    kv_pairs = PAIRS_PER_STEP if wide else 1
    kv_idx = lambda b, p, t: (b, 0, p * PAIRS_PER_STEP // (q_pairs_per_kv * kv_pairs))
    return pl.pallas_call(
        functools.partial(_pair_attn_kernel, wide=wide),
        grid=(bsz, n_steps, NTILES),
        in_specs=[pl.BlockSpec((1, TM, qw * PAIRS_PER_STEP), lambda b, p, t: (b, t, p)),
                  pl.BlockSpec((1, NT, qw * kv_pairs), kv_idx),
                  pl.BlockSpec((1, NT, LANES * kv_pairs), kv_idx)],
        out_specs=pl.BlockSpec((1, TM, LANES * PAIRS_PER_STEP), lambda b, p, t: (b, t, p)),
        out_shape=jax.ShapeDtypeStruct((bsz, NT, n_steps * PAIRS_PER_STEP * LANES), BF16),
        compiler_params=_cparams(3), name="pair_attention",
    )(q, k, v)


def _na_attn_kernel(q_ref, kc_ref, k0_ref, k1_ref, k2_ref, vc_ref, v0_ref, v1_ref, v2_ref, bias_ref, o_ref):
    t = pl.program_id(1)

    @pl.when(t == 0)
    def _():
        for i in range(PAIRS_PER_STEP):
            sl = slice(LANES * i, LANES * (i + 1))
            qa, qb = _split_heads(q_ref[0, :, sl])
            k = kc_ref[0, :, sl]
            o_ref[0, :, sl] = _softmax_pv(_dot_t(qa, k), _dot_t(qb, k), vc_ref[0, :, sl]).astype(BF16)

    @pl.when(t > 0)
    def _():
        for i in range(PAIRS_PER_STEP):
            sl = slice(LANES * i, LANES * (i + 1))
            qa, qb = _split_heads(q_ref[0, :, sl])
            k = jnp.concatenate([k0_ref[0, :, sl], k1_ref[0, :, sl], k2_ref[0, :, sl], kc_ref[0, :, sl]], axis=0)
            v = jnp.concatenate([v0_ref[0, :, sl], v1_ref[0, :, sl], v2_ref[0, :, sl], vc_ref[0, :, sl]], axis=0)
            sa = _dot_t(qa, k) + bias_ref[0, 2 * i]
            sb = _dot_t(qb, k) + bias_ref[0, 2 * i + 1]
            o_ref[0, :, sl] = _softmax_pv(sa, sb, v).astype(BF16)


def _na_band_tile(t):
    return 1 + jnp.clip(t - 2, 0, LAT_TILES - BAND_TILES)


def _na_attention(q, k, v, bias):
    bsz = q.shape[0]
    n_steps = C_HEADS // (2 * PAIRS_PER_STEP)
    width = LANES * PAIRS_PER_STEP
    qspec = pl.BlockSpec((1, TM, width), lambda p, t, b: (b, t, p))
    cspec = pl.BlockSpec((1, TM, width), lambda p, t, b: (b, 0, p))
    bands = [pl.BlockSpec((1, TM, width), functools.partial(lambda p, t, b, i: (b, _na_band_tile(t) + i, p), i=i))
             for i in range(BAND_TILES)]
    variant = lambda t: jnp.where(t <= 1, 0, jnp.where(t == LAT_TILES, 2, 1))
    return pl.pallas_call(
        _na_attn_kernel,
        grid=(n_steps, NTILES, bsz),
        in_specs=[qspec, cspec] + bands + [cspec] + bands
                 + [pl.BlockSpec((1, 2 * PAIRS_PER_STEP, TM, (BAND_TILES + 1) * TM),
                                 lambda p, t, b: (variant(t), p, 0, 0))],
        out_specs=qspec,
        out_shape=jax.ShapeDtypeStruct((bsz, NT, C_HEADS * C_HEAD_DIM), BF16),
        compiler_params=_cparams(3), name="na_attention",
    )(q, k, k, k, k, v, v, v, v, bias)


def _out_router_kernel(o_ref, wo_ref, x_ref, mod_ref, g2_ref, r_ref, xo_ref, h_ref, aff_ref):
    mod = mod_ref[0, 0]
    x = x_ref[0] + mod[2:3] * _dot(o_ref[0], wo_ref[...])
    xo_ref[0] = x
    h = _prenorm(x, g2_ref[...], mod[3:4], mod[4:5]).astype(BF16)
    for r, word in enumerate(_pack_rows(h)):
        h_ref[0, pl.ds(r, TM, stride=HS_ROWS), :] = word
    logits = _dot(h, r_ref[...])
    lane = lax.broadcasted_iota(jnp.int32, (1, LANES), 1)
    logits = jnp.where(lane < N_EXPERTS, logits, NEG_INF)
    e = jnp.exp(logits - jnp.max(logits, axis=-1, keepdims=True))
    aff_ref[0] = e / jnp.sum(e, axis=-1, keepdims=True)


def _out_router(o, wo, x, modtab, g2, router):
    bsz = x.shape[0]
    return pl.pallas_call(
        _out_router_kernel,
        grid=(bsz, NTILES),
        in_specs=[_row_spec(D), _const_spec((D, D)), _row_spec(D), _mod_spec(), _const_spec((1, D)),
                  _const_spec((D, LANES))],
        out_specs=[_row_spec(D), pl.BlockSpec((1, TM * HS_ROWS, LANES), lambda b, t: (b, t, 0)), _row_spec(LANES)],
        out_shape=[jax.ShapeDtypeStruct((bsz, NT, D), F32),
                   jax.ShapeDtypeStruct((bsz, NT * HS_ROWS, LANES), jnp.int32),
                   jax.ShapeDtypeStruct((bsz, NT, LANES), F32)],
        input_output_aliases={2: 0},
        compiler_params=_cparams(2), name="out_router",
    )(o, wo, x, modtab, g2, router)


def _excl_cumsum_rows(m, tri):
    out = []
    offset = jnp.zeros((1, LANES), F32)
    for i in range(m.shape[0] // TM):
        blk = m[i * TM:(i + 1) * TM]
        out.append(_dot(tri, blk.astype(BF16)) + offset)
        offset = offset + jnp.sum(blk, axis=0, keepdims=True)
    return jnp.concatenate(out, axis=0) if len(out) > 1 else out[0]


def _topk_slots(aff, cap, tri):
    bits = pltpu.bitcast(aff, jnp.int32)

    def step(i, thr):
        cand = thr | jnp.left_shift(jnp.int32(1), 30 - i)
        cnt = jnp.sum(jnp.where(bits >= cand, 1.0, 0.0), axis=0, keepdims=True)
        return jnp.where(cnt >= cap, cand, thr)

    thr = lax.fori_loop(0, 31, step, jnp.zeros((1, LANES), jnp.int32))
    gt = bits > thr
    eq = bits == thr
    need = cap - jnp.sum(jnp.where(gt, 1.0, 0.0), axis=0, keepdims=True)
    eq_rank = _excl_cumsum_rows(jnp.where(eq, 1.0, 0.0), tri)
    sel = gt | (eq & (eq_rank < need))
    slot = _excl_cumsum_rows(jnp.where(sel, 1.0, 0.0), tri)
    return jnp.where(sel, slot, -1.0).astype(jnp.int32)


def _slot_lists(slot, aff, cap, first_row):
    n = slot.shape[0]
    cols = lax.broadcasted_iota(jnp.int32, (1, cap), 1)
    row4 = ((lax.broadcasted_iota(jnp.int32, (n, 1), 0) + first_row) * HS_ROWS).astype(F32)
    offs, gates = [], []
    for e in range(N_EXPERTS):
        hit = slot[:, e:e + 1] == cols
        offs.append(jnp.sum(jnp.where(hit, row4, 0.0), axis=0, keepdims=True).astype(jnp.int32))
        gates.append(jnp.sum(jnp.where(hit, aff[:, e:e + 1], 0.0), axis=0, keepdims=True))
    return offs, gates


def _topk_kernel(aff_ref, slot_ref, off_ref, gate_ref):
    r = lax.broadcasted_iota(jnp.int32, (TM, TM), 0)
    c = lax.broadcasted_iota(jnp.int32, (TM, TM), 1)
    tri = jnp.where(c < r, 1.0, 0.0).astype(BF16)
    for lo, hi, cap, dst in ((0, CTX, CAP_CTX, CAP_LAT), (CTX, NT, CAP_LAT, 0)):
        slot = _topk_slots(aff_ref[0, lo:hi], cap, tri)
        slot_ref[0, lo:hi] = slot
        offs, gates = _slot_lists(slot, aff_ref[0, lo:hi], cap, lo)
        for e in range(N_EXPERTS):
            off_ref[0, e:e + 1, dst:dst + cap] = offs[e]
            gate_ref[0, e:e + 1, dst:dst + cap] = gates[e]


def _topk(aff):
    bsz = aff.shape[0]
    spec = pl.BlockSpec((1, NT, LANES), lambda b: (b, 0, 0))
    lspec = pl.BlockSpec((1, N_EXPERTS, CAP_ALL), lambda b: (b, 0, 0))
    return pl.pallas_call(
        _topk_kernel, grid=(bsz,), in_specs=[spec], out_specs=[spec, lspec, lspec],
        out_shape=[jax.ShapeDtypeStruct((bsz, NT, LANES), jnp.int32),
                   jax.ShapeDtypeStruct((bsz, N_EXPERTS, CAP_ALL), jnp.int32),
                   jax.ShapeDtypeStruct((bsz, N_EXPERTS, CAP_ALL), F32)],
        compiler_params=_cparams(1), name="expert_topk",
    )(aff)


def _pack_rows(hb):
    words = []
    for r in range(HS_ROWS):
        lo = pltpu.bitcast(hb[:, 256 * r:256 * r + LANES].astype(F32), jnp.int32)
        hi = pltpu.bitcast(hb[:, 256 * r + LANES:256 * (r + 1)].astype(F32), jnp.int32)
        words.append(lax.shift_right_logical(lo, jnp.int32(16)) | hi)
    return words


def _unpack_rows(words):
    cols = []
    for w in words:
        cols.append(pltpu.bitcast(lax.shift_left(w, jnp.int32(16)), F32).astype(BF16))
        cols.append(pltpu.bitcast(w & jnp.int32(-65536), F32).astype(BF16))
    return jnp.concatenate(cols, axis=1)


def _expert_kernel(hs_ref, off_ref, gate_ref, wg_ref, wu_ref, wd_ref, *refs, n_slots):
    out_refs, tile_ref = refs[:-1], refs[-1]
    for j in range(n_slots):
        src = pl.multiple_of(off_ref[0, 0, 0, j], HS_ROWS)
        tile_ref[pl.ds(j, HS_ROWS, stride=TILE_STRIDE), :] = hs_ref[0, pl.ds(src, HS_ROWS), :]
    xg = _unpack_rows([tile_ref[r * TILE_STRIDE:r * TILE_STRIDE + n_slots, :] for r in range(HS_ROWS)])
    hid = (_silu(_dot(xg, wg_ref[0])) * _dot(xg, wu_ref[0])).astype(BF16)
    y = _dot(hid, wd_ref[0]) * gate_ref[0, 0, :n_slots]
    out_refs[0][0] = y[:CAP_LAT].astype(BF16)
    if n_slots > CAP_LAT:
        out_refs[1][0] = y[CAP_LAT:].astype(BF16)


def _experts(hs, off, gate, wg, wu, wd, with_ctx):
    bsz = hs.shape[0]
    n_slots = CAP_ALL if with_ctx else CAP_LAT
    wspec = pl.BlockSpec((1, D, EXPERT_FF), lambda e, b: (e, 0, 0))
    out_specs = [pl.BlockSpec((1, CAP_LAT, D), lambda e, b: (b, e, 0))]
    out_shape = [jax.ShapeDtypeStruct((bsz, N_EXPERTS * CAP_LAT, D), BF16)]
    if with_ctx:
        out_specs.append(pl.BlockSpec((1, CAP_CTX, D), lambda e, b: (b, e, 0)))
        out_shape.append(jax.ShapeDtypeStruct((bsz, N_EXPERTS * CAP_CTX, D), BF16))
    return pl.pallas_call(
        functools.partial(_expert_kernel, n_slots=n_slots),
        grid=(N_EXPERTS, bsz),
        in_specs=[pl.BlockSpec((1, NT * HS_ROWS, LANES), lambda e, b: (b, 0, 0)),
                  pl.BlockSpec((1, 1, 1, CAP_ALL), lambda e, b: (b, e, 0, 0), memory_space=pltpu.SMEM),
                  pl.BlockSpec((1, 1, CAP_ALL, 1), lambda e, b: (b, e, 0, 0)),
                  wspec, wspec, pl.BlockSpec((1, EXPERT_FF, D), lambda e, b: (e, 0, 0))],
        out_specs=out_specs, out_shape=out_shape,
        scratch_shapes=[pltpu.VMEM((HS_ROWS * TILE_STRIDE, LANES), jnp.int32)],
        compiler_params=_cparams(2), name="experts",
    )(hs, off, gate, wg, wu, wd)


def _combine_lat(slot, y, x, gate):
    cols = lax.broadcasted_iota(jnp.int32, (TM, CAP_LAT), 1)
    onehot = jnp.concatenate(
        [jnp.where(slot[:, e:e + 1] == cols, 1.0, 0.0).astype(BF16) for e in range(N_EXPERTS)], axis=1)
    return x + gate * _dot(onehot, y)


def _combine_kernel(slot_ref, yl_ref, yc_ref, x_ref, mod_ref, xo_ref):
    t = pl.program_id(1)
    gate = mod_ref[0, 0][5:6]

    @pl.when(t == 0)
    def _():
        slot = slot_ref[0]
        cols = lax.broadcasted_iota(jnp.int32, (TM, N_EXPERTS * CAP_CTX), 1)
        hit = cols < 0
        for e in range(N_EXPERTS):
            s = slot[:, e:e + 1]
            hit = hit | ((s >= 0) & (s + e * CAP_CTX == cols))
        xo_ref[0] = x_ref[0] + gate * _dot(jnp.where(hit, 1.0, 0.0).astype(BF16), yc_ref[0])

    @pl.when(t > 0)
    def _():
        xo_ref[0] = _combine_lat(slot_ref[0], yl_ref[0], x_ref[0], gate)


def _combine(slot, yl, yc, x, modtab):
    bsz = x.shape[0]
    return pl.pallas_call(
        _combine_kernel,
        grid=(bsz, NTILES),
        in_specs=[_row_spec(LANES), pl.BlockSpec((1, N_EXPERTS * CAP_LAT, D), lambda b, t: (b, 0, 0)),
                  pl.BlockSpec((1, N_EXPERTS * CAP_CTX, D), lambda b, t: (b, 0, 0)), _row_spec(D), _mod_spec()],
        out_specs=_row_spec(D),
        out_shape=jax.ShapeDtypeStruct((bsz, NT, D), F32),
        input_output_aliases={3: 0},
        compiler_params=_cparams(2), name="moe_combine",
    )(slot, yl, yc, x, modtab)


def _combine_last_kernel(slot_ref, yl_ref, x_ref, mod_ref, xo_ref):
    xo_ref[0] = _combine_lat(slot_ref[0], yl_ref[0], x_ref[0], mod_ref[0, 0][5:6])


def _combine_last(slot, yl, x, modtab):
    bsz = x.shape[0]
    lat = lambda width: pl.BlockSpec((1, TM, width), lambda b, t: (b, t + 1, 0))
    return pl.pallas_call(
        _combine_last_kernel,
        grid=(bsz, LAT_TILES),
        in_specs=[lat(LANES), pl.BlockSpec((1, N_EXPERTS * CAP_LAT, D), lambda b, t: (b, 0, 0)), lat(D),
                  pl.BlockSpec((1, 1, 6, D), lambda b, t: (b, 1, 0, 0))],
        out_specs=pl.BlockSpec((1, TM, D), lambda b, t: (b, t, 0)),
        out_shape=jax.ShapeDtypeStruct((bsz, SEQ, D), F32),
        compiler_params=_cparams(2), name="moe_combine_last",
    )(slot, yl, x, modtab)


def _rope_tables(rot_dim, width, lane_lo, lane_hi):
    n_freq = rot_dim // 4
    half = rot_dim // 2
    inv = jnp.float32(ROPE_THETA) ** (-jnp.arange(n_freq, dtype=F32) / n_freq)
    t = jnp.arange(SEQ, dtype=jnp.int32)
    row = (t // GRID_W).astype(F32)
    col = (t % GRID_W).astype(F32)
    ang = jnp.concatenate([row[:, None] * inv, col[:, None] * inv], axis=-1)
    cos, sin = jnp.cos(ang), jnp.sin(ang)
    lane = np.arange(width)
    inside = (lane % LANES >= lane_lo) & (lane % LANES < lane_hi)
    idx = lane % half
    sign = np.where(lane % rot_dim < half, -1.0, 1.0).astype(np.float32)
    cosw = jnp.where(inside[None, :], cos[:, idx], 1.0)
    sinw = jnp.where(inside[None, :], sin[:, idx] * sign[None, :], 0.0)
    ones = jnp.ones((CTX, width), F32)
    return jnp.concatenate([ones, cosw], axis=0), jnp.concatenate([0.0 * ones, sinw], axis=0)


def _block_diag_ones():
    i = np.arange(256)
    return jnp.asarray((i[:, None] // 64) == (i[None, :] // 64), dtype=BF16)


_GQA_ORDER = np.array([8 * kp + 4 * odd + i for kp in range(2) for i in range(4) for odd in range(2)])


def _na_bias(rpb):
    out = []
    rows = SEQ // GRID_W
    qrows, krows = TM // GRID_W, BAND_TILES * TM // GRID_W
    col = np.arange(GRID_W)
    cs = np.clip(col - NA_COLS // 2, 0, GRID_W - NA_COLS)
    ok_c = (col[None, :] >= cs[:, None]) & (col[None, :] < cs[:, None] + NA_COLS)
    dc = np.clip(col[None, :] - col[:, None] + NA_COLS - 1, 0, 2 * NA_COLS - 2)
    pick_c = (dc[:, :, None] == np.arange(2 * NA_COLS - 1)).astype(np.float32)
    for r0, bs in ((0, 0), (8, 4), (rows - 4, rows - 12)):
        r, kr = r0 + np.arange(qrows), bs + np.arange(krows)
        rs = np.clip(r - NA_ROWS // 2, 0, rows - NA_ROWS)
        ok_r = (kr[None, :] >= rs[:, None]) & (kr[None, :] < rs[:, None] + NA_ROWS)
        dr = kr[None, :] - r[:, None] + NA_ROWS - 1
        pick_r = (dr[:, :, None] == np.arange(2 * NA_ROWS - 1)).astype(np.float32)
        t1 = jnp.einsum('ard,hde->hare', pick_r, rpb.astype(F32), precision=lax.Precision.HIGHEST)
        band = jnp.einsum('hare,cse->hacrs', t1, pick_c, precision=lax.Precision.HIGHEST)
        ok = ok_r[:, None, :, None] & ok_c[None, :, None, :]
        band = jnp.where(ok[None], band * LOG2E, NEG_INF).reshape(C_HEADS, TM, BAND_TILES * TM)
        out.append(jnp.concatenate([band, jnp.zeros((C_HEADS, TM, TM), F32)], axis=-1))
    return jnp.stack(out)


def _gqa_params(w_qkv, qn, kn, wo):
    nq, nk = A_HEADS * A_HEAD_DIM, A_KV_HEADS * A_HEAD_DIM
    wq = w_qkv[:, :nq].reshape(D, A_HEADS, A_HEAD_DIM)[:, _GQA_ORDER].reshape(D, nq)
    w = jnp.concatenate([wq, w_qkv[:, nq:]], axis=1).astype(BF16)
    wo_p = wo.reshape(A_HEADS, A_HEAD_DIM, D)[_GQA_ORDER].reshape(nq, D).astype(BF16)
    scale = A_HEAD_DIM ** -0.5 * LOG2E
    return w, (jnp.tile(qn, 4) * scale)[None], jnp.tile(kn, 4)[None], wo_p


def _mla_params(w_dq, qn_lat, w_uq, w_dkv, kvn_lat, w_ukv, qn, kn, cossin_q, cossin_k):
    qd = B_NOPE + B_ROPE
    w1 = jnp.concatenate([w_dq, w_dkv, jnp.zeros((D, 768 - B_Q_RANK - B_KV_RANK - B_ROPE), F32)], axis=1)
    pad_q = jnp.zeros((B_Q_RANK, B_HEADS, LANES - qd), F32)
    wuq = jnp.concatenate([w_uq.reshape(B_Q_RANK, B_HEADS, qd), pad_q], axis=2).reshape(B_Q_RANK, B_HEADS * LANES)
    ukv = w_ukv.reshape(B_KV_RANK, B_HEADS, B_NOPE + B_V)
    wk_top = jnp.concatenate([ukv[:, :, :B_NOPE], jnp.zeros((B_KV_RANK, B_HEADS, LANES - B_NOPE), F32)], axis=2)
    route = np.zeros((LANES, B_HEADS, LANES), np.float32)
    route[np.arange(B_ROPE), :, B_NOPE + np.arange(B_ROPE)] = 1.0
    wk = jnp.concatenate([wk_top.reshape(B_KV_RANK, -1), jnp.asarray(route).reshape(LANES, -1)], axis=0)
    wv = ukv[:, :, B_NOPE:].reshape(B_KV_RANK, B_HEADS * B_V)
    scale = qd ** -0.5 * LOG2E
    zpad = jnp.zeros((LANES - qd,), F32)
    qgain = jnp.tile(jnp.concatenate([qn * scale, zpad]), 2)[None]
    qinv = jnp.tile(jnp.concatenate([jnp.full((B_NOPE,), 1.0 / B_NOPE), jnp.full((LANES - B_NOPE,), 1.0 / B_ROPE)]), 2)
    kgain = jnp.tile(jnp.concatenate([kn[:B_NOPE], jnp.zeros((LANES - B_NOPE,), F32)]), 2)[None]
    kmask = jnp.tile(jnp.concatenate([jnp.ones((B_NOPE,), F32), jnp.zeros((LANES - B_NOPE,), F32)]), 2)[None]
    gkr = jnp.concatenate([kn[B_NOPE:], jnp.zeros((LANES - B_ROPE,), F32)])[None]
    return dict(w1=w1.astype(BF16), gql=qn_lat[None], gkl=kvn_lat[None], gkr=gkr, kcos=cossin_k[0], ksin=cossin_k[1],
                wuq=wuq.astype(BF16), wk=wk.astype(BF16), wv=wv.astype(BF16), bd=_block_diag_ones(),
                qgain=qgain, qinv=qinv[None].astype(F32), qcos=cossin_q[0], qsin=cossin_q[1], kgain=kgain, kmask=kmask)


def kernel(x, c, ctx, c_ctx, ada_w, ada_b, norm1_w, norm2_w, a_wqkv, a_qnorm, a_knorm, a_wo, b_wdq, b_qnorm_lat, b_wuq, b_wdkv, b_kvnorm_lat, b_wukv, b_qnorm, b_knorm, b_wo, c_wqkv, c_qnorm, c_knorm, c_rpb, c_wo, moe_router, moe_wg, moe_wu, moe_wd):
    bsz = x.shape[0]
    assert x.shape[1:] == (SEQ, D) and ctx.shape[1:] == (CTX, D)
    mod_rows = -(-(bsz + 1) // 16) * 16
    cvec = jnp.concatenate([c, c_ctx[None], jnp.zeros((mod_rows - bsz - 1, D), F32)], axis=0)
    mods = _ada_all(cvec, ada_w, ada_b)
    xs = jnp.concatenate([ctx, x], axis=1)

    bd = _block_diag_ones()
    rope64 = _rope_tables(A_HEAD_DIM, 256, 0, LANES)
    rope_mla_q = _rope_tables(B_ROPE, 256, B_NOPE, B_NOPE + B_ROPE)
    rope_mla_k = _rope_tables(B_ROPE, LANES, 0, B_ROPE)
    router = jnp.pad(moe_router, ((0, 0), (0, 0), (0, LANES - N_EXPERTS))).astype(BF16)

    for i in range(DEPTH):
        last = i == DEPTH - 1
        kind, j = i % 3, i // 3
        m = mods[i]
        m_lat = m[:bsz].reshape(bsz, 6, D)
        m_ctx = jnp.broadcast_to(m[bsz].reshape(1, 6, D), (bsz, 6, D))
        modtab = jnp.stack([m_ctx, m_lat], axis=1)
        g1, g2 = norm1_w[i][None], norm2_w[i][None]

        if kind == 0:
            w, gq, gk, wo = _gqa_params(a_wqkv[j], a_qnorm[j], a_knorm[j], a_wo[j])
            q, k, v = _qkv_project(xs, modtab, g1, w, bd, gq, gk, rope64[0], rope64[1],
                                   A_HEADS * A_HEAD_DIM, A_KV_HEADS * A_HEAD_DIM, True)
            o = _pair_attention(q, k, v, wide=False, q_pairs_per_kv=4)
        elif kind == 1:
            p = _mla_params(b_wdq[j], b_qnorm_lat[j], b_wuq[j], b_wdkv[j], b_kvnorm_lat[j], b_wukv[j],
                            b_qnorm[j], b_knorm[j], rope_mla_q, rope_mla_k)
            q, k, v = _mla_project(xs, modtab, g1, p)
            o = _pair_attention(q, k, v, wide=True, q_pairs_per_kv=1)
            wo = b_wo[j].astype(BF16)
        else:
            hd = C_HEADS * C_HEAD_DIM
            gq = (jnp.tile(c_qnorm[j], 4) * (C_HEAD_DIM ** -0.5 * LOG2E))[None]
            gk = jnp.tile(c_knorm[j], 4)[None]
            q, k, v = _qkv_project(xs, modtab, g1, c_wqkv[j].astype(BF16), bd, gq, gk, rope64[0], rope64[1],
                                   hd, hd, False)
            o = _na_attention(q, k, v, _na_bias(c_rpb[j]))
            wo = c_wo[j].astype(BF16)

        xs, hs, aff = _out_router(o, wo, xs, modtab, g2, router[i])
        slot, off, gate = _topk(aff)
        ys = _experts(hs, off[:, :, None, :], gate[:, :, :, None],
                      moe_wg[i].astype(BF16), moe_wu[i].astype(BF16), moe_wd[i].astype(BF16), with_ctx=not last)
        if last:
            return _combine_last(slot, ys[0], xs, modtab)
        xs = _combine(slot, ys[0], ys[1], xs, modtab)
```

```python
import functools

import numpy as np
import jax
import jax.numpy as jnp
from jax import lax
from jax.experimental import pallas as pl
from jax.experimental.pallas import tpu as pltpu

F32 = jnp.float32
BF16 = jnp.bfloat16

D = 1024
SEQ = 2048
CTX = 256
NT = CTX + SEQ
DEPTH = 4
GRID_W = 64
ROPE_THETA = 10000.0
EPS = 1e-6
NEG_INF = -1e30
LOG2E = 1.4426950408889634

TM = 256
NTILES = NT // TM
LAT_TILES = SEQ // TM

A_HEADS, A_KV_HEADS, A_HEAD_DIM = 16, 4, 64
B_HEADS, B_Q_RANK, B_KV_RANK, B_NOPE, B_ROPE, B_V = 16, 384, 256, 64, 32, 64
C_HEADS, C_HEAD_DIM, NA_ROWS, NA_COLS = 16, 64, 8, 16
N_EXPERTS, EXPERT_FF = 16, 1024
CAP_LAT = 2 * SEQ // N_EXPERTS
CAP_CTX = 2 * CTX // N_EXPERTS
CAP_ALL = CAP_LAT + CAP_CTX
HS_ROWS = D // 256
TILE_STRIDE = 296
LANES = 128
BAND_TILES = 3

VMEM_LIMIT = 56 * 1024 * 1024


def _cparams(n_axes):
    return pltpu.CompilerParams(dimension_semantics=("arbitrary",) * n_axes, vmem_limit_bytes=VMEM_LIMIT)


def _dot(a, b):
    return jnp.dot(a, b, preferred_element_type=F32)


def _dot_t(a, b):
    return lax.dot_general(a, b, (((1,), (1,)), ((), ())), preferred_element_type=F32)


def _silu(x):
    return x / (1.0 + jnp.exp(-x))


def _prenorm(x, gain, shift, scale):
    ms = jnp.mean(x * x, axis=-1, keepdims=True)
    return (x * lax.rsqrt(ms + EPS) * gain) * (1.0 + scale) + shift


def _seg_rms(z, bd, inv_n):
    z2 = z * z
    hi = z2.astype(BF16)
    lo = (z2 - hi.astype(F32)).astype(BF16)
    ss = _dot(hi, bd) + _dot(lo, bd)
    return lax.rsqrt(ss * inv_n + EPS)


def _rope(z, cosw, sinw, first_half, half):
    w = z.shape[-1]
    up = pltpu.roll(z, w - half, axis=1)
    dn = pltpu.roll(z, half, axis=1)
    return z * cosw + jnp.where(first_half, up, dn) * sinw


def _ada_kernel(c_ref, w_ref, b_ref, o_ref):
    s = _silu(c_ref[...]).astype(BF16)
    o_ref[0] = _dot(s, w_ref[0].astype(BF16)) + b_ref[0]


def _ada_all(cvec, ada_w, ada_b):
    rows = cvec.shape[0]
    tn = 512
    return pl.pallas_call(
        _ada_kernel,
        grid=(DEPTH, 6 * D // tn),
        in_specs=[pl.BlockSpec((rows, D), lambda i, j: (0, 0)),
                  pl.BlockSpec((1, D, tn), lambda i, j: (i, 0, j)),
                  pl.BlockSpec((1, 1, tn), lambda i, j: (i, 0, j))],
        out_specs=pl.BlockSpec((1, rows, tn), lambda i, j: (i, 0, j)),
        out_shape=jax.ShapeDtypeStruct((DEPTH, rows, 6 * D), F32),
        compiler_params=_cparams(2), name="adaln",
    )(cvec, ada_w, ada_b.reshape(DEPTH, 1, 6 * D))


def _mod_spec():
    return pl.BlockSpec((1, 1, 6, D), lambda b, t: (b, jnp.minimum(t, 1), 0, 0))


def _row_spec(width):
    return pl.BlockSpec((1, TM, width), lambda b, t: (b, t, 0))


def _const_spec(shape):
    return pl.BlockSpec(shape, lambda b, t: (0,) * len(shape))


def _tab_spec(width):
    return pl.BlockSpec((TM, width), lambda b, t: (t, 0))


def _qkv_kernel(x_ref, mod_ref, g1_ref, w_ref, bd_ref, gq_ref, gk_ref, cos_ref, sin_ref,
                q_ref, k_ref, v_ref, *, nq, nk, rope):
    mod = mod_ref[0, 0]
    h = _prenorm(x_ref[0], g1_ref[...], mod[0:1], mod[1:2]).astype(BF16)
    acc = _dot(h, w_ref[...])
    bd = bd_ref[...]
    lane = lax.broadcasted_iota(jnp.int32, (1, 256), 1)
    first_half = (lane % 64) < 32
    cosw, sinw = cos_ref[...], sin_ref[...]

    def finish(z, gain):
        z = z * _seg_rms(z, bd, 1.0 / 64) * gain
        if rope:
            z = _rope(z, cosw, sinw, first_half, 32)
        return z.astype(BF16)

    for j in range(nq // 256):
        q_ref[0, :, j * 256:(j + 1) * 256] = finish(acc[:, j * 256:(j + 1) * 256], gq_ref[...])
    for j in range(nk // 256):
        k_ref[0, :, j * 256:(j + 1) * 256] = finish(acc[:, nq + j * 256:nq + (j + 1) * 256], gk_ref[...])
    v_ref[0] = acc[:, nq + nk:].astype(BF16)


def _qkv_project(x, modtab, g1, w, bd, gq, gk, cosw, sinw, nq, nk, rope):
    bsz = x.shape[0]
    return pl.pallas_call(
        functools.partial(_qkv_kernel, nq=nq, nk=nk, rope=rope),
        grid=(bsz, NTILES),
        in_specs=[_row_spec(D), _mod_spec(), _const_spec((1, D)), _const_spec((D, nq + 2 * nk)),
                  _const_spec((256, 256)), _const_spec((1, 256)), _const_spec((1, 256)),
                  _tab_spec(256), _tab_spec(256)],
        out_specs=[_row_spec(nq), _row_spec(nk), _row_spec(nk)],
        out_shape=[jax.ShapeDtypeStruct((bsz, NT, nq), BF16), jax.ShapeDtypeStruct((bsz, NT, nk), BF16),
                   jax.ShapeDtypeStruct((bsz, NT, nk), BF16)],
        compiler_params=_cparams(2), name="qkv_project",
    )(x, modtab, g1, w, bd, gq, gk, cosw, sinw)


def _mla_kernel(x_ref, mod_ref, g1_ref, w1_ref, gql_ref, gkl_ref, gkr_ref, kcos_ref, ksin_ref,
                wuq_ref, wk_ref, wv_ref, bd_ref, qgain_ref, qinv_ref, qcos_ref, qsin_ref, kgain_ref, kmask_ref,
                q_ref, k_ref, v_ref):
    mod = mod_ref[0, 0]
    h = _prenorm(x_ref[0], g1_ref[...], mod[0:1], mod[1:2]).astype(BF16)
    a = _dot(h, w1_ref[...])
    cq = a[:, :B_Q_RANK]
    cq = (cq * lax.rsqrt(jnp.mean(cq * cq, axis=-1, keepdims=True) + EPS) * gql_ref[...]).astype(BF16)
    ckv = a[:, B_Q_RANK:B_Q_RANK + B_KV_RANK]
    ckv = (ckv * lax.rsqrt(jnp.mean(ckv * ckv, axis=-1, keepdims=True) + EPS) * gkl_ref[...]).astype(BF16)
    kr = a[:, B_Q_RANK + B_KV_RANK:]
    kr = kr * lax.rsqrt(jnp.sum(kr * kr, axis=-1, keepdims=True) * (1.0 / B_ROPE) + EPS) * gkr_ref[...]
    lane128 = lax.broadcasted_iota(jnp.int32, (1, LANES), 1)
    kr = _rope(kr, kcos_ref[...], ksin_ref[...], (lane128 % 32) < 16, 16).astype(BF16)

    bd = bd_ref[...]
    lane = lax.broadcasted_iota(jnp.int32, (1, 256), 1)
    first_half = (lane % 32) < 16
    qcos, qsin = qcos_ref[...], qsin_ref[...]
    q2 = _dot(cq, wuq_ref[...])
    k2 = _dot(jnp.concatenate([ckv, kr], axis=1), wk_ref[...])
    kmask = kmask_ref[...] > 0.0
    for j in range(q2.shape[1] // 256):
        sl = slice(j * 256, (j + 1) * 256)
        z = q2[:, sl]
        z = z * _seg_rms(z, bd, qinv_ref[...]) * qgain_ref[...]
        q_ref[0, :, sl] = _rope(z, qcos, qsin, first_half, 16).astype(BF16)
        z = k2[:, sl]
        k_ref[0, :, sl] = (z * jnp.where(kmask, _seg_rms(z, bd, 1.0 / 64) * kgain_ref[...], 1.0)).astype(BF16)
    v_ref[0] = _dot(ckv, wv_ref[...]).astype(BF16)


def _mla_project(x, modtab, g1, p):
    bsz = x.shape[0]
    hq = B_HEADS * LANES
    return pl.pallas_call(
        _mla_kernel,
        grid=(bsz, NTILES),
        in_specs=[_row_spec(D), _mod_spec(), _const_spec((1, D)), _const_spec((D, 768)),
                  _const_spec((1, B_Q_RANK)), _const_spec((1, B_KV_RANK)), _const_spec((1, LANES)),
                  _tab_spec(LANES), _tab_spec(LANES),
                  _const_spec((B_Q_RANK, hq)), _const_spec((B_KV_RANK + LANES, hq)),
                  _const_spec((B_KV_RANK, B_HEADS * B_V)), _const_spec((256, 256)),
                  _const_spec((1, 256)), _const_spec((1, 256)), _tab_spec(256), _tab_spec(256),
                  _const_spec((1, 256)), _const_spec((1, 256))],
        out_specs=[_row_spec(hq), _row_spec(hq), _row_spec(B_HEADS * B_V)],
        out_shape=[jax.ShapeDtypeStruct((bsz, NT, hq), BF16), jax.ShapeDtypeStruct((bsz, NT, hq), BF16),
                   jax.ShapeDtypeStruct((bsz, NT, B_HEADS * B_V), BF16)],
        compiler_params=_cparams(2), name="mla_project",
    )(x, modtab, g1, p["w1"], p["gql"], p["gkl"], p["gkr"], p["kcos"], p["ksin"], p["wuq"], p["wk"], p["wv"],
      p["bd"], p["qgain"], p["qinv"], p["qcos"], p["qsin"], p["kgain"], p["kmask"])


def _softmax_pv(sa, sb, v):
    outs = []
    for s in (sa, sb):
        p = jnp.exp2(s - jnp.max(s, axis=-1, keepdims=True))
        l = jnp.sum(p, axis=-1, keepdims=True)
        outs.append(_dot(p.astype(BF16), v) / l)
    lane = lax.broadcasted_iota(jnp.int32, (1, LANES), 1)
    return jnp.where(lane < 64, outs[0], outs[1])


def _split_heads(q):
    lane = lax.broadcasted_iota(jnp.int32, (1, LANES), 1)
    zero = jnp.zeros_like(q)
    return jnp.where(lane < 64, q, zero), jnp.where(lane >= 64, q, zero)


PAIRS_PER_STEP = 2


def _pair_attn_kernel(q_ref, k_ref, v_ref, o_ref, *, wide):
    t = pl.program_id(2)

    def run(nkeys):
        for i in range(PAIRS_PER_STEP):
            if wide:
                q = q_ref[0, :, 256 * i:256 * (i + 1)]
                k = k_ref[0, :nkeys, 256 * i:256 * (i + 1)]
                sa, sb = _dot_t(q[:, :LANES], k[:, :LANES]), _dot_t(q[:, LANES:], k[:, LANES:])
                v = v_ref[0, :nkeys, LANES * i:LANES * (i + 1)]
            else:
                qa, qb = _split_heads(q_ref[0, :, LANES * i:LANES * (i + 1)])
                k = k_ref[0, :nkeys]
                sa, sb = _dot_t(qa, k), _dot_t(qb, k)
                v = v_ref[0, :nkeys]
            o_ref[0, :, LANES * i:LANES * (i + 1)] = _softmax_pv(sa, sb, v).astype(BF16)

    @pl.when(t == 0)
    def _():
        run(CTX)

    @pl.when(t > 0)
    def _():
        run(NT)


def _pair_attention(q, k, v, *, wide, q_pairs_per_kv):
    bsz = q.shape[0]
    qw = 256 if wide else LANES
    n_steps = q.shape[2] // (qw * PAIRS_PER_STEP)
    kv_pairs = PAIRS_PER_STEP if wide else 1
    kv_idx = lambda b, p, t: (b, 0, p * PAIRS_PER_STEP // (q_pairs_per_kv * kv_pairs))
    return pl.pallas_call(
        functools.partial(_pair_attn_kernel, wide=wide),
        grid=(bsz, n_steps, NTILES),
        in_specs=[pl.BlockSpec((1, TM, qw * PAIRS_PER_STEP), lambda b, p, t: (b, t, p)),
                  pl.BlockSpec((1, NT, qw * kv_pairs), kv_idx),
                  pl.BlockSpec((1, NT, LANES * kv_pairs), kv_idx)],
        out_specs=pl.BlockSpec((1, TM, LANES * PAIRS_PER_STEP), lambda b, p, t: (b, t, p)),
        out_shape=jax.ShapeDtypeStruct((bsz, NT, n_steps * PAIRS_PER_STEP * LANES), BF16),
        compiler_params=_cparams(3), name="pair_attention",
    )(q, k, v)


def _na_attn_kernel(q_ref, kc_ref, k0_ref, k1_ref, k2_ref, vc_ref, v0_ref, v1_ref, v2_ref, bias_ref, o_ref):
    t = pl.program_id(1)

    @pl.when(t == 0)
    def _():
        for i in range(PAIRS_PER_STEP):
            sl = slice(LANES * i, LANES * (i + 1))
            qa, qb = _split_heads(q_ref[0, :, sl])
            k = kc_ref[0, :, sl]
            o_ref[0, :, sl] = _softmax_pv(_dot_t(qa, k), _dot_t(qb, k), vc_ref[0, :, sl]).astype(BF16)

    @pl.when(t > 0)
    def _():
        for i in range(PAIRS_PER_STEP):
            sl = slice(LANES * i, LANES * (i + 1))
            qa, qb = _split_heads(q_ref[0, :, sl])
            k = jnp.concatenate([k0_ref[0, :, sl], k1_ref[0, :, sl], k2_ref[0, :, sl], kc_ref[0, :, sl]], axis=0)
            v = jnp.concatenate([v0_ref[0, :, sl], v1_ref[0, :, sl], v2_ref[0, :, sl], vc_ref[0, :, sl]], axis=0)
            sa = _dot_t(qa, k) + bias_ref[0, 2 * i]
            sb = _dot_t(qb, k) + bias_ref[0, 2 * i + 1]
            o_ref[0, :, sl] = _softmax_pv(sa, sb, v).astype(BF16)


def _na_band_tile(t):
    return 1 + jnp.clip(t - 2, 0, LAT_TILES - BAND_TILES)


def _na_attention(q, k, v, bias):
    bsz = q.shape[0]
    n_steps = C_HEADS // (2 * PAIRS_PER_STEP)
    width = LANES * PAIRS_PER_STEP
    qspec = pl.BlockSpec((1, TM, width), lambda p, t, b: (b, t, p))
    cspec = pl.BlockSpec((1, TM, width), lambda p, t, b: (b, 0, p))
    bands = [pl.BlockSpec((1, TM, width), functools.partial(lambda p, t, b, i: (b, _na_band_tile(t) + i, p), i=i))
             for i in range(BAND_TILES)]
    variant = lambda t: jnp.where(t <= 1, 0, jnp.where(t == LAT_TILES, 2, 1))
    return pl.pallas_call(
        _na_attn_kernel,
        grid=(n_steps, NTILES, bsz),
        in_specs=[qspec, cspec] + bands + [cspec] + bands
                 + [pl.BlockSpec((1, 2 * PAIRS_PER_STEP, TM, (BAND_TILES + 1) * TM),
                                 lambda p, t, b: (variant(t), p, 0, 0))],
        out_specs=qspec,
        out_shape=jax.ShapeDtypeStruct((bsz, NT, C_HEADS * C_HEAD_DIM), BF16),
        compiler_params=_cparams(3), name="na_attention",
    )(q, k, k, k, k, v, v, v, v, bias)


def _out_router_kernel(o_ref, wo_ref, x_ref, mod_ref, g2_ref, r_ref, xo_ref, h_ref, aff_ref):
    mod = mod_ref[0, 0]
    x = x_ref[0] + mod[2:3] * _dot(o_ref[0], wo_ref[...])
    xo_ref[0] = x
    h = _prenorm(x, g2_ref[...], mod[3:4], mod[4:5]).astype(BF16)
    for r, word in enumerate(_pack_rows(h)):
        h_ref[0, pl.ds(r, TM, stride=HS_ROWS), :] = word
    logits = _dot(h, r_ref[...])
    lane = lax.broadcasted_iota(jnp.int32, (1, LANES), 1)
    logits = jnp.where(lane < N_EXPERTS, logits, NEG_INF)
    e = jnp.exp(logits - jnp.max(logits, axis=-1, keepdims=True))
    aff_ref[0] = e / jnp.sum(e, axis=-1, keepdims=True)


def _out_router(o, wo, x, modtab, g2, router):
    bsz = x.shape[0]
    return pl.pallas_call(
        _out_router_kernel,
        grid=(bsz, NTILES),
        in_specs=[_row_spec(D), _const_spec((D, D)), _row_spec(D), _mod_spec(), _const_spec((1, D)),
                  _const_spec((D, LANES))],
        out_specs=[_row_spec(D), pl.BlockSpec((1, TM * HS_ROWS, LANES), lambda b, t: (b, t, 0)), _row_spec(LANES)],
        out_shape=[jax.ShapeDtypeStruct((bsz, NT, D), F32),
                   jax.ShapeDtypeStruct((bsz, NT * HS_ROWS, LANES), jnp.int32),
                   jax.ShapeDtypeStruct((bsz, NT, LANES), F32)],
        input_output_aliases={2: 0},
        compiler_params=_cparams(2), name="out_router",
    )(o, wo, x, modtab, g2, router)


def _excl_cumsum_lanes(m, triu):
    out = []
    offset = jnp.zeros((m.shape[0], 1), F32)
    for i in range(m.shape[1] // TM):
        blk = m[:, i * TM:(i + 1) * TM]
        out.append(_dot(blk.astype(BF16), triu) + offset)
        offset = offset + jnp.sum(blk, axis=1, keepdims=True)
    return jnp.concatenate(out, axis=1) if len(out) > 1 else out[0]


def _kth_largest_bits(bits_list, caps):
    def count(bits, cand):
        return jnp.sum(jnp.where(bits >= cand, 1.0, 0.0), axis=1, keepdims=True)

    def step(i, thrs):
        b1 = jnp.left_shift(jnp.int32(1), 29 - 2 * i)
        b0 = jnp.left_shift(jnp.int32(1), 28 - 2 * i)
        out = []
        for bits, cap, thr in zip(bits_list, caps, thrs):
            c1, c2, c3 = count(bits, thr | b1), count(bits, thr | b0), count(bits, thr | b1 | b0)
            with_b1 = jnp.where(c3 >= cap, thr | b1 | b0, thr | b1)
            without = jnp.where(c2 >= cap, thr | b0, thr)
            out.append(jnp.where(c1 >= cap, with_b1, without))
        return tuple(out)

    top = jnp.int32(1 << 30)
    init = tuple(jnp.where(count(bits, top) >= cap, top, jnp.zeros((N_EXPERTS, 1), jnp.int32))
                 for bits, cap in zip(bits_list, caps))
    return lax.fori_loop(0, 15, step, init)


def _topk_slots(bits, thr, cap, triu):
    gt = bits > thr
    eq = bits == thr
    need = cap - jnp.sum(jnp.where(gt, 1.0, 0.0), axis=1, keepdims=True)
    eq_rank = _excl_cumsum_lanes(jnp.where(eq, 1.0, 0.0), triu)
    sel = gt | (eq & (eq_rank < need))
    slot = _excl_cumsum_lanes(jnp.where(sel, 1.0, 0.0), triu)
    return jnp.where(sel, slot, -1.0)


OFF_LANE = 3 * N_EXPERTS


def _token_values(aff, first_row):
    n = aff.shape[0]
    hi = aff.astype(BF16).astype(F32)
    r1 = aff - hi
    mid = r1.astype(BF16).astype(F32)
    lo = (r1 - mid).astype(BF16).astype(F32)
    off = (lax.broadcasted_iota(jnp.int32, (n, 1), 0) + first_row) * HS_ROWS
    lane = lax.broadcasted_iota(jnp.int32, (1, LANES), 1)
    vals = (hi + pltpu.roll(mid, N_EXPERTS, axis=1) + pltpu.roll(lo, 2 * N_EXPERTS, axis=1)
            + jnp.where(lane == OFF_LANE, (off >> 7).astype(F32), 0.0)
            + jnp.where(lane == OFF_LANE + 1, (off & (LANES - 1)).astype(F32), 0.0))
    return vals.astype(BF16)


def _topk_kernel(aff_t_ref, aff_ref, slot_ref, off_ref, gate_ref):
    r = lax.broadcasted_iota(jnp.int32, (TM, TM), 0)
    c = lax.broadcasted_iota(jnp.int32, (TM, TM), 1)
    triu = jnp.where(r < c, 1.0, 0.0).astype(BF16)
    lane = lax.broadcasted_iota(jnp.int32, (1, LANES), 1)
    segments = ((0, CTX, CAP_CTX, CAP_LAT), (CTX, NT, CAP_LAT, 0))
    bits = [pltpu.bitcast(aff_t_ref[0, :, lo:hi], jnp.int32) for lo, hi, _, _ in segments]
    thrs = _kth_largest_bits(bits, [cap for _, _, cap, _ in segments])
    for (lo, hi, cap, dst), seg_bits, thr in zip(segments, bits, thrs):
        slot = _topk_slots(seg_bits, thr, cap, triu)
        slot_ref[0, :, lo:hi] = slot.astype(jnp.int32)
        vals = _token_values(aff_ref[0, lo:hi], lo)
        slot_ids = lax.broadcasted_iota(jnp.int32, (cap, 1), 0).astype(F32)
        for e in range(N_EXPERTS):
            hit = jnp.where(slot_ids == slot[e:e + 1, :], 1.0, 0.0).astype(BF16)
            picked = _dot(hit, vals)
            mine = ((lane & (N_EXPERTS - 1)) == e) & (lane < OFF_LANE)
            gate = jnp.sum(jnp.where(mine, picked, 0.0), axis=1, keepdims=True)
            off = picked[:, OFF_LANE:OFF_LANE + 1] * float(LANES) + picked[:, OFF_LANE + 1:OFF_LANE + 2]
            gate_ref[0, e, dst:dst + cap, :] = gate
            off_ref[0, e, dst:dst + cap, :] = off.astype(jnp.int32)


def _topk(aff_t, aff):
    bsz = aff.shape[0]
    tspec = pl.BlockSpec((1, N_EXPERTS, NT), lambda b: (b, 0, 0))
    lspec = pl.BlockSpec((1, N_EXPERTS, CAP_ALL, 1), lambda b: (b, 0, 0, 0))
    return pl.pallas_call(
        _topk_kernel, grid=(bsz,),
        in_specs=[tspec, pl.BlockSpec((1, NT, LANES), lambda b: (b, 0, 0))],
        out_specs=[tspec, lspec, lspec],
        out_shape=[jax.ShapeDtypeStruct((bsz, N_EXPERTS, NT), jnp.int32),
                   jax.ShapeDtypeStruct((bsz, N_EXPERTS, CAP_ALL, 1), jnp.int32),
                   jax.ShapeDtypeStruct((bsz, N_EXPERTS, CAP_ALL, 1), F32)],
        compiler_params=_cparams(1), name="expert_topk",
    )(aff_t, aff)


def _pack_rows(hb):
    words = []
    for r in range(HS_ROWS):
        lo = pltpu.bitcast(hb[:, 256 * r:256 * r + LANES].astype(F32), jnp.int32)
        hi = pltpu.bitcast(hb[:, 256 * r + LANES:256 * (r + 1)].astype(F32), jnp.int32)
        words.append(lax.shift_right_logical(lo, jnp.int32(16)) | hi)
    return words


def _unpack_rows(words):
    cols = []
    for w in words:
        cols.append(pltpu.bitcast(lax.shift_left(w, jnp.int32(16)), F32).astype(BF16))
        cols.append(pltpu.bitcast(w & jnp.int32(-65536), F32).astype(BF16))
    return jnp.concatenate(cols, axis=1)


def _expert_kernel(hs_ref, off_ref, gate_ref, wg_ref, wu_ref, wd_ref, *refs, n_slots):
    out_refs, tile_ref = refs[:-1], refs[-1]
    for j in range(n_slots):
        src = pl.multiple_of(off_ref[0, 0, 0, j], HS_ROWS)
        tile_ref[pl.ds(j, HS_ROWS, stride=TILE_STRIDE), :] = hs_ref[0, pl.ds(src, HS_ROWS), :]
    xg = _unpack_rows([tile_ref[r * TILE_STRIDE:r * TILE_STRIDE + n_slots, :] for r in range(HS_ROWS)])
    hid = (_silu(_dot(xg, wg_ref[0])) * _dot(xg, wu_ref[0])).astype(BF16)
    y = _dot(hid, wd_ref[0]) * gate_ref[0, 0, :n_slots]
    out_refs[0][0] = y[:CAP_LAT].astype(BF16)
    if n_slots > CAP_LAT:
        out_refs[1][0] = y[CAP_LAT:].astype(BF16)


def _experts(hs, off, gate, wg, wu, wd, with_ctx):
    bsz = hs.shape[0]
    n_slots = CAP_ALL if with_ctx else CAP_LAT
    wspec = pl.BlockSpec((1, D, EXPERT_FF), lambda e, b: (e, 0, 0))
    out_specs = [pl.BlockSpec((1, CAP_LAT, D), lambda e, b: (b, e, 0))]
    out_shape = [jax.ShapeDtypeStruct((bsz, N_EXPERTS * CAP_LAT, D), BF16)]
    if with_ctx:
        out_specs.append(pl.BlockSpec((1, CAP_CTX, D), lambda e, b: (b, e, 0)))
        out_shape.append(jax.ShapeDtypeStruct((bsz, N_EXPERTS * CAP_CTX, D), BF16))
    return pl.pallas_call(
        functools.partial(_expert_kernel, n_slots=n_slots),
        grid=(N_EXPERTS, bsz),
        in_specs=[pl.BlockSpec((1, NT * HS_ROWS, LANES), lambda e, b: (b, 0, 0)),
                  pl.BlockSpec((1, 1, 1, CAP_ALL), lambda e, b: (b, e, 0, 0), memory_space=pltpu.SMEM),
                  pl.BlockSpec((1, 1, CAP_ALL, 1), lambda e, b: (b, e, 0, 0)),
                  wspec, wspec, pl.BlockSpec((1, EXPERT_FF, D), lambda e, b: (e, 0, 0))],
        out_specs=out_specs, out_shape=out_shape,
        scratch_shapes=[pltpu.VMEM((HS_ROWS * TILE_STRIDE, LANES), jnp.int32)],
        compiler_params=_cparams(2), name="experts",
    )(hs, off, gate, wg, wu, wd)


def _combine_lat(slot, y, x, gate):
    cols = lax.broadcasted_iota(jnp.int32, (TM, CAP_LAT), 1)
    onehot = jnp.concatenate(
        [jnp.where(slot[:, e:e + 1] == cols, 1.0, 0.0).astype(BF16) for e in range(N_EXPERTS)], axis=1)
    return x + gate * _dot(onehot, y)


def _combine_kernel(slot_ref, yl_ref, yc_ref, x_ref, mod_ref, xo_ref):
    t = pl.program_id(1)
    gate = mod_ref[0, 0][5:6]

    @pl.when(t == 0)
    def _():
        slot = slot_ref[0]
        cols = lax.broadcasted_iota(jnp.int32, (TM, N_EXPERTS * CAP_CTX), 1)
        hit = cols < 0
        for e in range(N_EXPERTS):
            s = slot[:, e:e + 1]
            hit = hit | ((s >= 0) & (s + e * CAP_CTX == cols))
        xo_ref[0] = x_ref[0] + gate * _dot(jnp.where(hit, 1.0, 0.0).astype(BF16), yc_ref[0])

    @pl.when(t > 0)
    def _():
        xo_ref[0] = _combine_lat(slot_ref[0], yl_ref[0], x_ref[0], gate)


def _combine(slot, yl, yc, x, modtab):
    bsz = x.shape[0]
    return pl.pallas_call(
        _combine_kernel,
        grid=(bsz, NTILES),
        in_specs=[_row_spec(LANES), pl.BlockSpec((1, N_EXPERTS * CAP_LAT, D), lambda b, t: (b, 0, 0)),
                  pl.BlockSpec((1, N_EXPERTS * CAP_CTX, D), lambda b, t: (b, 0, 0)), _row_spec(D), _mod_spec()],
        out_specs=_row_spec(D),
        out_shape=jax.ShapeDtypeStruct((bsz, NT, D), F32),
        input_output_aliases={3: 0},
        compiler_params=_cparams(2), name="moe_combine",
    )(slot, yl, yc, x, modtab)


def _combine_last_kernel(slot_ref, yl_ref, x_ref, mod_ref, xo_ref):
    xo_ref[0] = _combine_lat(slot_ref[0], yl_ref[0], x_ref[0], mod_ref[0, 0][5:6])


def _combine_last(slot, yl, x, modtab):
    bsz = x.shape[0]
    lat = lambda width: pl.BlockSpec((1, TM, width), lambda b, t: (b, t + 1, 0))
    return pl.pallas_call(
        _combine_last_kernel,
        grid=(bsz, LAT_TILES),
        in_specs=[lat(LANES), pl.BlockSpec((1, N_EXPERTS * CAP_LAT, D), lambda b, t: (b, 0, 0)), lat(D),
                  pl.BlockSpec((1, 1, 6, D), lambda b, t: (b, 1, 0, 0))],
        out_specs=pl.BlockSpec((1, TM, D), lambda b, t: (b, t, 0)),
        out_shape=jax.ShapeDtypeStruct((bsz, SEQ, D), F32),
        compiler_params=_cparams(2), name="moe_combine_last",
    )(slot, yl, x, modtab)


def _rope_tables(rot_dim, width, lane_lo, lane_hi):
    n_freq = rot_dim // 4
    half = rot_dim // 2
    inv = jnp.float32(ROPE_THETA) ** (-jnp.arange(n_freq, dtype=F32) / n_freq)
    t = jnp.arange(SEQ, dtype=jnp.int32)
    row = (t // GRID_W).astype(F32)
    col = (t % GRID_W).astype(F32)
    ang = jnp.concatenate([row[:, None] * inv, col[:, None] * inv], axis=-1)
    cos, sin = jnp.cos(ang), jnp.sin(ang)
    lane = np.arange(width)
    inside = (lane % LANES >= lane_lo) & (lane % LANES < lane_hi)
    idx = lane % half
    sign = np.where(lane % rot_dim < half, -1.0, 1.0).astype(np.float32)
    cosw = jnp.where(inside[None, :], cos[:, idx], 1.0)
    sinw = jnp.where(inside[None, :], sin[:, idx] * sign[None, :], 0.0)
    ones = jnp.ones((CTX, width), F32)
    return jnp.concatenate([ones, cosw], axis=0), jnp.concatenate([0.0 * ones, sinw], axis=0)


def _block_diag_ones():
    i = np.arange(256)
    return jnp.asarray((i[:, None] // 64) == (i[None, :] // 64), dtype=BF16)


_GQA_ORDER = np.array([8 * kp + 4 * odd + i for kp in range(2) for i in range(4) for odd in range(2)])


def _na_bias(rpb):
    out = []
    rows = SEQ // GRID_W
    qrows, krows = TM // GRID_W, BAND_TILES * TM // GRID_W
    col = np.arange(GRID_W)
    cs = np.clip(col - NA_COLS // 2, 0, GRID_W - NA_COLS)
    ok_c = (col[None, :] >= cs[:, None]) & (col[None, :] < cs[:, None] + NA_COLS)
    dc = np.clip(col[None, :] - col[:, None] + NA_COLS - 1, 0, 2 * NA_COLS - 2)
    pick_c = (dc[:, :, None] == np.arange(2 * NA_COLS - 1)).astype(np.float32)
    for r0, bs in ((0, 0), (8, 4), (rows - 4, rows - 12)):
        r, kr = r0 + np.arange(qrows), bs + np.arange(krows)
        rs = np.clip(r - NA_ROWS // 2, 0, rows - NA_ROWS)
        ok_r = (kr[None, :] >= rs[:, None]) & (kr[None, :] < rs[:, None] + NA_ROWS)
        dr = kr[None, :] - r[:, None] + NA_ROWS - 1
        pick_r = (dr[:, :, None] == np.arange(2 * NA_ROWS - 1)).astype(np.float32)
        t1 = jnp.einsum('ard,hde->hare', pick_r, rpb.astype(F32), precision=lax.Precision.HIGHEST)
        band = jnp.einsum('hare,cse->hacrs', t1, pick_c, precision=lax.Precision.HIGHEST)
        ok = ok_r[:, None, :, None] & ok_c[None, :, None, :]
        band = jnp.where(ok[None], band * LOG2E, NEG_INF).reshape(C_HEADS, TM, BAND_TILES * TM)
        out.append(jnp.concatenate([band, jnp.zeros((C_HEADS, TM, TM), F32)], axis=-1))
    return jnp.stack(out)


def _gqa_params(w_qkv, qn, kn, wo):
    nq, nk = A_HEADS * A_HEAD_DIM, A_KV_HEADS * A_HEAD_DIM
    wq = w_qkv[:, :nq].reshape(D, A_HEADS, A_HEAD_DIM)[:, _GQA_ORDER].reshape(D, nq)
    w = jnp.concatenate([wq, w_qkv[:, nq:]], axis=1).astype(BF16)
    wo_p = wo.reshape(A_HEADS, A_HEAD_DIM, D)[_GQA_ORDER].reshape(nq, D).astype(BF16)
    scale = A_HEAD_DIM ** -0.5 * LOG2E
    return w, (jnp.tile(qn, 4) * scale)[None], jnp.tile(kn, 4)[None], wo_p


def _mla_params(w_dq, qn_lat, w_uq, w_dkv, kvn_lat, w_ukv, qn, kn, cossin_q, cossin_k):
    qd = B_NOPE + B_ROPE
    w1 = jnp.concatenate([w_dq, w_dkv, jnp.zeros((D, 768 - B_Q_RANK - B_KV_RANK - B_ROPE), F32)], axis=1)
    pad_q = jnp.zeros((B_Q_RANK, B_HEADS, LANES - qd), F32)
    wuq = jnp.concatenate([w_uq.reshape(B_Q_RANK, B_HEADS, qd), pad_q], axis=2).reshape(B_Q_RANK, B_HEADS * LANES)
    ukv = w_ukv.reshape(B_KV_RANK, B_HEADS, B_NOPE + B_V)
    wk_top = jnp.concatenate([ukv[:, :, :B_NOPE], jnp.zeros((B_KV_RANK, B_HEADS, LANES - B_NOPE), F32)], axis=2)
    route = np.zeros((LANES, B_HEADS, LANES), np.float32)
    route[np.arange(B_ROPE), :, B_NOPE + np.arange(B_ROPE)] = 1.0
    wk = jnp.concatenate([wk_top.reshape(B_KV_RANK, -1), jnp.asarray(route).reshape(LANES, -1)], axis=0)
    wv = ukv[:, :, B_NOPE:].reshape(B_KV_RANK, B_HEADS * B_V)
    scale = qd ** -0.5 * LOG2E
    zpad = jnp.zeros((LANES - qd,), F32)
    qgain = jnp.tile(jnp.concatenate([qn * scale, zpad]), 2)[None]
    qinv = jnp.tile(jnp.concatenate([jnp.full((B_NOPE,), 1.0 / B_NOPE), jnp.full((LANES - B_NOPE,), 1.0 / B_ROPE)]), 2)
    kgain = jnp.tile(jnp.concatenate([kn[:B_NOPE], jnp.zeros((LANES - B_NOPE,), F32)]), 2)[None]
    kmask = jnp.tile(jnp.concatenate([jnp.ones((B_NOPE,), F32), jnp.zeros((LANES - B_NOPE,), F32)]), 2)[None]
    gkr = jnp.concatenate([kn[B_NOPE:], jnp.zeros((LANES - B_ROPE,), F32)])[None]
    return dict(w1=w1.astype(BF16), gql=qn_lat[None], gkl=kvn_lat[None], gkr=gkr, kcos=cossin_k[0], ksin=cossin_k[1],
                wuq=wuq.astype(BF16), wk=wk.astype(BF16), wv=wv.astype(BF16), bd=_block_diag_ones(),
                qgain=qgain, qinv=qinv[None].astype(F32), qcos=cossin_q[0], qsin=cossin_q[1], kgain=kgain, kmask=kmask)


def kernel(x, c, ctx, c_ctx, ada_w, ada_b, norm1_w, norm2_w, a_wqkv, a_qnorm, a_knorm, a_wo, b_wdq, b_qnorm_lat, b_wuq, b_wdkv, b_kvnorm_lat, b_wukv, b_qnorm, b_knorm, b_wo, c_wqkv, c_qnorm, c_knorm, c_rpb, c_wo, moe_router, moe_wg, moe_wu, moe_wd):
    bsz = x.shape[0]
    assert x.shape[1:] == (SEQ, D) and ctx.shape[1:] == (CTX, D)
    mod_rows = -(-(bsz + 1) // 16) * 16
    cvec = jnp.concatenate([c, c_ctx[None], jnp.zeros((mod_rows - bsz - 1, D), F32)], axis=0)
    mods = _ada_all(cvec, ada_w, ada_b)
    xs = jnp.concatenate([ctx, x], axis=1)

    bd = _block_diag_ones()
    rope64 = _rope_tables(A_HEAD_DIM, 256, 0, LANES)
    rope_mla_q = _rope_tables(B_ROPE, 256, B_NOPE, B_NOPE + B_ROPE)
    rope_mla_k = _rope_tables(B_ROPE, LANES, 0, B_ROPE)
    router = jnp.pad(moe_router, ((0, 0), (0, 0), (0, LANES - N_EXPERTS))).astype(BF16)

    for i in range(DEPTH):
        last = i == DEPTH - 1
        kind, j = i % 3, i // 3
        m = mods[i]
        m_lat = m[:bsz].reshape(bsz, 6, D)
        m_ctx = jnp.broadcast_to(m[bsz].reshape(1, 6, D), (bsz, 6, D))
        modtab = jnp.stack([m_ctx, m_lat], axis=1)
        g1, g2 = norm1_w[i][None], norm2_w[i][None]

        if kind == 0:
            w, gq, gk, wo = _gqa_params(a_wqkv[j], a_qnorm[j], a_knorm[j], a_wo[j])
            q, k, v = _qkv_project(xs, modtab, g1, w, bd, gq, gk, rope64[0], rope64[1],
                                   A_HEADS * A_HEAD_DIM, A_KV_HEADS * A_HEAD_DIM, True)
            o = _pair_attention(q, k, v, wide=False, q_pairs_per_kv=4)
        elif kind == 1:
            p = _mla_params(b_wdq[j], b_qnorm_lat[j], b_wuq[j], b_wdkv[j], b_kvnorm_lat[j], b_wukv[j],
                            b_qnorm[j], b_knorm[j], rope_mla_q, rope_mla_k)
            q, k, v = _mla_project(xs, modtab, g1, p)
            o = _pair_attention(q, k, v, wide=True, q_pairs_per_kv=1)
            wo = b_wo[j].astype(BF16)
        else:
            hd = C_HEADS * C_HEAD_DIM
            gq = (jnp.tile(c_qnorm[j], 4) * (C_HEAD_DIM ** -0.5 * LOG2E))[None]
            gk = jnp.tile(c_knorm[j], 4)[None]
            q, k, v = _qkv_project(xs, modtab, g1, c_wqkv[j].astype(BF16), bd, gq, gk, rope64[0], rope64[1],
                                   hd, hd, False)
            o = _na_attention(q, k, v, _na_bias(c_rpb[j]))
            wo = c_wo[j].astype(BF16)

        xs, hs, aff = _out_router(o, wo, xs, modtab, g2, router[i])
        slot_t, off, gate = _topk(jnp.swapaxes(aff[:, :, :N_EXPERTS], 1, 2), aff)
        slot = jnp.pad(jnp.swapaxes(slot_t, 1, 2), ((0, 0), (0, 0), (0, LANES - N_EXPERTS)), constant_values=-1)
        ys = _experts(hs, off.reshape(bsz, N_EXPERTS, 1, CAP_ALL), gate,
                      moe_wg[i].astype(BF16), moe_wu[i].astype(BF16), moe_wd[i].astype(BF16), with_ctx=not last)
        if last:
            return _combine_last(slot, ys[0], xs, modtab)
        xs = _combine(slot, ys[0], ys[1], xs, modtab)
```

```python
import functools

import numpy as np
import jax
import jax.numpy as jnp
from jax import lax
from jax.experimental import pallas as pl
from jax.experimental.pallas import tpu as pltpu

F32 = jnp.float32
BF16 = jnp.bfloat16

D = 1024
SEQ = 2048
CTX = 256
NT = CTX + SEQ
DEPTH = 4
GRID_W = 64
ROPE_THETA = 10000.0
EPS = 1e-6
NEG_INF = -1e30
LOG2E = 1.4426950408889634

TM = 256
NTILES = NT // TM
LAT_TILES = SEQ // TM

A_HEADS, A_KV_HEADS, A_HEAD_DIM = 16, 4, 64
B_HEADS, B_Q_RANK, B_KV_RANK, B_NOPE, B_ROPE, B_V = 16, 384, 256, 64, 32, 64
C_HEADS, C_HEAD_DIM, NA_ROWS, NA_COLS = 16, 64, 8, 16
N_EXPERTS, EXPERT_FF = 16, 1024
CAP_LAT = 2 * SEQ // N_EXPERTS
CAP_CTX = 2 * CTX // N_EXPERTS
CAP_ALL = CAP_LAT + CAP_CTX
HS_ROWS = D // 256
TILE_STRIDE = 296
LANES = 128
BAND_TILES = 3

VMEM_LIMIT = 56 * 1024 * 1024


def _cparams(n_axes):
    return pltpu.CompilerParams(dimension_semantics=("arbitrary",) * n_axes, vmem_limit_bytes=VMEM_LIMIT)


def _dot(a, b):
    return jnp.dot(a, b, preferred_element_type=F32)


def _dot_t(a, b):
    return lax.dot_general(a, b, (((1,), (1,)), ((), ())), preferred_element_type=F32)


def _silu(x):
    return x / (1.0 + jnp.exp(-x))


def _prenorm(x, gain, shift, scale):
    ms = jnp.mean(x * x, axis=-1, keepdims=True)
    return (x * lax.rsqrt(ms + EPS) * gain) * (1.0 + scale) + shift


def _seg_rms(z, bd, inv_n):
    z2 = z * z
    hi = z2.astype(BF16)
    lo = (z2 - hi.astype(F32)).astype(BF16)
    ss = _dot(hi, bd) + _dot(lo, bd)
    return lax.rsqrt(ss * inv_n + EPS)


def _rope(z, cosw, sinw, first_half, half):
    w = z.shape[-1]
    up = pltpu.roll(z, w - half, axis=1)
    dn = pltpu.roll(z, half, axis=1)
    return z * cosw + jnp.where(first_half, up, dn) * sinw


def _ada_kernel(c_ref, w_ref, b_ref, o_ref):
    s = _silu(c_ref[...]).astype(BF16)
    o_ref[0] = _dot(s, w_ref[0].astype(BF16)) + b_ref[0]


def _ada_all(cvec, ada_w, ada_b):
    rows = cvec.shape[0]
    tn = 512
    return pl.pallas_call(
        _ada_kernel,
        grid=(DEPTH, 6 * D // tn),
        in_specs=[pl.BlockSpec((rows, D), lambda i, j: (0, 0)),
                  pl.BlockSpec((1, D, tn), lambda i, j: (i, 0, j)),
                  pl.BlockSpec((1, 1, tn), lambda i, j: (i, 0, j))],
        out_specs=pl.BlockSpec((1, rows, tn), lambda i, j: (i, 0, j)),
        out_shape=jax.ShapeDtypeStruct((DEPTH, rows, 6 * D), F32),
        compiler_params=_cparams(2), name="adaln",
    )(cvec, ada_w, ada_b.reshape(DEPTH, 1, 6 * D))


def _mod_spec():
    return pl.BlockSpec((1, 1, 6, D), lambda b, t: (b, jnp.minimum(t, 1), 0, 0))


def _row_spec(width):
    return pl.BlockSpec((1, TM, width), lambda b, t: (b, t, 0))


def _const_spec(shape):
    return pl.BlockSpec(shape, lambda b, t: (0,) * len(shape))


def _tab_spec(width):
    return pl.BlockSpec((TM, width), lambda b, t: (t, 0))


def _qkv_kernel(x_ref, mod_ref, g1_ref, w_ref, bd_ref, gq_ref, gk_ref, cos_ref, sin_ref,
                q_ref, k_ref, v_ref, *, nq, nk, rope):
    mod = mod_ref[0, 0]
    h = _prenorm(x_ref[0], g1_ref[...], mod[0:1], mod[1:2]).astype(BF16)
    acc = _dot(h, w_ref[...])
    bd = bd_ref[...]
    lane = lax.broadcasted_iota(jnp.int32, (1, 256), 1)
    first_half = (lane % 64) < 32
    cosw, sinw = cos_ref[...], sin_ref[...]

    def finish(z, gain):
        z = z * _seg_rms(z, bd, 1.0 / 64) * gain
        if rope:
            z = _rope(z, cosw, sinw, first_half, 32)
        return z.astype(BF16)

    for j in range(nq // 256):
        q_ref[0, :, j * 256:(j + 1) * 256] = finish(acc[:, j * 256:(j + 1) * 256], gq_ref[...])
    for j in range(nk // 256):
        k_ref[0, :, j * 256:(j + 1) * 256] = finish(acc[:, nq + j * 256:nq + (j + 1) * 256], gk_ref[...])
    v_ref[0] = acc[:, nq + nk:].astype(BF16)


def _qkv_project(x, modtab, g1, w, bd, gq, gk, cosw, sinw, nq, nk, rope):
    bsz = x.shape[0]
    return pl.pallas_call(
        functools.partial(_qkv_kernel, nq=nq, nk=nk, rope=rope),
        grid=(bsz, NTILES),
        in_specs=[_row_spec(D), _mod_spec(), _const_spec((1, D)), _const_spec((D, nq + 2 * nk)),
                  _const_spec((256, 256)), _const_spec((1, 256)), _const_spec((1, 256)),
                  _tab_spec(256), _tab_spec(256)],
        out_specs=[_row_spec(nq), _row_spec(nk), _row_spec(nk)],
        out_shape=[jax.ShapeDtypeStruct((bsz, NT, nq), BF16), jax.ShapeDtypeStruct((bsz, NT, nk), BF16),
                   jax.ShapeDtypeStruct((bsz, NT, nk), BF16)],
        compiler_params=_cparams(2), name="qkv_project",
    )(x, modtab, g1, w, bd, gq, gk, cosw, sinw)


def _mla_kernel(x_ref, mod_ref, g1_ref, w1_ref, gql_ref, gkl_ref, gkr_ref, kcos_ref, ksin_ref,
                wuq_ref, wk_ref, wv_ref, bd_ref, qgain_ref, qinv_ref, qcos_ref, qsin_ref, kgain_ref, kmask_ref,
                q_ref, k_ref, v_ref):
    mod = mod_ref[0, 0]
    h = _prenorm(x_ref[0], g1_ref[...], mod[0:1], mod[1:2]).astype(BF16)
    a = _dot(h, w1_ref[...])
    cq = a[:, :B_Q_RANK]
    cq = (cq * lax.rsqrt(jnp.mean(cq * cq, axis=-1, keepdims=True) + EPS) * gql_ref[...]).astype(BF16)
    ckv = a[:, B_Q_RANK:B_Q_RANK + B_KV_RANK]
    ckv = (ckv * lax.rsqrt(jnp.mean(ckv * ckv, axis=-1, keepdims=True) + EPS) * gkl_ref[...]).astype(BF16)
    kr = a[:, B_Q_RANK + B_KV_RANK:]
    kr = kr * lax.rsqrt(jnp.sum(kr * kr, axis=-1, keepdims=True) * (1.0 / B_ROPE) + EPS) * gkr_ref[...]
    lane128 = lax.broadcasted_iota(jnp.int32, (1, LANES), 1)
    kr = _rope(kr, kcos_ref[...], ksin_ref[...], (lane128 % 32) < 16, 16).astype(BF16)

    bd = bd_ref[...]
    lane = lax.broadcasted_iota(jnp.int32, (1, 256), 1)
    first_half = (lane % 32) < 16
    qcos, qsin = qcos_ref[...], qsin_ref[...]
    q2 = _dot(cq, wuq_ref[...])
    k2 = _dot(jnp.concatenate([ckv, kr], axis=1), wk_ref[...])
    kmask = kmask_ref[...] > 0.0
    for j in range(q2.shape[1] // 256):
        sl = slice(j * 256, (j + 1) * 256)
        z = q2[:, sl]
        z = z * _seg_rms(z, bd, qinv_ref[...]) * qgain_ref[...]
        q_ref[0, :, sl] = _rope(z, qcos, qsin, first_half, 16).astype(BF16)
        z = k2[:, sl]
        k_ref[0, :, sl] = (z * jnp.where(kmask, _seg_rms(z, bd, 1.0 / 64) * kgain_ref[...], 1.0)).astype(BF16)
    v_ref[0] = _dot(ckv, wv_ref[...]).astype(BF16)


def _mla_project(x, modtab, g1, p):
    bsz = x.shape[0]
    hq = B_HEADS * LANES
    return pl.pallas_call(
        _mla_kernel,
        grid=(bsz, NTILES),
        in_specs=[_row_spec(D), _mod_spec(), _const_spec((1, D)), _const_spec((D, 768)),
                  _const_spec((1, B_Q_RANK)), _const_spec((1, B_KV_RANK)), _const_spec((1, LANES)),
                  _tab_spec(LANES), _tab_spec(LANES),
                  _const_spec((B_Q_RANK, hq)), _const_spec((B_KV_RANK + LANES, hq)),
                  _const_spec((B_KV_RANK, B_HEADS * B_V)), _const_spec((256, 256)),
                  _const_spec((1, 256)), _const_spec((1, 256)), _tab_spec(256), _tab_spec(256),
                  _const_spec((1, 256)), _const_spec((1, 256))],
        out_specs=[_row_spec(hq), _row_spec(hq), _row_spec(B_HEADS * B_V)],
        out_shape=[jax.ShapeDtypeStruct((bsz, NT, hq), BF16), jax.ShapeDtypeStruct((bsz, NT, hq), BF16),
                   jax.ShapeDtypeStruct((bsz, NT, B_HEADS * B_V), BF16)],
        compiler_params=_cparams(2), name="mla_project",
    )(x, modtab, g1, p["w1"], p["gql"], p["gkl"], p["gkr"], p["kcos"], p["ksin"], p["wuq"], p["wk"], p["wv"],
      p["bd"], p["qgain"], p["qinv"], p["qcos"], p["qsin"], p["kgain"], p["kmask"])


def _softmax_pv(sa, sb, v):
    outs = []
    for s in (sa, sb):
        p = jnp.exp2(s - jnp.max(s, axis=-1, keepdims=True))
        l = jnp.sum(p, axis=-1, keepdims=True)
        outs.append(_dot(p.astype(BF16), v) / l)
    lane = lax.broadcasted_iota(jnp.int32, (1, LANES), 1)
    return jnp.where(lane < 64, outs[0], outs[1])


def _split_heads(q):
    lane = lax.broadcasted_iota(jnp.int32, (1, LANES), 1)
    zero = jnp.zeros_like(q)
    return jnp.where(lane < 64, q, zero), jnp.where(lane >= 64, q, zero)


PAIRS_PER_STEP = 4


def _pair_attn_kernel(q_ref, k_ref, v_ref, o_ref, *, wide):
    t = pl.program_id(2)

    def run(nkeys):
        for i in range(PAIRS_PER_STEP):
            if wide:
                q = q_ref[0, :, 256 * i:256 * (i + 1)]
                k = k_ref[0, :nkeys, 256 * i:256 * (i + 1)]
                sa, sb = _dot_t(q[:, :LANES], k[:, :LANES]), _dot_t(q[:, LANES:], k[:, LANES:])
                v = v_ref[0, :nkeys, LANES * i:LANES * (i + 1)]
            else:
                qa, qb = _split_heads(q_ref[0, :, LANES * i:LANES * (i + 1)])
                k = k_ref[0, :nkeys]
                sa, sb = _dot_t(qa, k), _dot_t(qb, k)
                v = v_ref[0, :nkeys]
            o_ref[0, :, LANES * i:LANES * (i + 1)] = _softmax_pv(sa, sb, v).astype(BF16)

    @pl.when(t == 0)
    def _():
        run(CTX)

    @pl.when(t > 0)
    def _():
        run(NT)


def _pair_attention(q, k, v, *, wide, q_pairs_per_kv):
    bsz = q.shape[0]
    qw = 256 if wide else LANES
    n_steps = q.shape[2] // (qw * PAIRS_PER_STEP)
    kv_pairs = PAIRS_PER_STEP if wide else 1
    kv_idx = lambda b, p, t: (b, 0, p * PAIRS_PER_STEP // (q_pairs_per_kv * kv_pairs))
    return pl.pallas_call(
        functools.partial(_pair_attn_kernel, wide=wide),
        grid=(bsz, n_steps, NTILES),
        in_specs=[pl.BlockSpec((1, TM, qw * PAIRS_PER_STEP), lambda b, p, t: (b, t, p)),
                  pl.BlockSpec((1, NT, qw * kv_pairs), kv_idx),
                  pl.BlockSpec((1, NT, LANES * kv_pairs), kv_idx)],
        out_specs=pl.BlockSpec((1, TM, LANES * PAIRS_PER_STEP), lambda b, p, t: (b, t, p)),
        out_shape=jax.ShapeDtypeStruct((bsz, NT, n_steps * PAIRS_PER_STEP * LANES), BF16),
        compiler_params=_cparams(3), name="pair_attention",
    )(q, k, v)


def _na_attn_kernel(q_ref, kc_ref, k0_ref, k1_ref, k2_ref, vc_ref, v0_ref, v1_ref, v2_ref, bias_ref, o_ref):
    t = pl.program_id(1)

    @pl.when(t == 0)
    def _():
        for i in range(PAIRS_PER_STEP):
            sl = slice(LANES * i, LANES * (i + 1))
            qa, qb = _split_heads(q_ref[0, :, sl])
            k = kc_ref[0, :, sl]
            o_ref[0, :, sl] = _softmax_pv(_dot_t(qa, k), _dot_t(qb, k), vc_ref[0, :, sl]).astype(BF16)

    @pl.when(t > 0)
    def _():
        for i in range(PAIRS_PER_STEP):
            sl = slice(LANES * i, LANES * (i + 1))
            qa, qb = _split_heads(q_ref[0, :, sl])
            k = jnp.concatenate([k0_ref[0, :, sl], k1_ref[0, :, sl], k2_ref[0, :, sl], kc_ref[0, :, sl]], axis=0)
            v = jnp.concatenate([v0_ref[0, :, sl], v1_ref[0, :, sl], v2_ref[0, :, sl], vc_ref[0, :, sl]], axis=0)
            sa = _dot_t(qa, k) + bias_ref[0, 2 * i]
            sb = _dot_t(qb, k) + bias_ref[0, 2 * i + 1]
            o_ref[0, :, sl] = _softmax_pv(sa, sb, v).astype(BF16)


def _na_band_tile(t):
    return 1 + jnp.clip(t - 2, 0, LAT_TILES - BAND_TILES)


def _na_attention(q, k, v, bias):
    bsz = q.shape[0]
    n_steps = C_HEADS // (2 * PAIRS_PER_STEP)
    width = LANES * PAIRS_PER_STEP
    qspec = pl.BlockSpec((1, TM, width), lambda p, t, b: (b, t, p))
    cspec = pl.BlockSpec((1, TM, width), lambda p, t, b: (b, 0, p))
    bands = [pl.BlockSpec((1, TM, width), functools.partial(lambda p, t, b, i: (b, _na_band_tile(t) + i, p), i=i))
             for i in range(BAND_TILES)]
    variant = lambda t: jnp.where(t <= 1, 0, jnp.where(t == LAT_TILES, 2, 1))
    return pl.pallas_call(
        _na_attn_kernel,
        grid=(n_steps, NTILES, bsz),
        in_specs=[qspec, cspec] + bands + [cspec] + bands
                 + [pl.BlockSpec((1, 2 * PAIRS_PER_STEP, TM, (BAND_TILES + 1) * TM),
                                 lambda p, t, b: (variant(t), p, 0, 0))],
        out_specs=qspec,
        out_shape=jax.ShapeDtypeStruct((bsz, NT, C_HEADS * C_HEAD_DIM), BF16),
        compiler_params=_cparams(3), name="na_attention",
    )(q, k, k, k, k, v, v, v, v, bias)


def _out_router_kernel(o_ref, wo_ref, x_ref, mod_ref, g2_ref, r_ref, xo_ref, h_ref, aff_ref):
    mod = mod_ref[0, 0]
    x = x_ref[0] + mod[2:3] * _dot(o_ref[0], wo_ref[...])
    xo_ref[0] = x
    h = _prenorm(x, g2_ref[...], mod[3:4], mod[4:5]).astype(BF16)
    for r, word in enumerate(_pack_rows(h)):
        h_ref[0, pl.ds(r, TM, stride=HS_ROWS), :] = word
    logits = _dot(h, r_ref[...])
    lane = lax.broadcasted_iota(jnp.int32, (1, LANES), 1)
    logits = jnp.where(lane < N_EXPERTS, logits, NEG_INF)
    e = jnp.exp(logits - jnp.max(logits, axis=-1, keepdims=True))
    aff_ref[0] = e / jnp.sum(e, axis=-1, keepdims=True)


def _out_router(o, wo, x, modtab, g2, router):
    bsz = x.shape[0]
    return pl.pallas_call(
        _out_router_kernel,
        grid=(bsz, NTILES),
        in_specs=[_row_spec(D), _const_spec((D, D)), _row_spec(D), _mod_spec(), _const_spec((1, D)),
                  _const_spec((D, LANES))],
        out_specs=[_row_spec(D), pl.BlockSpec((1, TM * HS_ROWS, LANES), lambda b, t: (b, t, 0)), _row_spec(LANES)],
        out_shape=[jax.ShapeDtypeStruct((bsz, NT, D), F32),
                   jax.ShapeDtypeStruct((bsz, NT * HS_ROWS, LANES), jnp.int32),
                   jax.ShapeDtypeStruct((bsz, NT, LANES), F32)],
        input_output_aliases={2: 0},
        compiler_params=_cparams(2), name="out_router",
    )(o, wo, x, modtab, g2, router)


def _excl_cumsum_lanes(m, triu):
    out = []
    offset = jnp.zeros((m.shape[0], 1), F32)
    for i in range(m.shape[1] // TM):
        blk = m[:, i * TM:(i + 1) * TM]
        out.append(_dot(blk.astype(BF16), triu) + offset)
        offset = offset + jnp.sum(blk, axis=1, keepdims=True)
    return jnp.concatenate(out, axis=1) if len(out) > 1 else out[0]


def _kth_largest_bits(bits_list, caps):
    def count(bits, cand):
        return jnp.sum(jnp.where(bits >= cand, 1.0, 0.0), axis=1, keepdims=True)

    def step(i, thrs):
        b1 = jnp.left_shift(jnp.int32(1), 29 - 2 * i)
        b0 = jnp.left_shift(jnp.int32(1), 28 - 2 * i)
        out = []
        for bits, cap, thr in zip(bits_list, caps, thrs):
            c1, c2, c3 = count(bits, thr | b1), count(bits, thr | b0), count(bits, thr | b1 | b0)
            with_b1 = jnp.where(c3 >= cap, thr | b1 | b0, thr | b1)
            without = jnp.where(c2 >= cap, thr | b0, thr)
            out.append(jnp.where(c1 >= cap, with_b1, without))
        return tuple(out)

    top = jnp.int32(1 << 30)
    init = tuple(jnp.where(count(bits, top) >= cap, top, jnp.zeros((N_EXPERTS, 1), jnp.int32))
                 for bits, cap in zip(bits_list, caps))
    return lax.fori_loop(0, 15, step, init)


def _topk_slots(bits, thr, cap, triu):
    gt = bits > thr
    eq = bits == thr
    need = cap - jnp.sum(jnp.where(gt, 1.0, 0.0), axis=1, keepdims=True)
    eq_rank = _excl_cumsum_lanes(jnp.where(eq, 1.0, 0.0), triu)
    sel = gt | (eq & (eq_rank < need))
    slot = _excl_cumsum_lanes(jnp.where(sel, 1.0, 0.0), triu)
    return jnp.where(sel, slot, -1.0)


OFF_LANE = 3 * N_EXPERTS


def _token_values(aff, first_row):
    n = aff.shape[0]
    hi = aff.astype(BF16).astype(F32)
    r1 = aff - hi
    mid = r1.astype(BF16).astype(F32)
    lo = (r1 - mid).astype(BF16).astype(F32)
    off = (lax.broadcasted_iota(jnp.int32, (n, 1), 0) + first_row) * HS_ROWS
    lane = lax.broadcasted_iota(jnp.int32, (1, LANES), 1)
    vals = (hi + pltpu.roll(mid, N_EXPERTS, axis=1) + pltpu.roll(lo, 2 * N_EXPERTS, axis=1)
            + jnp.where(lane == OFF_LANE, (off >> 7).astype(F32), 0.0)
            + jnp.where(lane == OFF_LANE + 1, (off & (LANES - 1)).astype(F32), 0.0))
    return vals.astype(BF16)


def _topk_kernel(aff_t_ref, aff_ref, slot_ref, off_ref, gate_ref):
    r = lax.broadcasted_iota(jnp.int32, (TM, TM), 0)
    c = lax.broadcasted_iota(jnp.int32, (TM, TM), 1)
    triu = jnp.where(r < c, 1.0, 0.0).astype(BF16)
    lane = lax.broadcasted_iota(jnp.int32, (1, LANES), 1)
    segments = ((0, CTX, CAP_CTX, CAP_LAT), (CTX, NT, CAP_LAT, 0))
    bits = [pltpu.bitcast(aff_t_ref[0, :, lo:hi], jnp.int32) for lo, hi, _, _ in segments]
    thrs = _kth_largest_bits(bits, [cap for _, _, cap, _ in segments])
    for (lo, hi, cap, dst), seg_bits, thr in zip(segments, bits, thrs):
        slot = _topk_slots(seg_bits, thr, cap, triu)
        slot_ref[0, :, lo:hi] = slot.astype(jnp.int32)
        vals = _token_values(aff_ref[0, lo:hi], lo)
        slot_ids = lax.broadcasted_iota(jnp.int32, (cap, 1), 0).astype(F32)
        for e in range(N_EXPERTS):
            hit = jnp.where(slot_ids == slot[e:e + 1, :], 1.0, 0.0).astype(BF16)
            picked = _dot(hit, vals)
            mine = ((lane & (N_EXPERTS - 1)) == e) & (lane < OFF_LANE)
            gate = jnp.sum(jnp.where(mine, picked, 0.0), axis=1, keepdims=True)
            off = picked[:, OFF_LANE:OFF_LANE + 1] * float(LANES) + picked[:, OFF_LANE + 1:OFF_LANE + 2]
            gate_ref[0, e, dst:dst + cap, :] = gate
            off_ref[0, e, dst:dst + cap, :] = off.astype(jnp.int32)


def _topk(aff_t, aff):
    bsz = aff.shape[0]
    tspec = pl.BlockSpec((1, N_EXPERTS, NT), lambda b: (b, 0, 0))
    lspec = pl.BlockSpec((1, N_EXPERTS, CAP_ALL, 1), lambda b: (b, 0, 0, 0))
    return pl.pallas_call(
        _topk_kernel, grid=(bsz,),
        in_specs=[tspec, pl.BlockSpec((1, NT, LANES), lambda b: (b, 0, 0))],
        out_specs=[tspec, lspec, lspec],
        out_shape=[jax.ShapeDtypeStruct((bsz, N_EXPERTS, NT), jnp.int32),
                   jax.ShapeDtypeStruct((bsz, N_EXPERTS, CAP_ALL, 1), jnp.int32),
                   jax.ShapeDtypeStruct((bsz, N_EXPERTS, CAP_ALL, 1), F32)],
        compiler_params=_cparams(1), name="expert_topk",
    )(aff_t, aff)


def _pack_rows(hb):
    words = []
    for r in range(HS_ROWS):
        lo = pltpu.bitcast(hb[:, 256 * r:256 * r + LANES].astype(F32), jnp.int32)
        hi = pltpu.bitcast(hb[:, 256 * r + LANES:256 * (r + 1)].astype(F32), jnp.int32)
        words.append(lax.shift_right_logical(lo, jnp.int32(16)) | hi)
    return words


def _unpack_rows(words):
    cols = []
    for w in words:
        cols.append(pltpu.bitcast(lax.shift_left(w, jnp.int32(16)), F32).astype(BF16))
        cols.append(pltpu.bitcast(w & jnp.int32(-65536), F32).astype(BF16))
    return jnp.concatenate(cols, axis=1)


def _expert_kernel(hs_ref, off_ref, gate_ref, wg_ref, wu_ref, wd_ref, *refs, n_slots):
    out_refs, (tile_ref, wg_s, wu_s, wd_s) = refs[:-4], refs[-4:]

    @pl.when(pl.program_id(1) == 0)
    def _():
        wg_s[...] = wg_ref[0, 0].astype(BF16)
        wu_s[...] = wu_ref[0, 0].astype(BF16)
        wd_s[...] = wd_ref[0, 0].astype(BF16)

    for j in range(n_slots):
        src = pl.multiple_of(off_ref[0, 0, 0, j], HS_ROWS)
        tile_ref[pl.ds(j, HS_ROWS, stride=TILE_STRIDE), :] = hs_ref[0, pl.ds(src, HS_ROWS), :]
    xg = _unpack_rows([tile_ref[r * TILE_STRIDE:r * TILE_STRIDE + n_slots, :] for r in range(HS_ROWS)])
    hid = (_silu(_dot(xg, wg_s[...])) * _dot(xg, wu_s[...])).astype(BF16)
    y = _dot(hid, wd_s[...]) * gate_ref[0, 0, :n_slots]
    out_refs[0][0] = y[:CAP_LAT].astype(BF16)
    if n_slots > CAP_LAT:
        out_refs[1][0] = y[CAP_LAT:].astype(BF16)


def _experts(hs, off, gate, wg, wu, wd, layer, with_ctx):
    bsz = hs.shape[0]
    n_slots = CAP_ALL if with_ctx else CAP_LAT
    wspec = pl.BlockSpec((1, 1, D, EXPERT_FF), lambda e, b: (layer, e, 0, 0))
    out_specs = [pl.BlockSpec((1, CAP_LAT, D), lambda e, b: (b, e, 0))]
    out_shape = [jax.ShapeDtypeStruct((bsz, N_EXPERTS * CAP_LAT, D), BF16)]
    if with_ctx:
        out_specs.append(pl.BlockSpec((1, CAP_CTX, D), lambda e, b: (b, e, 0)))
        out_shape.append(jax.ShapeDtypeStruct((bsz, N_EXPERTS * CAP_CTX, D), BF16))
    return pl.pallas_call(
        functools.partial(_expert_kernel, n_slots=n_slots),
        grid=(N_EXPERTS, bsz),
        in_specs=[pl.BlockSpec((1, NT * HS_ROWS, LANES), lambda e, b: (b, 0, 0)),
                  pl.BlockSpec((1, 1, 1, CAP_ALL), lambda e, b: (b, e, 0, 0), memory_space=pltpu.SMEM),
                  pl.BlockSpec((1, 1, CAP_ALL, 1), lambda e, b: (b, e, 0, 0)),
                  wspec, wspec, pl.BlockSpec((1, 1, EXPERT_FF, D), lambda e, b: (layer, e, 0, 0))],
        out_specs=out_specs, out_shape=out_shape,
        scratch_shapes=[pltpu.VMEM((HS_ROWS * TILE_STRIDE, LANES), jnp.int32),
                        pltpu.VMEM((D, EXPERT_FF), BF16), pltpu.VMEM((D, EXPERT_FF), BF16),
                        pltpu.VMEM((EXPERT_FF, D), BF16)],
        compiler_params=_cparams(2), name="experts",
    )(hs, off, gate, wg, wu, wd)


def _combine_lat(slot, y, x, gate):
    cols = lax.broadcasted_iota(jnp.int32, (TM, CAP_LAT), 1)
    onehot = jnp.concatenate(
        [jnp.where(slot[:, e:e + 1] == cols, 1.0, 0.0).astype(BF16) for e in range(N_EXPERTS)], axis=1)
    return x + gate * _dot(onehot, y)


def _combine_kernel(slot_ref, yl_ref, yc_ref, x_ref, mod_ref, xo_ref):
    t = pl.program_id(1)
    gate = mod_ref[0, 0][5:6]

    @pl.when(t == 0)
    def _():
        slot = slot_ref[0]
        cols = lax.broadcasted_iota(jnp.int32, (TM, N_EXPERTS * CAP_CTX), 1)
        hit = cols < 0
        for e in range(N_EXPERTS):
            s = slot[:, e:e + 1]
            hit = hit | ((s >= 0) & (s + e * CAP_CTX == cols))
        xo_ref[0] = x_ref[0] + gate * _dot(jnp.where(hit, 1.0, 0.0).astype(BF16), yc_ref[0])

    @pl.when(t > 0)
    def _():
        xo_ref[0] = _combine_lat(slot_ref[0], yl_ref[0], x_ref[0], gate)


def _combine(slot, yl, yc, x, modtab):
    bsz = x.shape[0]
    return pl.pallas_call(
        _combine_kernel,
        grid=(bsz, NTILES),
        in_specs=[_row_spec(LANES), pl.BlockSpec((1, N_EXPERTS * CAP_LAT, D), lambda b, t: (b, 0, 0)),
                  pl.BlockSpec((1, N_EXPERTS * CAP_CTX, D), lambda b, t: (b, 0, 0)), _row_spec(D), _mod_spec()],
        out_specs=_row_spec(D),
        out_shape=jax.ShapeDtypeStruct((bsz, NT, D), F32),
        input_output_aliases={3: 0},
        compiler_params=_cparams(2), name="moe_combine",
    )(slot, yl, yc, x, modtab)


def _combine_last_kernel(slot_ref, yl_ref, x_ref, mod_ref, xo_ref):
    xo_ref[0] = _combine_lat(slot_ref[0], yl_ref[0], x_ref[0], mod_ref[0, 0][5:6])


def _combine_last(slot, yl, x, modtab):
    bsz = x.shape[0]
    lat = lambda width: pl.BlockSpec((1, TM, width), lambda b, t: (b, t + 1, 0))
    return pl.pallas_call(
        _combine_last_kernel,
        grid=(bsz, LAT_TILES),
        in_specs=[lat(LANES), pl.BlockSpec((1, N_EXPERTS * CAP_LAT, D), lambda b, t: (b, 0, 0)), lat(D),
                  pl.BlockSpec((1, 1, 6, D), lambda b, t: (b, 1, 0, 0))],
        out_specs=pl.BlockSpec((1, TM, D), lambda b, t: (b, t, 0)),
        out_shape=jax.ShapeDtypeStruct((bsz, SEQ, D), F32),
        compiler_params=_cparams(2), name="moe_combine_last",
    )(slot, yl, x, modtab)


def _rope_tables(rot_dim, width, lane_lo, lane_hi):
    n_freq = rot_dim // 4
    half = rot_dim // 2
    inv = jnp.float32(ROPE_THETA) ** (-jnp.arange(n_freq, dtype=F32) / n_freq)
    t = jnp.arange(SEQ, dtype=jnp.int32)
    row = (t // GRID_W).astype(F32)
    col = (t % GRID_W).astype(F32)
    ang = jnp.concatenate([row[:, None] * inv, col[:, None] * inv], axis=-1)
    cos, sin = jnp.cos(ang), jnp.sin(ang)
    lane = np.arange(width)
    inside = (lane % LANES >= lane_lo) & (lane % LANES < lane_hi)
    idx = lane % half
    sign = np.where(lane % rot_dim < half, -1.0, 1.0).astype(np.float32)
    cosw = jnp.where(inside[None, :], cos[:, idx], 1.0)
    sinw = jnp.where(inside[None, :], sin[:, idx] * sign[None, :], 0.0)
    ones = jnp.ones((CTX, width), F32)
    return jnp.concatenate([ones, cosw], axis=0), jnp.concatenate([0.0 * ones, sinw], axis=0)


def _block_diag_ones():
    i = np.arange(256)
    return jnp.asarray((i[:, None] // 64) == (i[None, :] // 64), dtype=BF16)


_GQA_ORDER = np.array([8 * kp + 4 * odd + i for kp in range(2) for i in range(4) for odd in range(2)])


def _na_bias(rpb):
    out = []
    rows = SEQ // GRID_W
    qrows, krows = TM // GRID_W, BAND_TILES * TM // GRID_W
    col = np.arange(GRID_W)
    cs = np.clip(col - NA_COLS // 2, 0, GRID_W - NA_COLS)
    ok_c = (col[None, :] >= cs[:, None]) & (col[None, :] < cs[:, None] + NA_COLS)
    dc = np.clip(col[None, :] - col[:, None] + NA_COLS - 1, 0, 2 * NA_COLS - 2)
    pick_c = (dc[:, :, None] == np.arange(2 * NA_COLS - 1)).astype(np.float32)
    for r0, bs in ((0, 0), (8, 4), (rows - 4, rows - 12)):
        r, kr = r0 + np.arange(qrows), bs + np.arange(krows)
        rs = np.clip(r - NA_ROWS // 2, 0, rows - NA_ROWS)
        ok_r = (kr[None, :] >= rs[:, None]) & (kr[None, :] < rs[:, None] + NA_ROWS)
        dr = kr[None, :] - r[:, None] + NA_ROWS - 1
        pick_r = (dr[:, :, None] == np.arange(2 * NA_ROWS - 1)).astype(np.float32)
        t1 = jnp.einsum('ard,hde->hare', pick_r, rpb.astype(F32), precision=lax.Precision.HIGHEST)
        band = jnp.einsum('hare,cse->hacrs', t1, pick_c, precision=lax.Precision.HIGHEST)
        ok = ok_r[:, None, :, None] & ok_c[None, :, None, :]
        band = jnp.where(ok[None], band * LOG2E, NEG_INF).reshape(C_HEADS, TM, BAND_TILES * TM)
        out.append(jnp.concatenate([band, jnp.zeros((C_HEADS, TM, TM), F32)], axis=-1))
    return jnp.stack(out)


def _gqa_params(w_qkv, qn, kn, wo):
    nq, nk = A_HEADS * A_HEAD_DIM, A_KV_HEADS * A_HEAD_DIM
    wq = w_qkv[:, :nq].reshape(D, A_HEADS, A_HEAD_DIM)[:, _GQA_ORDER].reshape(D, nq)
    w = jnp.concatenate([wq, w_qkv[:, nq:]], axis=1).astype(BF16)
    wo_p = wo.reshape(A_HEADS, A_HEAD_DIM, D)[_GQA_ORDER].reshape(nq, D).astype(BF16)
    scale = A_HEAD_DIM ** -0.5 * LOG2E
    return w, (jnp.tile(qn, 4) * scale)[None], jnp.tile(kn, 4)[None], wo_p


def _mla_params(w_dq, qn_lat, w_uq, w_dkv, kvn_lat, w_ukv, qn, kn, cossin_q, cossin_k):
    qd = B_NOPE + B_ROPE
    w1 = jnp.concatenate([w_dq, w_dkv, jnp.zeros((D, 768 - B_Q_RANK - B_KV_RANK - B_ROPE), F32)], axis=1)
    pad_q = jnp.zeros((B_Q_RANK, B_HEADS, LANES - qd), F32)
    wuq = jnp.concatenate([w_uq.reshape(B_Q_RANK, B_HEADS, qd), pad_q], axis=2).reshape(B_Q_RANK, B_HEADS * LANES)
    ukv = w_ukv.reshape(B_KV_RANK, B_HEADS, B_NOPE + B_V)
    wk_top = jnp.concatenate([ukv[:, :, :B_NOPE], jnp.zeros((B_KV_RANK, B_HEADS, LANES - B_NOPE), F32)], axis=2)
    route = np.zeros((LANES, B_HEADS, LANES), np.float32)
    route[np.arange(B_ROPE), :, B_NOPE + np.arange(B_ROPE)] = 1.0
    wk = jnp.concatenate([wk_top.reshape(B_KV_RANK, -1), jnp.asarray(route).reshape(LANES, -1)], axis=0)
    wv = ukv[:, :, B_NOPE:].reshape(B_KV_RANK, B_HEADS * B_V)
    scale = qd ** -0.5 * LOG2E
    zpad = jnp.zeros((LANES - qd,), F32)
    qgain = jnp.tile(jnp.concatenate([qn * scale, zpad]), 2)[None]
    qinv = jnp.tile(jnp.concatenate([jnp.full((B_NOPE,), 1.0 / B_NOPE), jnp.full((LANES - B_NOPE,), 1.0 / B_ROPE)]), 2)
    kgain = jnp.tile(jnp.concatenate([kn[:B_NOPE], jnp.zeros((LANES - B_NOPE,), F32)]), 2)[None]
    kmask = jnp.tile(jnp.concatenate([jnp.ones((B_NOPE,), F32), jnp.zeros((LANES - B_NOPE,), F32)]), 2)[None]
    gkr = jnp.concatenate([kn[B_NOPE:], jnp.zeros((LANES - B_ROPE,), F32)])[None]
    return dict(w1=w1.astype(BF16), gql=qn_lat[None], gkl=kvn_lat[None], gkr=gkr, kcos=cossin_k[0], ksin=cossin_k[1],
                wuq=wuq.astype(BF16), wk=wk.astype(BF16), wv=wv.astype(BF16), bd=_block_diag_ones(),
                qgain=qgain, qinv=qinv[None].astype(F32), qcos=cossin_q[0], qsin=cossin_q[1], kgain=kgain, kmask=kmask)


def kernel(x, c, ctx, c_ctx, ada_w, ada_b, norm1_w, norm2_w, a_wqkv, a_qnorm, a_knorm, a_wo, b_wdq, b_qnorm_lat, b_wuq, b_wdkv, b_kvnorm_lat, b_wukv, b_qnorm, b_knorm, b_wo, c_wqkv, c_qnorm, c_knorm, c_rpb, c_wo, moe_router, moe_wg, moe_wu, moe_wd):
    bsz = x.shape[0]
    assert x.shape[1:] == (SEQ, D) and ctx.shape[1:] == (CTX, D)
    mod_rows = -(-(bsz + 1) // 16) * 16
    cvec = jnp.concatenate([c, c_ctx[None], jnp.zeros((mod_rows - bsz - 1, D), F32)], axis=0)
    mods = _ada_all(cvec, ada_w, ada_b)
    xs = jnp.concatenate([ctx, x], axis=1)

    bd = _block_diag_ones()
    rope64 = _rope_tables(A_HEAD_DIM, 256, 0, LANES)
    rope_mla_q = _rope_tables(B_ROPE, 256, B_NOPE, B_NOPE + B_ROPE)
    rope_mla_k = _rope_tables(B_ROPE, LANES, 0, B_ROPE)
    router = jnp.pad(moe_router, ((0, 0), (0, 0), (0, LANES - N_EXPERTS))).astype(BF16)

    for i in range(DEPTH):
        last = i == DEPTH - 1
        kind, j = i % 3, i // 3
        m = mods[i]
        m_lat = m[:bsz].reshape(bsz, 6, D)
        m_ctx = jnp.broadcast_to(m[bsz].reshape(1, 6, D), (bsz, 6, D))
        modtab = jnp.stack([m_ctx, m_lat], axis=1)
        g1, g2 = norm1_w[i][None], norm2_w[i][None]

        if kind == 0:
            w, gq, gk, wo = _gqa_params(a_wqkv[j], a_qnorm[j], a_knorm[j], a_wo[j])
            q, k, v = _qkv_project(xs, modtab, g1, w, bd, gq, gk, rope64[0], rope64[1],
                                   A_HEADS * A_HEAD_DIM, A_KV_HEADS * A_HEAD_DIM, True)
            o = _pair_attention(q, k, v, wide=False, q_pairs_per_kv=4)
        elif kind == 1:
            p = _mla_params(b_wdq[j], b_qnorm_lat[j], b_wuq[j], b_wdkv[j], b_kvnorm_lat[j], b_wukv[j],
                            b_qnorm[j], b_knorm[j], rope_mla_q, rope_mla_k)
            q, k, v = _mla_project(xs, modtab, g1, p)
            o = _pair_attention(q, k, v, wide=True, q_pairs_per_kv=1)
            wo = b_wo[j].astype(BF16)
        else:
            hd = C_HEADS * C_HEAD_DIM
            gq = (jnp.tile(c_qnorm[j], 4) * (C_HEAD_DIM ** -0.5 * LOG2E))[None]
            gk = jnp.tile(c_knorm[j], 4)[None]
            q, k, v = _qkv_project(xs, modtab, g1, c_wqkv[j].astype(BF16), bd, gq, gk, rope64[0], rope64[1],
                                   hd, hd, False)
            o = _na_attention(q, k, v, _na_bias(c_rpb[j]))
            wo = c_wo[j].astype(BF16)

        xs, hs, aff = _out_router(o, wo, xs, modtab, g2, router[i])
        slot_t, off, gate = _topk(jnp.swapaxes(aff[:, :, :N_EXPERTS], 1, 2), aff)
        slot = jnp.pad(jnp.swapaxes(slot_t, 1, 2), ((0, 0), (0, 0), (0, LANES - N_EXPERTS)), constant_values=-1)
        ys = _experts(hs, off.reshape(bsz, N_EXPERTS, 1, CAP_ALL), gate,
                      moe_wg, moe_wu, moe_wd, layer=i, with_ctx=not last)
        if last:
            return _combine_last(slot, ys[0], xs, modtab)
        xs = _combine(slot, ys[0], ys[1], xs, modtab)
```

```python
import functools

import numpy as np
import jax
import jax.numpy as jnp
from jax import lax
from jax.experimental import pallas as pl
from jax.experimental.pallas import tpu as pltpu

F32 = jnp.float32
BF16 = jnp.bfloat16

D = 1024
SEQ = 2048
CTX = 256
NT = CTX + SEQ
DEPTH = 4
GRID_W = 64
ROPE_THETA = 10000.0
EPS = 1e-6
NEG_INF = -1e30
LOG2E = 1.4426950408889634

TM = 256
NTILES = NT // TM
LAT_TILES = SEQ // TM

A_HEADS, A_KV_HEADS, A_HEAD_DIM = 16, 4, 64
B_HEADS, B_Q_RANK, B_KV_RANK, B_NOPE, B_ROPE, B_V = 16, 384, 256, 64, 32, 64
C_HEADS, C_HEAD_DIM, NA_ROWS, NA_COLS = 16, 64, 8, 16
N_EXPERTS, EXPERT_FF = 16, 1024
CAP_LAT = 2 * SEQ // N_EXPERTS
CAP_CTX = 2 * CTX // N_EXPERTS
CAP_ALL = CAP_LAT + CAP_CTX
HS_ROWS = D // 256
TILE_STRIDE = 296
LANES = 128
BAND_TILES = 3

VMEM_LIMIT = 56 * 1024 * 1024


def _cparams(n_axes):
    return pltpu.CompilerParams(dimension_semantics=("arbitrary",) * n_axes, vmem_limit_bytes=VMEM_LIMIT)


def _dot(a, b):
    return jnp.dot(a, b, preferred_element_type=F32)


def _dot_t(a, b):
    return lax.dot_general(a, b, (((1,), (1,)), ((), ())), preferred_element_type=F32)


def _silu(x):
    return x / (1.0 + jnp.exp(-x))


def _prenorm(x, gain, shift, scale):
    ms = jnp.mean(x * x, axis=-1, keepdims=True)
    return (x * lax.rsqrt(ms + EPS) * gain) * (1.0 + scale) + shift


def _seg_rms(z, bd, inv_n):
    z2 = z * z
    hi = z2.astype(BF16)
    lo = (z2 - hi.astype(F32)).astype(BF16)
    ss = _dot(hi, bd) + _dot(lo, bd)
    return lax.rsqrt(ss * inv_n + EPS)


def _rope(z, cosw, sinw, first_half, half):
    w = z.shape[-1]
    up = pltpu.roll(z, w - half, axis=1)
    dn = pltpu.roll(z, half, axis=1)
    return z * cosw + jnp.where(first_half, up, dn) * sinw


def _ada_kernel(c_ref, w_ref, b_ref, o_ref):
    s = _silu(c_ref[...]).astype(BF16)
    o_ref[0] = _dot(s, w_ref[0].astype(BF16)) + b_ref[0]


def _ada_all(cvec, ada_w, ada_b):
    rows = cvec.shape[0]
    tn = 512
    return pl.pallas_call(
        _ada_kernel,
        grid=(DEPTH, 6 * D // tn),
        in_specs=[pl.BlockSpec((rows, D), lambda i, j: (0, 0)),
                  pl.BlockSpec((1, D, tn), lambda i, j: (i, 0, j)),
                  pl.BlockSpec((1, 1, tn), lambda i, j: (i, 0, j))],
        out_specs=pl.BlockSpec((1, rows, tn), lambda i, j: (i, 0, j)),
        out_shape=jax.ShapeDtypeStruct((DEPTH, rows, 6 * D), F32),
        compiler_params=_cparams(2), name="adaln",
    )(cvec, ada_w, ada_b.reshape(DEPTH, 1, 6 * D))


def _mod_spec():
    return pl.BlockSpec((1, 1, 6, D), lambda b, t: (b, jnp.minimum(t, 1), 0, 0))


def _row_spec(width):
    return pl.BlockSpec((1, TM, width), lambda b, t: (b, t, 0))


def _const_spec(shape):
    return pl.BlockSpec(shape, lambda b, t: (0,) * len(shape))


def _tab_spec(width):
    return pl.BlockSpec((TM, width), lambda b, t: (t, 0))


def _qkv_kernel(x_ref, mod_ref, g1_ref, w_ref, bd_ref, gq_ref, gk_ref, cos_ref, sin_ref,
                q_ref, k_ref, v_ref, *, nq, nk, rope):
    mod = mod_ref[0, 0]
    h = _prenorm(x_ref[0], g1_ref[...], mod[0:1], mod[1:2]).astype(BF16)
    acc = _dot(h, w_ref[...])
    bd = bd_ref[...]
    lane = lax.broadcasted_iota(jnp.int32, (1, 256), 1)
    first_half = (lane % 64) < 32
    cosw, sinw = cos_ref[...], sin_ref[...]

    def finish(z, gain):
        z = z * _seg_rms(z, bd, 1.0 / 64) * gain
        if rope:
            z = _rope(z, cosw, sinw, first_half, 32)
        return z.astype(BF16)

    for j in range(nq // 256):
        q_ref[0, :, j * 256:(j + 1) * 256] = finish(acc[:, j * 256:(j + 1) * 256], gq_ref[...])
    for j in range(nk // 256):
        k_ref[0, :, j * 256:(j + 1) * 256] = finish(acc[:, nq + j * 256:nq + (j + 1) * 256], gk_ref[...])
    v_ref[0] = acc[:, nq + nk:].astype(BF16)


def _qkv_project(x, modtab, g1, w, bd, gq, gk, cosw, sinw, nq, nk, rope):
    bsz = x.shape[0]
    return pl.pallas_call(
        functools.partial(_qkv_kernel, nq=nq, nk=nk, rope=rope),
        grid=(bsz, NTILES),
        in_specs=[_row_spec(D), _mod_spec(), _const_spec((1, D)), _const_spec((D, nq + 2 * nk)),
                  _const_spec((256, 256)), _const_spec((1, 256)), _const_spec((1, 256)),
                  _tab_spec(256), _tab_spec(256)],
        out_specs=[_row_spec(nq), _row_spec(nk), _row_spec(nk)],
        out_shape=[jax.ShapeDtypeStruct((bsz, NT, nq), BF16), jax.ShapeDtypeStruct((bsz, NT, nk), BF16),
                   jax.ShapeDtypeStruct((bsz, NT, nk), BF16)],
        compiler_params=_cparams(2), name="qkv_project",
    )(x, modtab, g1, w, bd, gq, gk, cosw, sinw)


def _mla_kernel(x_ref, mod_ref, g1_ref, w1_ref, gql_ref, gkl_ref, gkr_ref, kcos_ref, ksin_ref,
                wuq_ref, wk_ref, wv_ref, bd_ref, qgain_ref, qinv_ref, qcos_ref, qsin_ref, kgain_ref, kmask_ref,
                q_ref, k_ref, v_ref):
    mod = mod_ref[0, 0]
    h = _prenorm(x_ref[0], g1_ref[...], mod[0:1], mod[1:2]).astype(BF16)
    a = _dot(h, w1_ref[...])
    cq = a[:, :B_Q_RANK]
    cq = (cq * lax.rsqrt(jnp.mean(cq * cq, axis=-1, keepdims=True) + EPS) * gql_ref[...]).astype(BF16)
    ckv = a[:, B_Q_RANK:B_Q_RANK + B_KV_RANK]
    ckv = (ckv * lax.rsqrt(jnp.mean(ckv * ckv, axis=-1, keepdims=True) + EPS) * gkl_ref[...]).astype(BF16)
    kr = a[:, B_Q_RANK + B_KV_RANK:]
    kr = kr * lax.rsqrt(jnp.sum(kr * kr, axis=-1, keepdims=True) * (1.0 / B_ROPE) + EPS) * gkr_ref[...]
    lane128 = lax.broadcasted_iota(jnp.int32, (1, LANES), 1)
    kr = _rope(kr, kcos_ref[...], ksin_ref[...], (lane128 % 32) < 16, 16).astype(BF16)

    bd = bd_ref[...]
    lane = lax.broadcasted_iota(jnp.int32, (1, 256), 1)
    first_half = (lane % 32) < 16
    qcos, qsin = qcos_ref[...], qsin_ref[...]
    q2 = _dot(cq, wuq_ref[...])
    k2 = _dot(jnp.concatenate([ckv, kr], axis=1), wk_ref[...])
    kmask = kmask_ref[...] > 0.0
    for j in range(q2.shape[1] // 256):
        sl = slice(j * 256, (j + 1) * 256)
        z = q2[:, sl]
        z = z * _seg_rms(z, bd, qinv_ref[...]) * qgain_ref[...]
        q_ref[0, :, sl] = _rope(z, qcos, qsin, first_half, 16).astype(BF16)
        z = k2[:, sl]
        k_ref[0, :, sl] = (z * jnp.where(kmask, _seg_rms(z, bd, 1.0 / 64) * kgain_ref[...], 1.0)).astype(BF16)
    v_ref[0] = _dot(ckv, wv_ref[...]).astype(BF16)


def _mla_project(x, modtab, g1, p):
    bsz = x.shape[0]
    hq = B_HEADS * LANES
    return pl.pallas_call(
        _mla_kernel,
        grid=(bsz, NTILES),
        in_specs=[_row_spec(D), _mod_spec(), _const_spec((1, D)), _const_spec((D, 768)),
                  _const_spec((1, B_Q_RANK)), _const_spec((1, B_KV_RANK)), _const_spec((1, LANES)),
                  _tab_spec(LANES), _tab_spec(LANES),
                  _const_spec((B_Q_RANK, hq)), _const_spec((B_KV_RANK + LANES, hq)),
                  _const_spec((B_KV_RANK, B_HEADS * B_V)), _const_spec((256, 256)),
                  _const_spec((1, 256)), _const_spec((1, 256)), _tab_spec(256), _tab_spec(256),
                  _const_spec((1, 256)), _const_spec((1, 256))],
        out_specs=[_row_spec(hq), _row_spec(hq), _row_spec(B_HEADS * B_V)],
        out_shape=[jax.ShapeDtypeStruct((bsz, NT, hq), BF16), jax.ShapeDtypeStruct((bsz, NT, hq), BF16),
                   jax.ShapeDtypeStruct((bsz, NT, B_HEADS * B_V), BF16)],
        compiler_params=_cparams(2), name="mla_project",
    )(x, modtab, g1, p["w1"], p["gql"], p["gkl"], p["gkr"], p["kcos"], p["ksin"], p["wuq"], p["wk"], p["wv"],
      p["bd"], p["qgain"], p["qinv"], p["qcos"], p["qsin"], p["kgain"], p["kmask"])


def _softmax_pv(sa, sb, v):
    outs = []
    for s in (sa, sb):
        p = jnp.exp2(s - jnp.max(s, axis=-1, keepdims=True))
        l = jnp.sum(p, axis=-1, keepdims=True)
        outs.append(_dot(p.astype(BF16), v) / l)
    lane = lax.broadcasted_iota(jnp.int32, (1, LANES), 1)
    return jnp.where(lane < 64, outs[0], outs[1])


def _split_heads(q):
    lane = lax.broadcasted_iota(jnp.int32, (1, LANES), 1)
    zero = jnp.zeros_like(q)
    return jnp.where(lane < 64, q, zero), jnp.where(lane >= 64, q, zero)


PAIRS_PER_STEP = 4


def _pair_attn_kernel(q_ref, k_ref, v_ref, o_ref, *, wide):
    t = pl.program_id(2)

    def run(nkeys):
        for i in range(PAIRS_PER_STEP):
            if wide:
                q = q_ref[0, :, 256 * i:256 * (i + 1)]
                k = k_ref[0, :nkeys, 256 * i:256 * (i + 1)]
                sa, sb = _dot_t(q[:, :LANES], k[:, :LANES]), _dot_t(q[:, LANES:], k[:, LANES:])
                v = v_ref[0, :nkeys, LANES * i:LANES * (i + 1)]
            else:
                qa, qb = _split_heads(q_ref[0, :, LANES * i:LANES * (i + 1)])
                k = k_ref[0, :nkeys]
                sa, sb = _dot_t(qa, k), _dot_t(qb, k)
                v = v_ref[0, :nkeys]
            o_ref[0, :, LANES * i:LANES * (i + 1)] = _softmax_pv(sa, sb, v).astype(BF16)

    @pl.when(t == 0)
    def _():
        run(CTX)

    @pl.when(t > 0)
    def _():
        run(NT)


def _pair_attention(q, k, v, *, wide, q_pairs_per_kv):
    bsz = q.shape[0]
    qw = 256 if wide else LANES
    n_steps = q.shape[2] // (qw * PAIRS_PER_STEP)
    kv_pairs = PAIRS_PER_STEP if wide else 1
    kv_idx = lambda b, p, t: (b, 0, p * PAIRS_PER_STEP // (q_pairs_per_kv * kv_pairs))
    return pl.pallas_call(
        functools.partial(_pair_attn_kernel, wide=wide),
        grid=(bsz, n_steps, NTILES),
        in_specs=[pl.BlockSpec((1, TM, qw * PAIRS_PER_STEP), lambda b, p, t: (b, t, p)),
                  pl.BlockSpec((1, NT, qw * kv_pairs), kv_idx),
                  pl.BlockSpec((1, NT, LANES * kv_pairs), kv_idx)],
        out_specs=pl.BlockSpec((1, TM, LANES * PAIRS_PER_STEP), lambda b, p, t: (b, t, p)),
        out_shape=jax.ShapeDtypeStruct((bsz, NT, n_steps * PAIRS_PER_STEP * LANES), BF16),
        compiler_params=_cparams(3), name="pair_attention",
    )(q, k, v)


def _na_attn_kernel(q_ref, kc_ref, k0_ref, k1_ref, k2_ref, vc_ref, v0_ref, v1_ref, v2_ref, bias_ref, o_ref):
    t = pl.program_id(1)

    @pl.when(t == 0)
    def _():
        for i in range(PAIRS_PER_STEP):
            sl = slice(LANES * i, LANES * (i + 1))
            qa, qb = _split_heads(q_ref[0, :, sl])
            k = kc_ref[0, :, sl]
            o_ref[0, :, sl] = _softmax_pv(_dot_t(qa, k), _dot_t(qb, k), vc_ref[0, :, sl]).astype(BF16)

    @pl.when(t > 0)
    def _():
        for i in range(PAIRS_PER_STEP):
            sl = slice(LANES * i, LANES * (i + 1))
            qa, qb = _split_heads(q_ref[0, :, sl])
            k = jnp.concatenate([k0_ref[0, :, sl], k1_ref[0, :, sl], k2_ref[0, :, sl], kc_ref[0, :, sl]], axis=0)
            v = jnp.concatenate([v0_ref[0, :, sl], v1_ref[0, :, sl], v2_ref[0, :, sl], vc_ref[0, :, sl]], axis=0)
            sa = _dot_t(qa, k) + bias_ref[0, 2 * i]
            sb = _dot_t(qb, k) + bias_ref[0, 2 * i + 1]
            o_ref[0, :, sl] = _softmax_pv(sa, sb, v).astype(BF16)


def _na_band_tile(t):
    return 1 + jnp.clip(t - 2, 0, LAT_TILES - BAND_TILES)


def _na_attention(q, k, v, bias):
    bsz = q.shape[0]
    n_steps = C_HEADS // (2 * PAIRS_PER_STEP)
    width = LANES * PAIRS_PER_STEP
    qspec = pl.BlockSpec((1, TM, width), lambda p, t, b: (b, t, p))
    cspec = pl.BlockSpec((1, TM, width), lambda p, t, b: (b, 0, p))
    bands = [pl.BlockSpec((1, TM, width), functools.partial(lambda p, t, b, i: (b, _na_band_tile(t) + i, p), i=i))
             for i in range(BAND_TILES)]
    variant = lambda t: jnp.where(t <= 1, 0, jnp.where(t == LAT_TILES, 2, 1))
    return pl.pallas_call(
        _na_attn_kernel,
        grid=(n_steps, NTILES, bsz),
        in_specs=[qspec, cspec] + bands + [cspec] + bands
                 + [pl.BlockSpec((1, 2 * PAIRS_PER_STEP, TM, (BAND_TILES + 1) * TM),
                                 lambda p, t, b: (variant(t), p, 0, 0))],
        out_specs=qspec,
        out_shape=jax.ShapeDtypeStruct((bsz, NT, C_HEADS * C_HEAD_DIM), BF16),
        compiler_params=_cparams(3), name="na_attention",
    )(q, k, k, k, k, v, v, v, v, bias)


def _out_router_kernel(o_ref, wo_ref, x_ref, mod_ref, g2_ref, r_ref, xo_ref, h_ref, aff_ref):
    mod = mod_ref[0, 0]
    x = x_ref[0] + mod[2:3] * _dot(o_ref[0], wo_ref[...])
    xo_ref[0] = x
    h = _prenorm(x, g2_ref[...], mod[3:4], mod[4:5]).astype(BF16)
    for r, word in enumerate(_pack_rows(h)):
        h_ref[0, pl.ds(r, TM, stride=HS_ROWS), :] = word
    logits = _dot(h, r_ref[...])
    lane = lax.broadcasted_iota(jnp.int32, (1, LANES), 1)
    logits = jnp.where(lane < N_EXPERTS, logits, NEG_INF)
    e = jnp.exp(logits - jnp.max(logits, axis=-1, keepdims=True))
    aff_ref[0] = e / jnp.sum(e, axis=-1, keepdims=True)


def _out_router(o, wo, x, modtab, g2, router):
    bsz = x.shape[0]
    return pl.pallas_call(
        _out_router_kernel,
        grid=(bsz, NTILES),
        in_specs=[_row_spec(D), _const_spec((D, D)), _row_spec(D), _mod_spec(), _const_spec((1, D)),
                  _const_spec((D, LANES))],
        out_specs=[_row_spec(D), pl.BlockSpec((1, TM * HS_ROWS, LANES), lambda b, t: (b, t, 0)), _row_spec(LANES)],
        out_shape=[jax.ShapeDtypeStruct((bsz, NT, D), F32),
                   jax.ShapeDtypeStruct((bsz, NT * HS_ROWS, LANES), jnp.int32),
                   jax.ShapeDtypeStruct((bsz, NT, LANES), F32)],
        input_output_aliases={2: 0},
        compiler_params=_cparams(2), name="out_router",
    )(o, wo, x, modtab, g2, router)


def _excl_cumsum_lanes(m, triu):
    out = []
    offset = jnp.zeros((m.shape[0], 1), F32)
    for i in range(m.shape[1] // TM):
        blk = m[:, i * TM:(i + 1) * TM]
        out.append(_dot(blk.astype(BF16), triu) + offset)
        offset = offset + jnp.sum(blk, axis=1, keepdims=True)
    return jnp.concatenate(out, axis=1) if len(out) > 1 else out[0]


def _kth_largest_bits(bits_list, caps):
    def count(bits, cand):
        return jnp.sum(jnp.where(bits >= cand, 1.0, 0.0), axis=1, keepdims=True)

    def step(i, thrs):
        b1 = jnp.left_shift(jnp.int32(1), 29 - 2 * i)
        b0 = jnp.left_shift(jnp.int32(1), 28 - 2 * i)
        out = []
        for bits, cap, thr in zip(bits_list, caps, thrs):
            c1, c2, c3 = count(bits, thr | b1), count(bits, thr | b0), count(bits, thr | b1 | b0)
            with_b1 = jnp.where(c3 >= cap, thr | b1 | b0, thr | b1)
            without = jnp.where(c2 >= cap, thr | b0, thr)
            out.append(jnp.where(c1 >= cap, with_b1, without))
        return tuple(out)

    top = jnp.int32(1 << 30)
    init = tuple(jnp.where(count(bits, top) >= cap, top, jnp.zeros((N_EXPERTS, 1), jnp.int32))
                 for bits, cap in zip(bits_list, caps))
    return lax.fori_loop(0, 15, step, init)


def _topk_slots(bits, thr, cap, triu):
    gt = bits > thr
    eq = bits == thr
    need = cap - jnp.sum(jnp.where(gt, 1.0, 0.0), axis=1, keepdims=True)
    eq_rank = _excl_cumsum_lanes(jnp.where(eq, 1.0, 0.0), triu)
    sel = gt | (eq & (eq_rank < need))
    rank = _excl_cumsum_lanes(jnp.where(sel, 1.0, 0.0), triu)
    return jnp.where(sel, rank, -1.0), rank


OFF_LANE = 3 * N_EXPERTS


def _token_values(aff, first_row):
    n = aff.shape[0]
    hi = aff.astype(BF16).astype(F32)
    r1 = aff - hi
    mid = r1.astype(BF16).astype(F32)
    lo = (r1 - mid).astype(BF16).astype(F32)
    off = (lax.broadcasted_iota(jnp.int32, (n, 1), 0) + first_row) * HS_ROWS
    lane = lax.broadcasted_iota(jnp.int32, (1, LANES), 1)
    vals = (hi + pltpu.roll(mid, N_EXPERTS, axis=1) + pltpu.roll(lo, 2 * N_EXPERTS, axis=1)
            + jnp.where(lane == OFF_LANE, (off >> 7).astype(F32), 0.0)
            + jnp.where(lane == OFF_LANE + 1, (off & (LANES - 1)).astype(F32), 0.0))
    return vals.astype(BF16)


def _topk_kernel(aff_t_ref, aff_ref, slot_ref, off_ref, gate_ref, start_ref):
    r = lax.broadcasted_iota(jnp.int32, (TM, TM), 0)
    c = lax.broadcasted_iota(jnp.int32, (TM, TM), 1)
    triu = jnp.where(r < c, 1.0, 0.0).astype(BF16)
    lane = lax.broadcasted_iota(jnp.int32, (1, LANES), 1)
    segments = ((0, CTX, CAP_CTX, CAP_LAT), (CTX, NT, CAP_LAT, 0))
    bits = [pltpu.bitcast(aff_t_ref[0, :, lo:hi], jnp.int32) for lo, hi, _, _ in segments]
    thrs = _kth_largest_bits(bits, [cap for _, _, cap, _ in segments])
    for (lo, hi, cap, dst), seg_bits, thr in zip(segments, bits, thrs):
        slot, rank = _topk_slots(seg_bits, thr, cap, triu)
        slot_ref[0, :, lo:hi] = slot.astype(jnp.int32)
        if lo == CTX:
            starts = jnp.where(lane == LAT_TILES, float(cap), 0.0)
            for j in range(LAT_TILES):
                starts = starts + jnp.where(lane == j, rank[:, TM * j:TM * j + 1], 0.0)
            start_ref[0] = starts.astype(jnp.int32)
        vals = _token_values(aff_ref[0, lo:hi], lo)
        slot_ids = lax.broadcasted_iota(jnp.int32, (cap, 1), 0).astype(F32)
        for e in range(N_EXPERTS):
            hit = jnp.where(slot_ids == slot[e:e + 1, :], 1.0, 0.0).astype(BF16)
            picked = _dot(hit, vals)
            mine = ((lane & (N_EXPERTS - 1)) == e) & (lane < OFF_LANE)
            gate = jnp.sum(jnp.where(mine, picked, 0.0), axis=1, keepdims=True)
            off = picked[:, OFF_LANE:OFF_LANE + 1] * float(LANES) + picked[:, OFF_LANE + 1:OFF_LANE + 2]
            gate_ref[0, e, dst:dst + cap, :] = gate
            off_ref[0, e, dst:dst + cap, :] = off.astype(jnp.int32)


def _topk(aff_t, aff):
    bsz = aff.shape[0]
    tspec = pl.BlockSpec((1, N_EXPERTS, NT), lambda b: (b, 0, 0))
    lspec = pl.BlockSpec((1, N_EXPERTS, CAP_ALL, 1), lambda b: (b, 0, 0, 0))
    return pl.pallas_call(
        _topk_kernel, grid=(bsz,),
        in_specs=[tspec, pl.BlockSpec((1, NT, LANES), lambda b: (b, 0, 0))],
        out_specs=[tspec, lspec, lspec, pl.BlockSpec((1, N_EXPERTS, LANES), lambda b: (b, 0, 0))],
        out_shape=[jax.ShapeDtypeStruct((bsz, N_EXPERTS, NT), jnp.int32),
                   jax.ShapeDtypeStruct((bsz, N_EXPERTS, CAP_ALL, 1), jnp.int32),
                   jax.ShapeDtypeStruct((bsz, N_EXPERTS, CAP_ALL, 1), F32),
                   jax.ShapeDtypeStruct((bsz, N_EXPERTS, LANES), jnp.int32)],
        compiler_params=_cparams(1), name="expert_topk",
    )(aff_t, aff)


def _pack_rows(hb):
    words = []
    for r in range(HS_ROWS):
        lo = pltpu.bitcast(hb[:, 256 * r:256 * r + LANES].astype(F32), jnp.int32)
        hi = pltpu.bitcast(hb[:, 256 * r + LANES:256 * (r + 1)].astype(F32), jnp.int32)
        words.append(lax.shift_right_logical(lo, jnp.int32(16)) | hi)
    return words


def _unpack_rows(words):
    cols = []
    for w in words:
        cols.append(pltpu.bitcast(lax.shift_left(w, jnp.int32(16)), F32).astype(BF16))
        cols.append(pltpu.bitcast(w & jnp.int32(-65536), F32).astype(BF16))
    return jnp.concatenate(cols, axis=1)


def _expert_kernel(hs_ref, off_ref, gate_ref, wg_ref, wu_ref, wd_ref, *refs, n_slots):
    out_refs, (tile_ref, wg_s, wu_s, wd_s) = refs[:-4], refs[-4:]

    @pl.when(pl.program_id(1) == 0)
    def _():
        wg_s[...] = wg_ref[0, 0].astype(BF16)
        wu_s[...] = wu_ref[0, 0].astype(BF16)
        wd_s[...] = wd_ref[0, 0].astype(BF16)

    for j in range(n_slots):
        src = pl.multiple_of(off_ref[0, 0, 0, j], HS_ROWS)
        tile_ref[pl.ds(j, HS_ROWS, stride=TILE_STRIDE), :] = hs_ref[0, pl.ds(src, HS_ROWS), :]
    xg = _unpack_rows([tile_ref[r * TILE_STRIDE:r * TILE_STRIDE + n_slots, :] for r in range(HS_ROWS)])
    hid = (_silu(_dot(xg, wg_s[...])) * _dot(xg, wu_s[...])).astype(BF16)
    y = _dot(hid, wd_s[...]) * gate_ref[0, 0, :n_slots]
    out_refs[0][0] = y[:CAP_LAT].astype(BF16)
    if n_slots > CAP_LAT:
        out_refs[1][0] = y[CAP_LAT:].astype(BF16)


def _experts(hs, off, gate, wg, wu, wd, layer, with_ctx):
    bsz = hs.shape[0]
    n_slots = CAP_ALL if with_ctx else CAP_LAT
    wspec = pl.BlockSpec((1, 1, D, EXPERT_FF), lambda e, b: (layer, e, 0, 0))
    out_specs = [pl.BlockSpec((1, CAP_LAT, D), lambda e, b: (b, e, 0))]
    out_shape = [jax.ShapeDtypeStruct((bsz, N_EXPERTS * CAP_LAT, D), BF16)]
    if with_ctx:
        out_specs.append(pl.BlockSpec((1, CAP_CTX, D), lambda e, b: (b, e, 0)))
        out_shape.append(jax.ShapeDtypeStruct((bsz, N_EXPERTS * CAP_CTX, D), BF16))
    return pl.pallas_call(
        functools.partial(_expert_kernel, n_slots=n_slots),
        grid=(N_EXPERTS, bsz),
        in_specs=[pl.BlockSpec((1, NT * HS_ROWS, LANES), lambda e, b: (b, 0, 0)),
                  pl.BlockSpec((1, 1, 1, CAP_ALL), lambda e, b: (b, e, 0, 0), memory_space=pltpu.SMEM),
                  pl.BlockSpec((1, 1, CAP_ALL, 1), lambda e, b: (b, e, 0, 0)),
                  wspec, wspec, pl.BlockSpec((1, 1, EXPERT_FF, D), lambda e, b: (layer, e, 0, 0))],
        out_specs=out_specs, out_shape=out_shape,
        scratch_shapes=[pltpu.VMEM((HS_ROWS * TILE_STRIDE, LANES), jnp.int32),
                        pltpu.VMEM((D, EXPERT_FF), BF16), pltpu.VMEM((D, EXPERT_FF), BF16),
                        pltpu.VMEM((EXPERT_FF, D), BF16)],
        compiler_params=_cparams(2), name="experts",
    )(hs, off, gate, wg, wu, wd)


WIN = 64
WIN_GROUP = 256 // WIN


def _combine_lat(slot_ref, yl_ref, win_ref, x_ref, gate, xo_ref):
    slot = slot_ref[0]
    fits = win_ref[0, 0, 0, N_EXPERTS] - win_ref[0, 0, 0, 0] <= WIN
    for e in range(1, N_EXPERTS):
        fits = jnp.logical_and(fits, win_ref[0, 0, 0, N_EXPERTS + e] - win_ref[0, 0, 0, e] <= WIN)

    @pl.when(fits)
    def _():
        cols = lax.broadcasted_iota(jnp.int32, (TM, WIN * WIN_GROUP), 1)
        acc = None
        for g in range(N_EXPERTS // WIN_GROUP):
            hit = cols < 0
            rows = []
            for j in range(WIN_GROUP):
                e = WIN_GROUP * g + j
                start = win_ref[0, 0, 0, e]
                s = slot[:, e:e + 1]
                hit = hit | (jnp.where(s >= 0, s - start + WIN * j, -1) == cols)
                rows.append(yl_ref[0, pl.ds(pl.multiple_of(e * CAP_LAT + start, 16), WIN), :])
            part = _dot(jnp.where(hit, 1.0, 0.0).astype(BF16), jnp.concatenate(rows, axis=0))
            acc = part if acc is None else acc + part
        xo_ref[0] = x_ref[0] + gate * acc

    @pl.when(jnp.logical_not(fits))
    def _():
        cols = lax.broadcasted_iota(jnp.int32, (TM, CAP_LAT), 1)
        onehot = jnp.concatenate(
            [jnp.where(slot[:, e:e + 1] == cols, 1.0, 0.0).astype(BF16) for e in range(N_EXPERTS)], axis=1)
        xo_ref[0] = x_ref[0] + gate * _dot(onehot, yl_ref[0])


def _combine_kernel(slot_ref, yl_ref, yc_ref, win_ref, x_ref, mod_ref, xo_ref):
    t = pl.program_id(1)
    gate = mod_ref[0, 0][5:6]

    @pl.when(t == 0)
    def _():
        slot = slot_ref[0]
        cols = lax.broadcasted_iota(jnp.int32, (TM, N_EXPERTS * CAP_CTX), 1)
        hit = cols < 0
        for e in range(N_EXPERTS):
            s = slot[:, e:e + 1]
            hit = hit | (jnp.where(s >= 0, s + e * CAP_CTX, -1) == cols)
        xo_ref[0] = x_ref[0] + gate * _dot(jnp.where(hit, 1.0, 0.0).astype(BF16), yc_ref[0])

    @pl.when(t > 0)
    def _():
        _combine_lat(slot_ref, yl_ref, win_ref, x_ref, gate, xo_ref)


def _win_spec(tile_of_step):
    return pl.BlockSpec((1, 1, 1, 2 * N_EXPERTS), lambda b, t: (b, tile_of_step(t), 0, 0), memory_space=pltpu.SMEM)


def _combine(slot, yl, yc, win, x, modtab):
    bsz = x.shape[0]
    return pl.pallas_call(
        _combine_kernel,
        grid=(bsz, NTILES),
        in_specs=[_row_spec(LANES), pl.BlockSpec((1, N_EXPERTS * CAP_LAT, D), lambda b, t: (b, 0, 0)),
                  pl.BlockSpec((1, N_EXPERTS * CAP_CTX, D), lambda b, t: (b, 0, 0)),
                  _win_spec(lambda t: jnp.maximum(t - 1, 0)), _row_spec(D), _mod_spec()],
        out_specs=_row_spec(D),
        out_shape=jax.ShapeDtypeStruct((bsz, NT, D), F32),
        input_output_aliases={4: 0},
        compiler_params=_cparams(2), name="moe_combine",
    )(slot, yl, yc, win, x, modtab)


def _combine_last_kernel(slot_ref, yl_ref, win_ref, x_ref, mod_ref, xo_ref):
    _combine_lat(slot_ref, yl_ref, win_ref, x_ref, mod_ref[0, 0][5:6], xo_ref)


def _combine_last(slot, yl, win, x, modtab):
    bsz = x.shape[0]
    lat = lambda width: pl.BlockSpec((1, TM, width), lambda b, t: (b, t + 1, 0))
    return pl.pallas_call(
        _combine_last_kernel,
        grid=(bsz, LAT_TILES),
        in_specs=[lat(LANES), pl.BlockSpec((1, N_EXPERTS * CAP_LAT, D), lambda b, t: (b, 0, 0)),
                  _win_spec(lambda t: t), lat(D), pl.BlockSpec((1, 1, 6, D), lambda b, t: (b, 1, 0, 0))],
        out_specs=pl.BlockSpec((1, TM, D), lambda b, t: (b, t, 0)),
        out_shape=jax.ShapeDtypeStruct((bsz, SEQ, D), F32),
        compiler_params=_cparams(2), name="moe_combine_last",
    )(slot, yl, win, x, modtab)


def _rope_tables(rot_dim, width, lane_lo, lane_hi):
    n_freq = rot_dim // 4
    half = rot_dim // 2
    inv = jnp.float32(ROPE_THETA) ** (-jnp.arange(n_freq, dtype=F32) / n_freq)
    t = jnp.arange(SEQ, dtype=jnp.int32)
    row = (t // GRID_W).astype(F32)
    col = (t % GRID_W).astype(F32)
    ang = jnp.concatenate([row[:, None] * inv, col[:, None] * inv], axis=-1)
    cos, sin = jnp.cos(ang), jnp.sin(ang)
    lane = np.arange(width)
    inside = (lane % LANES >= lane_lo) & (lane % LANES < lane_hi)
    idx = lane % half
    sign = np.where(lane % rot_dim < half, -1.0, 1.0).astype(np.float32)
    cosw = jnp.where(inside[None, :], cos[:, idx], 1.0)
    sinw = jnp.where(inside[None, :], sin[:, idx] * sign[None, :], 0.0)
    ones = jnp.ones((CTX, width), F32)
    return jnp.concatenate([ones, cosw], axis=0), jnp.concatenate([0.0 * ones, sinw], axis=0)


def _block_diag_ones():
    i = np.arange(256)
    return jnp.asarray((i[:, None] // 64) == (i[None, :] // 64), dtype=BF16)


_GQA_ORDER = np.array([8 * kp + 4 * odd + i for kp in range(2) for i in range(4) for odd in range(2)])


def _na_bias(rpb):
    out = []
    rows = SEQ // GRID_W
    qrows, krows = TM // GRID_W, BAND_TILES * TM // GRID_W
    col = np.arange(GRID_W)
    cs = np.clip(col - NA_COLS // 2, 0, GRID_W - NA_COLS)
    ok_c = (col[None, :] >= cs[:, None]) & (col[None, :] < cs[:, None] + NA_COLS)
    dc = np.clip(col[None, :] - col[:, None] + NA_COLS - 1, 0, 2 * NA_COLS - 2)
    pick_c = (dc[:, :, None] == np.arange(2 * NA_COLS - 1)).astype(np.float32)
    for r0, bs in ((0, 0), (8, 4), (rows - 4, rows - 12)):
        r, kr = r0 + np.arange(qrows), bs + np.arange(krows)
        rs = np.clip(r - NA_ROWS // 2, 0, rows - NA_ROWS)
        ok_r = (kr[None, :] >= rs[:, None]) & (kr[None, :] < rs[:, None] + NA_ROWS)
        dr = kr[None, :] - r[:, None] + NA_ROWS - 1
        pick_r = (dr[:, :, None] == np.arange(2 * NA_ROWS - 1)).astype(np.float32)
        t1 = jnp.einsum('ard,hde->hare', pick_r, rpb.astype(F32), precision=lax.Precision.HIGHEST)
        band = jnp.einsum('hare,cse->hacrs', t1, pick_c, precision=lax.Precision.HIGHEST)
        ok = ok_r[:, None, :, None] & ok_c[None, :, None, :]
        band = jnp.where(ok[None], band * LOG2E, NEG_INF).reshape(C_HEADS, TM, BAND_TILES * TM)
        out.append(jnp.concatenate([band, jnp.zeros((C_HEADS, TM, TM), F32)], axis=-1))
    return jnp.stack(out)


def _gqa_params(w_qkv, qn, kn, wo):
    nq, nk = A_HEADS * A_HEAD_DIM, A_KV_HEADS * A_HEAD_DIM
    wq = w_qkv[:, :nq].reshape(D, A_HEADS, A_HEAD_DIM)[:, _GQA_ORDER].reshape(D, nq)
    w = jnp.concatenate([wq, w_qkv[:, nq:]], axis=1).astype(BF16)
    wo_p = wo.reshape(A_HEADS, A_HEAD_DIM, D)[_GQA_ORDER].reshape(nq, D).astype(BF16)
    scale = A_HEAD_DIM ** -0.5 * LOG2E
    return w, (jnp.tile(qn, 4) * scale)[None], jnp.tile(kn, 4)[None], wo_p


def _mla_params(w_dq, qn_lat, w_uq, w_dkv, kvn_lat, w_ukv, qn, kn, cossin_q, cossin_k):
    qd = B_NOPE + B_ROPE
    w1 = jnp.concatenate([w_dq, w_dkv, jnp.zeros((D, 768 - B_Q_RANK - B_KV_RANK - B_ROPE), F32)], axis=1)
    pad_q = jnp.zeros((B_Q_RANK, B_HEADS, LANES - qd), F32)
    wuq = jnp.concatenate([w_uq.reshape(B_Q_RANK, B_HEADS, qd), pad_q], axis=2).reshape(B_Q_RANK, B_HEADS * LANES)
    ukv = w_ukv.reshape(B_KV_RANK, B_HEADS, B_NOPE + B_V)
    wk_top = jnp.concatenate([ukv[:, :, :B_NOPE], jnp.zeros((B_KV_RANK, B_HEADS, LANES - B_NOPE), F32)], axis=2)
    route = np.zeros((LANES, B_HEADS, LANES), np.float32)
    route[np.arange(B_ROPE), :, B_NOPE + np.arange(B_ROPE)] = 1.0
    wk = jnp.concatenate([wk_top.reshape(B_KV_RANK, -1), jnp.asarray(route).reshape(LANES, -1)], axis=0)
    wv = ukv[:, :, B_NOPE:].reshape(B_KV_RANK, B_HEADS * B_V)
    scale = qd ** -0.5 * LOG2E
    zpad = jnp.zeros((LANES - qd,), F32)
    qgain = jnp.tile(jnp.concatenate([qn * scale, zpad]), 2)[None]
    qinv = jnp.tile(jnp.concatenate([jnp.full((B_NOPE,), 1.0 / B_NOPE), jnp.full((LANES - B_NOPE,), 1.0 / B_ROPE)]), 2)
    kgain = jnp.tile(jnp.concatenate([kn[:B_NOPE], jnp.zeros((LANES - B_NOPE,), F32)]), 2)[None]
    kmask = jnp.tile(jnp.concatenate([jnp.ones((B_NOPE,), F32), jnp.zeros((LANES - B_NOPE,), F32)]), 2)[None]
    gkr = jnp.concatenate([kn[B_NOPE:], jnp.zeros((LANES - B_ROPE,), F32)])[None]
    return dict(w1=w1.astype(BF16), gql=qn_lat[None], gkl=kvn_lat[None], gkr=gkr, kcos=cossin_k[0], ksin=cossin_k[1],
                wuq=wuq.astype(BF16), wk=wk.astype(BF16), wv=wv.astype(BF16), bd=_block_diag_ones(),
                qgain=qgain, qinv=qinv[None].astype(F32), qcos=cossin_q[0], qsin=cossin_q[1], kgain=kgain, kmask=kmask)


def kernel(x, c, ctx, c_ctx, ada_w, ada_b, norm1_w, norm2_w, a_wqkv, a_qnorm, a_knorm, a_wo, b_wdq, b_qnorm_lat, b_wuq, b_wdkv, b_kvnorm_lat, b_wukv, b_qnorm, b_knorm, b_wo, c_wqkv, c_qnorm, c_knorm, c_rpb, c_wo, moe_router, moe_wg, moe_wu, moe_wd):
    bsz = x.shape[0]
    assert x.shape[1:] == (SEQ, D) and ctx.shape[1:] == (CTX, D)
    mod_rows = -(-(bsz + 1) // 16) * 16
    cvec = jnp.concatenate([c, c_ctx[None], jnp.zeros((mod_rows - bsz - 1, D), F32)], axis=0)
    mods = _ada_all(cvec, ada_w, ada_b)
    xs = jnp.concatenate([ctx, x], axis=1)

    bd = _block_diag_ones()
    rope64 = _rope_tables(A_HEAD_DIM, 256, 0, LANES)
    rope_mla_q = _rope_tables(B_ROPE, 256, B_NOPE, B_NOPE + B_ROPE)
    rope_mla_k = _rope_tables(B_ROPE, LANES, 0, B_ROPE)
    router = jnp.pad(moe_router, ((0, 0), (0, 0), (0, LANES - N_EXPERTS))).astype(BF16)

    for i in range(DEPTH):
        last = i == DEPTH - 1
        kind, j = i % 3, i // 3
        m = mods[i]
        m_lat = m[:bsz].reshape(bsz, 6, D)
        m_ctx = jnp.broadcast_to(m[bsz].reshape(1, 6, D), (bsz, 6, D))
        modtab = jnp.stack([m_ctx, m_lat], axis=1)
        g1, g2 = norm1_w[i][None], norm2_w[i][None]

        if kind == 0:
            w, gq, gk, wo = _gqa_params(a_wqkv[j], a_qnorm[j], a_knorm[j], a_wo[j])
            q, k, v = _qkv_project(xs, modtab, g1, w, bd, gq, gk, rope64[0], rope64[1],
                                   A_HEADS * A_HEAD_DIM, A_KV_HEADS * A_HEAD_DIM, True)
            o = _pair_attention(q, k, v, wide=False, q_pairs_per_kv=4)
        elif kind == 1:
            p = _mla_params(b_wdq[j], b_qnorm_lat[j], b_wuq[j], b_wdkv[j], b_kvnorm_lat[j], b_wukv[j],
                            b_qnorm[j], b_knorm[j], rope_mla_q, rope_mla_k)
            q, k, v = _mla_project(xs, modtab, g1, p)
            o = _pair_attention(q, k, v, wide=True, q_pairs_per_kv=1)
            wo = b_wo[j].astype(BF16)
        else:
            hd = C_HEADS * C_HEAD_DIM
            gq = (jnp.tile(c_qnorm[j], 4) * (C_HEAD_DIM ** -0.5 * LOG2E))[None]
            gk = jnp.tile(c_knorm[j], 4)[None]
            q, k, v = _qkv_project(xs, modtab, g1, c_wqkv[j].astype(BF16), bd, gq, gk, rope64[0], rope64[1],
                                   hd, hd, False)
            o = _na_attention(q, k, v, _na_bias(c_rpb[j]))
            wo = c_wo[j].astype(BF16)

        xs, hs, aff = _out_router(o, wo, xs, modtab, g2, router[i])
        slot_t, off, gate, starts = _topk(jnp.swapaxes(aff[:, :, :N_EXPERTS], 1, 2), aff)
        slot = jnp.pad(jnp.swapaxes(slot_t, 1, 2), ((0, 0), (0, 0), (0, LANES - N_EXPERTS)), constant_values=-1)
        run = jnp.swapaxes(starts[:, :, :LAT_TILES + 1], 1, 2)
        win_start = jnp.minimum(run[:, :LAT_TILES] // 16 * 16, CAP_LAT - WIN)
        win = jnp.concatenate([win_start, run[:, 1:]], axis=-1)[:, :, None, :]
        ys = _experts(hs, off.reshape(bsz, N_EXPERTS, 1, CAP_ALL), gate,
                      moe_wg, moe_wu, moe_wd, layer=i, with_ctx=not last)
        if last:
            return _combine_last(slot, ys[0], win, xs, modtab)
        xs = _combine(slot, ys[0], ys[1], win, xs, modtab)
```

```python
import functools

import numpy as np
import jax
import jax.numpy as jnp
from jax import lax
from jax.experimental import pallas as pl
from jax.experimental.pallas import tpu as pltpu

F32 = jnp.float32
BF16 = jnp.bfloat16

D = 1024
SEQ = 2048
CTX = 256
NT = CTX + SEQ
DEPTH = 4
GRID_W = 64
ROPE_THETA = 10000.0
EPS = 1e-6
NEG_INF = -1e30
LOG2E = 1.4426950408889634

TM = 256
NTILES = NT // TM
LAT_TILES = SEQ // TM

A_HEADS, A_KV_HEADS, A_HEAD_DIM = 16, 4, 64
B_HEADS, B_Q_RANK, B_KV_RANK, B_NOPE, B_ROPE, B_V = 16, 384, 256, 64, 32, 64
C_HEADS, C_HEAD_DIM, NA_ROWS, NA_COLS = 16, 64, 8, 16
N_EXPERTS, EXPERT_FF = 16, 1024
CAP_LAT = 2 * SEQ // N_EXPERTS
CAP_CTX = 2 * CTX // N_EXPERTS
CAP_ALL = CAP_LAT + CAP_CTX
HS_ROWS = D // 256
TILE_STRIDE = 296
LANES = 128
BAND_TILES = 3

VMEM_LIMIT = 56 * 1024 * 1024


def _cparams(n_axes):
    return pltpu.CompilerParams(dimension_semantics=("arbitrary",) * n_axes, vmem_limit_bytes=VMEM_LIMIT)


def _dot(a, b):
    return jnp.dot(a, b, preferred_element_type=F32)


def _dot_t(a, b):
    return lax.dot_general(a, b, (((1,), (1,)), ((), ())), preferred_element_type=F32)


def _silu(x):
    return x / (1.0 + jnp.exp(-x))


def _prenorm(x, gain, shift, scale):
    ms = jnp.mean(x * x, axis=-1, keepdims=True)
    return (x * lax.rsqrt(ms + EPS) * gain) * (1.0 + scale) + shift


def _seg_rms(z, bd, inv_n):
    z2 = z * z
    hi = z2.astype(BF16)
    lo = (z2 - hi.astype(F32)).astype(BF16)
    ss = _dot(hi, bd) + _dot(lo, bd)
    return lax.rsqrt(ss * inv_n + EPS)


def _rope(z, cosw, sinw, first_half, half):
    w = z.shape[-1]
    up = pltpu.roll(z, w - half, axis=1)
    dn = pltpu.roll(z, half, axis=1)
    return z * cosw + jnp.where(first_half, up, dn) * sinw


def _ada_kernel(c_ref, w_ref, b_ref, o_ref):
    s = _silu(c_ref[...]).astype(BF16)
    o_ref[0] = _dot(s, w_ref[0].astype(BF16)) + b_ref[0]


def _ada_all(cvec, ada_w, ada_b):
    rows = cvec.shape[0]
    tn = 512
    return pl.pallas_call(
        _ada_kernel,
        grid=(DEPTH, 6 * D // tn),
        in_specs=[pl.BlockSpec((rows, D), lambda i, j: (0, 0)),
                  pl.BlockSpec((1, D, tn), lambda i, j: (i, 0, j)),
                  pl.BlockSpec((1, 1, tn), lambda i, j: (i, 0, j))],
        out_specs=pl.BlockSpec((1, rows, tn), lambda i, j: (i, 0, j)),
        out_shape=jax.ShapeDtypeStruct((DEPTH, rows, 6 * D), F32),
        compiler_params=_cparams(2), name="adaln",
    )(cvec, ada_w, ada_b.reshape(DEPTH, 1, 6 * D))


def _samples_per_step(bsz):
    return 2 if bsz % 2 == 0 else 1


def _mod_spec(n=1):
    return pl.BlockSpec((n, 1, 6, D), lambda b, t: (b, jnp.minimum(t, 1), 0, 0))


def _row_spec(width, n=1):
    return pl.BlockSpec((n, TM, width), lambda b, t: (b, t, 0))


def _rows_of(ref, prep=lambda s, x: x):
    return jnp.concatenate([prep(s, ref[s]) for s in range(ref.shape[0])], axis=0)


def _store_rows(ref, cols, val):
    for s in range(ref.shape[0]):
        ref[s, :, cols] = val[s * TM:(s + 1) * TM]


def _const_spec(shape):
    return pl.BlockSpec(shape, lambda b, t: (0,) * len(shape))


def _tab_spec(width):
    return pl.BlockSpec((TM, width), lambda b, t: (t, 0))


def _qkv_kernel(x_ref, mod_ref, g1_ref, w_ref, bd_ref, gq_ref, gk_ref, cos_ref, sin_ref,
                q_ref, k_ref, v_ref, *, nq, nk, rope):
    n = x_ref.shape[0]
    g1 = g1_ref[...]
    h = _rows_of(x_ref, lambda s, x: _prenorm(x, g1, mod_ref[s, 0][0:1], mod_ref[s, 0][1:2])).astype(BF16)
    acc = _dot(h, w_ref[...])
    bd = bd_ref[...]
    lane = lax.broadcasted_iota(jnp.int32, (1, 256), 1)
    first_half = (lane % 64) < 32
    cosw, sinw = jnp.concatenate([cos_ref[...]] * n, axis=0), jnp.concatenate([sin_ref[...]] * n, axis=0)

    def finish(z, gain):
        z = z * _seg_rms(z, bd, 1.0 / 64) * gain
        if rope:
            z = _rope(z, cosw, sinw, first_half, 32)
        return z.astype(BF16)

    for j in range(nq // 256):
        cols = slice(j * 256, (j + 1) * 256)
        _store_rows(q_ref, cols, finish(acc[:, cols], gq_ref[...]))
    for j in range(nk // 256):
        cols = slice(j * 256, (j + 1) * 256)
        _store_rows(k_ref, cols, finish(acc[:, nq + j * 256:nq + (j + 1) * 256], gk_ref[...]))
    _store_rows(v_ref, slice(None), acc[:, nq + nk:].astype(BF16))


def _qkv_project(x, modtab, g1, w, bd, gq, gk, cosw, sinw, nq, nk, rope):
    bsz = x.shape[0]
    n = _samples_per_step(bsz)
    return pl.pallas_call(
        functools.partial(_qkv_kernel, nq=nq, nk=nk, rope=rope),
        grid=(bsz // n, NTILES),
        in_specs=[_row_spec(D, n), _mod_spec(n), _const_spec((1, D)), _const_spec((D, nq + 2 * nk)),
                  _const_spec((256, 256)), _const_spec((1, 256)), _const_spec((1, 256)),
                  _tab_spec(256), _tab_spec(256)],
        out_specs=[_row_spec(nq, n), _row_spec(nk, n), _row_spec(nk, n)],
        out_shape=[jax.ShapeDtypeStruct((bsz, NT, nq), BF16), jax.ShapeDtypeStruct((bsz, NT, nk), BF16),
                   jax.ShapeDtypeStruct((bsz, NT, nk), BF16)],
        compiler_params=_cparams(2), name="qkv_project",
    )(x, modtab, g1, w, bd, gq, gk, cosw, sinw)


def _mla_kernel(x_ref, mod_ref, g1_ref, w1_ref, gql_ref, gkl_ref, gkr_ref, kcos_ref, ksin_ref,
                wuq_ref, wk_ref, wv_ref, bd_ref, qgain_ref, qinv_ref, qcos_ref, qsin_ref, kgain_ref, kmask_ref,
                q_ref, k_ref, v_ref):
    n = x_ref.shape[0]
    g1 = g1_ref[...]
    tile = lambda ref: jnp.concatenate([ref[...]] * n, axis=0)
    h = _rows_of(x_ref, lambda s, x: _prenorm(x, g1, mod_ref[s, 0][0:1], mod_ref[s, 0][1:2])).astype(BF16)
    a = _dot(h, w1_ref[...])
    cq = a[:, :B_Q_RANK]
    cq = (cq * lax.rsqrt(jnp.mean(cq * cq, axis=-1, keepdims=True) + EPS) * gql_ref[...]).astype(BF16)
    ckv = a[:, B_Q_RANK:B_Q_RANK + B_KV_RANK]
    ckv = (ckv * lax.rsqrt(jnp.mean(ckv * ckv, axis=-1, keepdims=True) + EPS) * gkl_ref[...]).astype(BF16)
    kr = a[:, B_Q_RANK + B_KV_RANK:]
    kr = kr * lax.rsqrt(jnp.sum(kr * kr, axis=-1, keepdims=True) * (1.0 / B_ROPE) + EPS) * gkr_ref[...]
    lane128 = lax.broadcasted_iota(jnp.int32, (1, LANES), 1)
    kr = _rope(kr, tile(kcos_ref), tile(ksin_ref), (lane128 % 32) < 16, 16).astype(BF16)

    bd = bd_ref[...]
    lane = lax.broadcasted_iota(jnp.int32, (1, 256), 1)
    first_half = (lane % 32) < 16
    qcos, qsin = tile(qcos_ref), tile(qsin_ref)
    q2 = _dot(cq, wuq_ref[...])
    k2 = _dot(jnp.concatenate([ckv, kr], axis=1), wk_ref[...])
    kmask = kmask_ref[...] > 0.0
    for j in range(q2.shape[1] // 256):
        sl = slice(j * 256, (j + 1) * 256)
        z = q2[:, sl]
        z = z * _seg_rms(z, bd, qinv_ref[...]) * qgain_ref[...]
        _store_rows(q_ref, sl, _rope(z, qcos, qsin, first_half, 16).astype(BF16))
        z = k2[:, sl]
        _store_rows(k_ref, sl, (z * jnp.where(kmask, _seg_rms(z, bd, 1.0 / 64) * kgain_ref[...], 1.0)).astype(BF16))
    _store_rows(v_ref, slice(None), _dot(ckv, wv_ref[...]).astype(BF16))


def _mla_project(x, modtab, g1, p):
    bsz = x.shape[0]
    n = _samples_per_step(bsz)
    hq = B_HEADS * LANES
    return pl.pallas_call(
        _mla_kernel,
        grid=(bsz // n, NTILES),
        in_specs=[_row_spec(D, n), _mod_spec(n), _const_spec((1, D)), _const_spec((D, 768)),
                  _const_spec((1, B_Q_RANK)), _const_spec((1, B_KV_RANK)), _const_spec((1, LANES)),
                  _tab_spec(LANES), _tab_spec(LANES),
                  _const_spec((B_Q_RANK, hq)), _const_spec((B_KV_RANK + LANES, hq)),
                  _const_spec((B_KV_RANK, B_HEADS * B_V)), _const_spec((256, 256)),
                  _const_spec((1, 256)), _const_spec((1, 256)), _tab_spec(256), _tab_spec(256),
                  _const_spec((1, 256)), _const_spec((1, 256))],
        out_specs=[_row_spec(hq, n), _row_spec(hq, n), _row_spec(B_HEADS * B_V, n)],
        out_shape=[jax.ShapeDtypeStruct((bsz, NT, hq), BF16), jax.ShapeDtypeStruct((bsz, NT, hq), BF16),
                   jax.ShapeDtypeStruct((bsz, NT, B_HEADS * B_V), BF16)],
        compiler_params=_cparams(2), name="mla_project",
    )(x, modtab, g1, p["w1"], p["gql"], p["gkl"], p["gkr"], p["kcos"], p["ksin"], p["wuq"], p["wk"], p["wv"],
      p["bd"], p["qgain"], p["qinv"], p["qcos"], p["qsin"], p["kgain"], p["kmask"])


def _softmax_pv(sa, sb, v):
    outs = []
    for s in (sa, sb):
        p = jnp.exp2(s - jnp.max(s, axis=-1, keepdims=True))
        l = jnp.sum(p, axis=-1, keepdims=True)
        outs.append(_dot(p.astype(BF16), v) / l)
    lane = lax.broadcasted_iota(jnp.int32, (1, LANES), 1)
    return jnp.where(lane < 64, outs[0], outs[1])


def _split_heads(q):
    lane = lax.broadcasted_iota(jnp.int32, (1, LANES), 1)
    zero = jnp.zeros_like(q)
    return jnp.where(lane < 64, q, zero), jnp.where(lane >= 64, q, zero)


PAIRS_PER_STEP = 4


DENSE_PAIRS_PER_STEP = 8


def _pair_attn_kernel(q_ref, k_ref, v_ref, o_ref, *, wide, q_pairs_per_kv):
    t = pl.program_id(2)

    def run(nkeys):
        for i in range(DENSE_PAIRS_PER_STEP):
            kv = i // q_pairs_per_kv
            v = v_ref[0, :nkeys, LANES * kv:LANES * (kv + 1)]
            if wide:
                q = q_ref[0, :, 256 * i:256 * (i + 1)]
                k = k_ref[0, :nkeys, 256 * kv:256 * (kv + 1)]
                sa, sb = _dot_t(q[:, :LANES], k[:, :LANES]), _dot_t(q[:, LANES:], k[:, LANES:])
            else:
                qa, qb = _split_heads(q_ref[0, :, LANES * i:LANES * (i + 1)])
                k = k_ref[0, :nkeys, LANES * kv:LANES * (kv + 1)]
                sa, sb = _dot_t(qa, k), _dot_t(qb, k)
            o_ref[0, :, LANES * i:LANES * (i + 1)] = _softmax_pv(sa, sb, v).astype(BF16)

    @pl.when(t == 0)
    def _():
        run(CTX)

    @pl.when(t > 0)
    def _():
        run(NT)


def _pair_attention(q, k, v, *, wide, q_pairs_per_kv):
    bsz = q.shape[0]
    qw = 256 if wide else LANES
    n_steps = q.shape[2] // (qw * DENSE_PAIRS_PER_STEP)
    kv_pairs = DENSE_PAIRS_PER_STEP // q_pairs_per_kv
    return pl.pallas_call(
        functools.partial(_pair_attn_kernel, wide=wide, q_pairs_per_kv=q_pairs_per_kv),
        grid=(bsz, n_steps, NTILES),
        in_specs=[pl.BlockSpec((1, TM, qw * DENSE_PAIRS_PER_STEP), lambda b, p, t: (b, t, p)),
                  pl.BlockSpec((1, NT, qw * kv_pairs), lambda b, p, t: (b, 0, p)),
                  pl.BlockSpec((1, NT, LANES * kv_pairs), lambda b, p, t: (b, 0, p))],
        out_specs=pl.BlockSpec((1, TM, LANES * DENSE_PAIRS_PER_STEP), lambda b, p, t: (b, t, p)),
        out_shape=jax.ShapeDtypeStruct((bsz, NT, n_steps * DENSE_PAIRS_PER_STEP * LANES), BF16),
        compiler_params=_cparams(3), name="pair_attention",
    )(q, k, v)


def _na_attn_kernel(q_ref, kc_ref, k0_ref, k1_ref, k2_ref, vc_ref, v0_ref, v1_ref, v2_ref, bias_ref, o_ref):
    t = pl.program_id(1)

    @pl.when(t == 0)
    def _():
        for i in range(PAIRS_PER_STEP):
            sl = slice(LANES * i, LANES * (i + 1))
            qa, qb = _split_heads(q_ref[0, :, sl])
            k = kc_ref[0, :, sl]
            o_ref[0, :, sl] = _softmax_pv(_dot_t(qa, k), _dot_t(qb, k), vc_ref[0, :, sl]).astype(BF16)

    @pl.when(t > 0)
    def _():
        for i in range(PAIRS_PER_STEP):
            sl = slice(LANES * i, LANES * (i + 1))
            qa, qb = _split_heads(q_ref[0, :, sl])
            k = jnp.concatenate([k0_ref[0, :, sl], k1_ref[0, :, sl], k2_ref[0, :, sl], kc_ref[0, :, sl]], axis=0)
            v = jnp.concatenate([v0_ref[0, :, sl], v1_ref[0, :, sl], v2_ref[0, :, sl], vc_ref[0, :, sl]], axis=0)
            sa = _dot_t(qa, k) + bias_ref[0, 2 * i]
            sb = _dot_t(qb, k) + bias_ref[0, 2 * i + 1]
            o_ref[0, :, sl] = _softmax_pv(sa, sb, v).astype(BF16)


def _na_band_tile(t):
    return 1 + jnp.clip(t - 2, 0, LAT_TILES - BAND_TILES)


def _na_attention(q, k, v, bias):
    bsz = q.shape[0]
    n_steps = C_HEADS // (2 * PAIRS_PER_STEP)
    width = LANES * PAIRS_PER_STEP
    qspec = pl.BlockSpec((1, TM, width), lambda p, t, b: (b, t, p))
    cspec = pl.BlockSpec((1, TM, width), lambda p, t, b: (b, 0, p))
    bands = [pl.BlockSpec((1, TM, width), functools.partial(lambda p, t, b, i: (b, _na_band_tile(t) + i, p), i=i))
             for i in range(BAND_TILES)]
    variant = lambda t: jnp.where(t <= 1, 0, jnp.where(t == LAT_TILES, 2, 1))
    return pl.pallas_call(
        _na_attn_kernel,
        grid=(n_steps, NTILES, bsz),
        in_specs=[qspec, cspec] + bands + [cspec] + bands
                 + [pl.BlockSpec((1, 2 * PAIRS_PER_STEP, TM, (BAND_TILES + 1) * TM),
                                 lambda p, t, b: (variant(t), p, 0, 0))],
        out_specs=qspec,
        out_shape=jax.ShapeDtypeStruct((bsz, NT, C_HEADS * C_HEAD_DIM), BF16),
        compiler_params=_cparams(3), name="na_attention",
    )(q, k, k, k, k, v, v, v, v, bias)


def _out_router_kernel(o_ref, wo_ref, x_ref, mod_ref, g2_ref, r_ref, xo_ref, h_ref, aff_ref):
    attn = _dot(_rows_of(o_ref), wo_ref[...])
    hs = []
    for s in range(x_ref.shape[0]):
        mod = mod_ref[s, 0]
        x = x_ref[s] + mod[2:3] * attn[s * TM:(s + 1) * TM]
        xo_ref[s] = x
        h = _prenorm(x, g2_ref[...], mod[3:4], mod[4:5]).astype(BF16)
        for r, word in enumerate(_pack_rows(h)):
            h_ref[s, pl.ds(r, TM, stride=HS_ROWS), :] = word
        hs.append(h)
    logits = _dot(jnp.concatenate(hs, axis=0), r_ref[...])
    lane = lax.broadcasted_iota(jnp.int32, (1, LANES), 1)
    logits = jnp.where(lane < N_EXPERTS, logits, NEG_INF)
    e = jnp.exp(logits - jnp.max(logits, axis=-1, keepdims=True))
    _store_rows(aff_ref, slice(None), e / jnp.sum(e, axis=-1, keepdims=True))


def _out_router(o, wo, x, modtab, g2, router):
    bsz = x.shape[0]
    n = _samples_per_step(bsz)
    return pl.pallas_call(
        _out_router_kernel,
        grid=(bsz // n, NTILES),
        in_specs=[_row_spec(D, n), _const_spec((D, D)), _row_spec(D, n), _mod_spec(n), _const_spec((1, D)),
                  _const_spec((D, LANES))],
        out_specs=[_row_spec(D, n), pl.BlockSpec((n, TM * HS_ROWS, LANES), lambda b, t: (b, t, 0)),
                   _row_spec(LANES, n)],
        out_shape=[jax.ShapeDtypeStruct((bsz, NT, D), F32),
                   jax.ShapeDtypeStruct((bsz, NT * HS_ROWS, LANES), jnp.int32),
                   jax.ShapeDtypeStruct((bsz, NT, LANES), F32)],
        input_output_aliases={2: 0},
        compiler_params=_cparams(2), name="out_router",
    )(o, wo, x, modtab, g2, router)


def _excl_cumsum_lanes(m, triu):
    out = []
    offset = jnp.zeros((m.shape[0], 1), F32)
    for i in range(m.shape[1] // TM):
        blk = m[:, i * TM:(i + 1) * TM]
        out.append(_dot(blk.astype(BF16), triu) + offset)
        offset = offset + jnp.sum(blk, axis=1, keepdims=True)
    return jnp.concatenate(out, axis=1) if len(out) > 1 else out[0]


def _kth_largest_bits(bits_list, caps):
    def count(bits, cand):
        return jnp.sum(jnp.where(bits >= cand, 1.0, 0.0), axis=1, keepdims=True)

    def step(i, thrs):
        b1 = jnp.left_shift(jnp.int32(1), 29 - 2 * i)
        b0 = jnp.left_shift(jnp.int32(1), 28 - 2 * i)
        out = []
        for bits, cap, thr in zip(bits_list, caps, thrs):
            c1, c2, c3 = count(bits, thr | b1), count(bits, thr | b0), count(bits, thr | b1 | b0)
            with_b1 = jnp.where(c3 >= cap, thr | b1 | b0, thr | b1)
            without = jnp.where(c2 >= cap, thr | b0, thr)
            out.append(jnp.where(c1 >= cap, with_b1, without))
        return tuple(out)

    top = jnp.int32(1 << 30)
    init = tuple(jnp.where(count(bits, top) >= cap, top, jnp.zeros((N_EXPERTS, 1), jnp.int32))
                 for bits, cap in zip(bits_list, caps))
    return lax.fori_loop(0, 15, step, init)


def _topk_slots(bits, thr, cap, triu):
    gt = bits > thr
    eq = bits == thr
    need = cap - jnp.sum(jnp.where(gt, 1.0, 0.0), axis=1, keepdims=True)
    eq_rank = _excl_cumsum_lanes(jnp.where(eq, 1.0, 0.0), triu)
    sel = gt | (eq & (eq_rank < need))
    rank = _excl_cumsum_lanes(jnp.where(sel, 1.0, 0.0), triu)
    return jnp.where(sel, rank, -1.0), rank


OFF_LANE = 3 * N_EXPERTS


def _token_values(aff, first_row):
    n = aff.shape[0]
    hi = aff.astype(BF16).astype(F32)
    r1 = aff - hi
    mid = r1.astype(BF16).astype(F32)
    lo = (r1 - mid).astype(BF16).astype(F32)
    off = (lax.broadcasted_iota(jnp.int32, (n, 1), 0) + first_row) * HS_ROWS
    lane = lax.broadcasted_iota(jnp.int32, (1, LANES), 1)
    vals = (hi + pltpu.roll(mid, N_EXPERTS, axis=1) + pltpu.roll(lo, 2 * N_EXPERTS, axis=1)
            + jnp.where(lane == OFF_LANE, (off >> 7).astype(F32), 0.0)
            + jnp.where(lane == OFF_LANE + 1, (off & (LANES - 1)).astype(F32), 0.0))
    return vals.astype(BF16)


def _topk_kernel(aff_t_ref, aff_ref, slot_ref, off_ref, gate_ref, start_ref):
    r = lax.broadcasted_iota(jnp.int32, (TM, TM), 0)
    c = lax.broadcasted_iota(jnp.int32, (TM, TM), 1)
    triu = jnp.where(r < c, 1.0, 0.0).astype(BF16)
    lane = lax.broadcasted_iota(jnp.int32, (1, LANES), 1)
    segments = ((0, CTX, CAP_CTX, CAP_LAT), (CTX, NT, CAP_LAT, 0))
    bits = [pltpu.bitcast(aff_t_ref[0, :, lo:hi], jnp.int32) for lo, hi, _, _ in segments]
    thrs = _kth_largest_bits(bits, [cap for _, _, cap, _ in segments])
    for (lo, hi, cap, dst), seg_bits, thr in zip(segments, bits, thrs):
        slot, rank = _topk_slots(seg_bits, thr, cap, triu)
        slot_ref[0, :, lo:hi] = slot.astype(jnp.int32)
        if lo == CTX:
            starts = jnp.where(lane == LAT_TILES, float(cap), 0.0)
            for j in range(LAT_TILES):
                starts = starts + jnp.where(lane == j, rank[:, TM * j:TM * j + 1], 0.0)
            start_ref[0] = starts.astype(jnp.int32)
        vals = _token_values(aff_ref[0, lo:hi], lo)
        slot_ids = lax.broadcasted_iota(jnp.int32, (cap, 1), 0).astype(F32)
        for e in range(N_EXPERTS):
            hit = jnp.where(slot_ids == slot[e:e + 1, :], 1.0, 0.0).astype(BF16)
            picked = _dot(hit, vals)
            mine = ((lane & (N_EXPERTS - 1)) == e) & (lane < OFF_LANE)
            gate = jnp.sum(jnp.where(mine, picked, 0.0), axis=1, keepdims=True)
            off = picked[:, OFF_LANE:OFF_LANE + 1] * float(LANES) + picked[:, OFF_LANE + 1:OFF_LANE + 2]
            gate_ref[0, e, dst:dst + cap, :] = gate
            off_ref[0, e, dst:dst + cap, :] = off.astype(jnp.int32)


def _topk(aff_t, aff):
    bsz = aff.shape[0]
    tspec = pl.BlockSpec((1, N_EXPERTS, NT), lambda b: (b, 0, 0))
    lspec = pl.BlockSpec((1, N_EXPERTS, CAP_ALL, 1), lambda b: (b, 0, 0, 0))
    return pl.pallas_call(
        _topk_kernel, grid=(bsz,),
        in_specs=[tspec, pl.BlockSpec((1, NT, LANES), lambda b: (b, 0, 0))],
        out_specs=[tspec, lspec, lspec, pl.BlockSpec((1, N_EXPERTS, LANES), lambda b: (b, 0, 0))],
        out_shape=[jax.ShapeDtypeStruct((bsz, N_EXPERTS, NT), jnp.int32),
                   jax.ShapeDtypeStruct((bsz, N_EXPERTS, CAP_ALL, 1), jnp.int32),
                   jax.ShapeDtypeStruct((bsz, N_EXPERTS, CAP_ALL, 1), F32),
                   jax.ShapeDtypeStruct((bsz, N_EXPERTS, LANES), jnp.int32)],
        compiler_params=_cparams(1), name="expert_topk",
    )(aff_t, aff)


def _pack_rows(hb):
    words = []
    for r in range(HS_ROWS):
        lo = pltpu.bitcast(hb[:, 256 * r:256 * r + LANES].astype(F32), jnp.int32)
        hi = pltpu.bitcast(hb[:, 256 * r + LANES:256 * (r + 1)].astype(F32), jnp.int32)
        words.append(lax.shift_right_logical(lo, jnp.int32(16)) | hi)
    return words


def _unpack_rows(words):
    cols = []
    for w in words:
        cols.append(pltpu.bitcast(lax.shift_left(w, jnp.int32(16)), F32).astype(BF16))
        cols.append(pltpu.bitcast(w & jnp.int32(-65536), F32).astype(BF16))
    return jnp.concatenate(cols, axis=1)


def _expert_kernel(hs_ref, off_ref, gate_ref, wg_ref, wu_ref, wd_ref, *refs, n_slots):
    out_refs, (tile_ref, wg_s, wu_s, wd_s) = refs[:-4], refs[-4:]

    @pl.when(pl.program_id(1) == 0)
    def _():
        wg_s[...] = wg_ref[0, 0].astype(BF16)
        wu_s[...] = wu_ref[0, 0].astype(BF16)
        wd_s[...] = wd_ref[0, 0].astype(BF16)

    for j in range(n_slots):
        src = pl.multiple_of(off_ref[0, 0, 0, j], HS_ROWS)
        tile_ref[pl.ds(j, HS_ROWS, stride=TILE_STRIDE), :] = hs_ref[0, pl.ds(src, HS_ROWS), :]
    xg = _unpack_rows([tile_ref[r * TILE_STRIDE:r * TILE_STRIDE + n_slots, :] for r in range(HS_ROWS)])
    hid = (_silu(_dot(xg, wg_s[...])) * _dot(xg, wu_s[...])).astype(BF16)
    y = _dot(hid, wd_s[...]) * gate_ref[0, 0, :n_slots]
    out_refs[0][0] = y[:CAP_LAT].astype(BF16)
    if n_slots > CAP_LAT:
        out_refs[1][0] = y[CAP_LAT:].astype(BF16)


def _experts(hs, off, gate, wg, wu, wd, layer, with_ctx):
    bsz = hs.shape[0]
    n_slots = CAP_ALL if with_ctx else CAP_LAT
    wspec = pl.BlockSpec((1, 1, D, EXPERT_FF), lambda e, b: (layer, e, 0, 0))
    out_specs = [pl.BlockSpec((1, CAP_LAT, D), lambda e, b: (b, e, 0))]
    out_shape = [jax.ShapeDtypeStruct((bsz, N_EXPERTS * CAP_LAT, D), BF16)]
    if with_ctx:
        out_specs.append(pl.BlockSpec((1, CAP_CTX, D), lambda e, b: (b, e, 0)))
        out_shape.append(jax.ShapeDtypeStruct((bsz, N_EXPERTS * CAP_CTX, D), BF16))
    return pl.pallas_call(
        functools.partial(_expert_kernel, n_slots=n_slots),
        grid=(N_EXPERTS, bsz),
        in_specs=[pl.BlockSpec((1, NT * HS_ROWS, LANES), lambda e, b: (b, 0, 0)),
                  pl.BlockSpec((1, 1, 1, CAP_ALL), lambda e, b: (b, e, 0, 0), memory_space=pltpu.SMEM),
                  pl.BlockSpec((1, 1, CAP_ALL, 1), lambda e, b: (b, e, 0, 0)),
                  wspec, wspec, pl.BlockSpec((1, 1, EXPERT_FF, D), lambda e, b: (layer, e, 0, 0))],
        out_specs=out_specs, out_shape=out_shape,
        scratch_shapes=[pltpu.VMEM((HS_ROWS * TILE_STRIDE, LANES), jnp.int32),
                        pltpu.VMEM((D, EXPERT_FF), BF16), pltpu.VMEM((D, EXPERT_FF), BF16),
                        pltpu.VMEM((EXPERT_FF, D), BF16)],
        compiler_params=_cparams(2), name="experts",
    )(hs, off, gate, wg, wu, wd)


WIN = 64
WIN_GROUP = 256 // WIN


def _combine_lat(slot_ref, yl_ref, win_ref, x_ref, gate, xo_ref):
    slot = slot_ref[0]
    fits = win_ref[0, 0, 0, N_EXPERTS] - win_ref[0, 0, 0, 0] <= WIN
    for e in range(1, N_EXPERTS):
        fits = jnp.logical_and(fits, win_ref[0, 0, 0, N_EXPERTS + e] - win_ref[0, 0, 0, e] <= WIN)

    @pl.when(fits)
    def _():
        cols = lax.broadcasted_iota(jnp.int32, (TM, WIN * WIN_GROUP), 1)
        acc = None
        for g in range(N_EXPERTS // WIN_GROUP):
            hit = cols < 0
            rows = []
            for j in range(WIN_GROUP):
                e = WIN_GROUP * g + j
                start = win_ref[0, 0, 0, e]
                s = slot[:, e:e + 1]
                hit = hit | (jnp.where(s >= 0, s - start + WIN * j, -1) == cols)
                rows.append(yl_ref[0, pl.ds(pl.multiple_of(e * CAP_LAT + start, 16), WIN), :])
            part = _dot(jnp.where(hit, 1.0, 0.0).astype(BF16), jnp.concatenate(rows, axis=0))
            acc = part if acc is None else acc + part
        xo_ref[0] = x_ref[0] + gate * acc

    @pl.when(jnp.logical_not(fits))
    def _():
        cols = lax.broadcasted_iota(jnp.int32, (TM, CAP_LAT), 1)
        onehot = jnp.concatenate(
            [jnp.where(slot[:, e:e + 1] == cols, 1.0, 0.0).astype(BF16) for e in range(N_EXPERTS)], axis=1)
        xo_ref[0] = x_ref[0] + gate * _dot(onehot, yl_ref[0])


def _combine_kernel(slot_ref, yl_ref, yc_ref, win_ref, x_ref, mod_ref, xo_ref):
    t = pl.program_id(1)
    gate = mod_ref[0, 0][5:6]

    @pl.when(t == 0)
    def _():
        slot = slot_ref[0]
        cols = lax.broadcasted_iota(jnp.int32, (TM, N_EXPERTS * CAP_CTX), 1)
        hit = cols < 0
        for e in range(N_EXPERTS):
            s = slot[:, e:e + 1]
            hit = hit | (jnp.where(s >= 0, s + e * CAP_CTX, -1) == cols)
        xo_ref[0] = x_ref[0] + gate * _dot(jnp.where(hit, 1.0, 0.0).astype(BF16), yc_ref[0])

    @pl.when(t > 0)
    def _():
        _combine_lat(slot_ref, yl_ref, win_ref, x_ref, gate, xo_ref)


def _win_spec(tile_of_step):
    return pl.BlockSpec((1, 1, 1, 2 * N_EXPERTS), lambda b, t: (b, tile_of_step(t), 0, 0), memory_space=pltpu.SMEM)


def _combine(slot, yl, yc, win, x, modtab):
    bsz = x.shape[0]
    return pl.pallas_call(
        _combine_kernel,
        grid=(bsz, NTILES),
        in_specs=[_row_spec(LANES), pl.BlockSpec((1, N_EXPERTS * CAP_LAT, D), lambda b, t: (b, 0, 0)),
                  pl.BlockSpec((1, N_EXPERTS * CAP_CTX, D), lambda b, t: (b, 0, 0)),
                  _win_spec(lambda t: jnp.maximum(t - 1, 0)), _row_spec(D), _mod_spec()],
        out_specs=_row_spec(D),
        out_shape=jax.ShapeDtypeStruct((bsz, NT, D), F32),
        input_output_aliases={4: 0},
        compiler_params=_cparams(2), name="moe_combine",
    )(slot, yl, yc, win, x, modtab)


def _combine_last_kernel(slot_ref, yl_ref, win_ref, x_ref, mod_ref, xo_ref):
    _combine_lat(slot_ref, yl_ref, win_ref, x_ref, mod_ref[0, 0][5:6], xo_ref)


def _combine_last(slot, yl, win, x, modtab):
    bsz = x.shape[0]
    lat = lambda width: pl.BlockSpec((1, TM, width), lambda b, t: (b, t + 1, 0))
    return pl.pallas_call(
        _combine_last_kernel,
        grid=(bsz, LAT_TILES),
        in_specs=[lat(LANES), pl.BlockSpec((1, N_EXPERTS * CAP_LAT, D), lambda b, t: (b, 0, 0)),
                  _win_spec(lambda t: t), lat(D), pl.BlockSpec((1, 1, 6, D), lambda b, t: (b, 1, 0, 0))],
        out_specs=pl.BlockSpec((1, TM, D), lambda b, t: (b, t, 0)),
        out_shape=jax.ShapeDtypeStruct((bsz, SEQ, D), F32),
        compiler_params=_cparams(2), name="moe_combine_last",
    )(slot, yl, win, x, modtab)


def _rope_tables(rot_dim, width, lane_lo, lane_hi):
    n_freq = rot_dim // 4
    half = rot_dim // 2
    inv = jnp.float32(ROPE_THETA) ** (-jnp.arange(n_freq, dtype=F32) / n_freq)
    t = jnp.arange(SEQ, dtype=jnp.int32)
    row = (t // GRID_W).astype(F32)
    col = (t % GRID_W).astype(F32)
    ang = jnp.concatenate([row[:, None] * inv, col[:, None] * inv], axis=-1)
    cos, sin = jnp.cos(ang), jnp.sin(ang)
    lane = np.arange(width)
    inside = (lane % LANES >= lane_lo) & (lane % LANES < lane_hi)
    idx = lane % half
    sign = np.where(lane % rot_dim < half, -1.0, 1.0).astype(np.float32)
    cosw = jnp.where(inside[None, :], cos[:, idx], 1.0)
    sinw = jnp.where(inside[None, :], sin[:, idx] * sign[None, :], 0.0)
    ones = jnp.ones((CTX, width), F32)
    return jnp.concatenate([ones, cosw], axis=0), jnp.concatenate([0.0 * ones, sinw], axis=0)


def _block_diag_ones():
    i = np.arange(256)
    return jnp.asarray((i[:, None] // 64) == (i[None, :] // 64), dtype=BF16)


_GQA_ORDER = np.array([8 * kp + 4 * odd + i for kp in range(2) for i in range(4) for odd in range(2)])


def _na_bias(rpb):
    out = []
    rows = SEQ // GRID_W
    qrows, krows = TM // GRID_W, BAND_TILES * TM // GRID_W
    col = np.arange(GRID_W)
    cs = np.clip(col - NA_COLS // 2, 0, GRID_W - NA_COLS)
    ok_c = (col[None, :] >= cs[:, None]) & (col[None, :] < cs[:, None] + NA_COLS)
    dc = np.clip(col[None, :] - col[:, None] + NA_COLS - 1, 0, 2 * NA_COLS - 2)
    pick_c = (dc[:, :, None] == np.arange(2 * NA_COLS - 1)).astype(np.float32)
    for r0, bs in ((0, 0), (8, 4), (rows - 4, rows - 12)):
        r, kr = r0 + np.arange(qrows), bs + np.arange(krows)
        rs = np.clip(r - NA_ROWS // 2, 0, rows - NA_ROWS)
        ok_r = (kr[None, :] >= rs[:, None]) & (kr[None, :] < rs[:, None] + NA_ROWS)
        dr = kr[None, :] - r[:, None] + NA_ROWS - 1
        pick_r = (dr[:, :, None] == np.arange(2 * NA_ROWS - 1)).astype(np.float32)
        t1 = jnp.einsum('ard,hde->hare', pick_r, rpb.astype(F32), precision=lax.Precision.HIGHEST)
        band = jnp.einsum('hare,cse->hacrs', t1, pick_c, precision=lax.Precision.HIGHEST)
        ok = ok_r[:, None, :, None] & ok_c[None, :, None, :]
        band = jnp.where(ok[None], band * LOG2E, NEG_INF).reshape(C_HEADS, TM, BAND_TILES * TM)
        out.append(jnp.concatenate([band, jnp.zeros((C_HEADS, TM, TM), F32)], axis=-1))
    return jnp.stack(out)


def _gqa_params(w_qkv, qn, kn, wo):
    nq, nk = A_HEADS * A_HEAD_DIM, A_KV_HEADS * A_HEAD_DIM
    wq = w_qkv[:, :nq].reshape(D, A_HEADS, A_HEAD_DIM)[:, _GQA_ORDER].reshape(D, nq)
    w = jnp.concatenate([wq, w_qkv[:, nq:]], axis=1).astype(BF16)
    wo_p = wo.reshape(A_HEADS, A_HEAD_DIM, D)[_GQA_ORDER].reshape(nq, D).astype(BF16)
    scale = A_HEAD_DIM ** -0.5 * LOG2E
    return w, (jnp.tile(qn, 4) * scale)[None], jnp.tile(kn, 4)[None], wo_p


def _mla_params(w_dq, qn_lat, w_uq, w_dkv, kvn_lat, w_ukv, qn, kn, cossin_q, cossin_k):
    qd = B_NOPE + B_ROPE
    w1 = jnp.concatenate([w_dq, w_dkv, jnp.zeros((D, 768 - B_Q_RANK - B_KV_RANK - B_ROPE), F32)], axis=1)
    pad_q = jnp.zeros((B_Q_RANK, B_HEADS, LANES - qd), F32)
    wuq = jnp.concatenate([w_uq.reshape(B_Q_RANK, B_HEADS, qd), pad_q], axis=2).reshape(B_Q_RANK, B_HEADS * LANES)
    ukv = w_ukv.reshape(B_KV_RANK, B_HEADS, B_NOPE + B_V)
    wk_top = jnp.concatenate([ukv[:, :, :B_NOPE], jnp.zeros((B_KV_RANK, B_HEADS, LANES - B_NOPE), F32)], axis=2)
    route = np.zeros((LANES, B_HEADS, LANES), np.float32)
    route[np.arange(B_ROPE), :, B_NOPE + np.arange(B_ROPE)] = 1.0
    wk = jnp.concatenate([wk_top.reshape(B_KV_RANK, -1), jnp.asarray(route).reshape(LANES, -1)], axis=0)
    wv = ukv[:, :, B_NOPE:].reshape(B_KV_RANK, B_HEADS * B_V)
    scale = qd ** -0.5 * LOG2E
    zpad = jnp.zeros((LANES - qd,), F32)
    qgain = jnp.tile(jnp.concatenate([qn * scale, zpad]), 2)[None]
    qinv = jnp.tile(jnp.concatenate([jnp.full((B_NOPE,), 1.0 / B_NOPE), jnp.full((LANES - B_NOPE,), 1.0 / B_ROPE)]), 2)
    kgain = jnp.tile(jnp.concatenate([kn[:B_NOPE], jnp.zeros((LANES - B_NOPE,), F32)]), 2)[None]
    kmask = jnp.tile(jnp.concatenate([jnp.ones((B_NOPE,), F32), jnp.zeros((LANES - B_NOPE,), F32)]), 2)[None]
    gkr = jnp.concatenate([kn[B_NOPE:], jnp.zeros((LANES - B_ROPE,), F32)])[None]
    return dict(w1=w1.astype(BF16), gql=qn_lat[None], gkl=kvn_lat[None], gkr=gkr, kcos=cossin_k[0], ksin=cossin_k[1],
                wuq=wuq.astype(BF16), wk=wk.astype(BF16), wv=wv.astype(BF16), bd=_block_diag_ones(),
                qgain=qgain, qinv=qinv[None].astype(F32), qcos=cossin_q[0], qsin=cossin_q[1], kgain=kgain, kmask=kmask)


def kernel(x, c, ctx, c_ctx, ada_w, ada_b, norm1_w, norm2_w, a_wqkv, a_qnorm, a_knorm, a_wo, b_wdq, b_qnorm_lat, b_wuq, b_wdkv, b_kvnorm_lat, b_wukv, b_qnorm, b_knorm, b_wo, c_wqkv, c_qnorm, c_knorm, c_rpb, c_wo, moe_router, moe_wg, moe_wu, moe_wd):
    bsz = x.shape[0]
    assert x.shape[1:] == (SEQ, D) and ctx.shape[1:] == (CTX, D)
    mod_rows = -(-(bsz + 1) // 16) * 16
    cvec = jnp.concatenate([c, c_ctx[None], jnp.zeros((mod_rows - bsz - 1, D), F32)], axis=0)
    mods = _ada_all(cvec, ada_w, ada_b)
    xs = jnp.concatenate([ctx, x], axis=1)

    bd = _block_diag_ones()
    rope64 = _rope_tables(A_HEAD_DIM, 256, 0, LANES)
    rope_mla_q = _rope_tables(B_ROPE, 256, B_NOPE, B_NOPE + B_ROPE)
    rope_mla_k = _rope_tables(B_ROPE, LANES, 0, B_ROPE)
    router = jnp.pad(moe_router, ((0, 0), (0, 0), (0, LANES - N_EXPERTS))).astype(BF16)

    for i in range(DEPTH):
        last = i == DEPTH - 1
        kind, j = i % 3, i // 3
        m = mods[i]
        m_lat = m[:bsz].reshape(bsz, 6, D)
        m_ctx = jnp.broadcast_to(m[bsz].reshape(1, 6, D), (bsz, 6, D))
        modtab = jnp.stack([m_ctx, m_lat], axis=1)
        g1, g2 = norm1_w[i][None], norm2_w[i][None]

        if kind == 0:
            w, gq, gk, wo = _gqa_params(a_wqkv[j], a_qnorm[j], a_knorm[j], a_wo[j])
            q, k, v = _qkv_project(xs, modtab, g1, w, bd, gq, gk, rope64[0], rope64[1],
                                   A_HEADS * A_HEAD_DIM, A_KV_HEADS * A_HEAD_DIM, True)
            o = _pair_attention(q, k, v, wide=False, q_pairs_per_kv=4)
        elif kind == 1:
            p = _mla_params(b_wdq[j], b_qnorm_lat[j], b_wuq[j], b_wdkv[j], b_kvnorm_lat[j], b_wukv[j],
                            b_qnorm[j], b_knorm[j], rope_mla_q, rope_mla_k)
            q, k, v = _mla_project(xs, modtab, g1, p)
            o = _pair_attention(q, k, v, wide=True, q_pairs_per_kv=1)
            wo = b_wo[j].astype(BF16)
        else:
            hd = C_HEADS * C_HEAD_DIM
            gq = (jnp.tile(c_qnorm[j], 4) * (C_HEAD_DIM ** -0.5 * LOG2E))[None]
            gk = jnp.tile(c_knorm[j], 4)[None]
            q, k, v = _qkv_project(xs, modtab, g1, c_wqkv[j].astype(BF16), bd, gq, gk, rope64[0], rope64[1],
                                   hd, hd, False)
            o = _na_attention(q, k, v, _na_bias(c_rpb[j]))
            wo = c_wo[j].astype(BF16)

        xs, hs, aff = _out_router(o, wo, xs, modtab, g2, router[i])
        slot_t, off, gate, starts = _topk(jnp.swapaxes(aff[:, :, :N_EXPERTS], 1, 2), aff)
        slot = jnp.pad(jnp.swapaxes(slot_t, 1, 2), ((0, 0), (0, 0), (0, LANES - N_EXPERTS)), constant_values=-1)
        run = jnp.swapaxes(starts[:, :, :LAT_TILES + 1], 1, 2)
        win_start = jnp.minimum(run[:, :LAT_TILES] // 16 * 16, CAP_LAT - WIN)
        win = jnp.concatenate([win_start, run[:, 1:]], axis=-1)[:, :, None, :]
        ys = _experts(hs, off.reshape(bsz, N_EXPERTS, 1, CAP_ALL), gate,
                      moe_wg, moe_wu, moe_wd, layer=i, with_ctx=not last)
        if last:
            return _combine_last(slot, ys[0], win, xs, modtab)
        xs = _combine(slot, ys[0], ys[1], win, xs, modtab)
```

```python
import functools

import numpy as np
import jax
import jax.numpy as jnp
from jax import lax
from jax.experimental import pallas as pl
from jax.experimental.pallas import tpu as pltpu

F32 = jnp.float32
BF16 = jnp.bfloat16

D = 1024
SEQ = 2048
CTX = 256
NT = CTX + SEQ
DEPTH = 4
GRID_W = 64
ROPE_THETA = 10000.0
EPS = 1e-6
NEG_INF = -1e30
LOG2E = 1.4426950408889634

TM = 256
NTILES = NT // TM
LAT_TILES = SEQ // TM

A_HEADS, A_KV_HEADS, A_HEAD_DIM = 16, 4, 64
B_HEADS, B_Q_RANK, B_KV_RANK, B_NOPE, B_ROPE, B_V = 16, 384, 256, 64, 32, 64
C_HEADS, C_HEAD_DIM, NA_ROWS, NA_COLS = 16, 64, 8, 16
N_EXPERTS, EXPERT_FF = 16, 1024
CAP_LAT = 2 * SEQ // N_EXPERTS
CAP_CTX = 2 * CTX // N_EXPERTS
CAP_ALL = CAP_LAT + CAP_CTX
HS_ROWS = D // 256
TILE_STRIDE = 296
LANES = 128
BAND_TILES = 3

VMEM_LIMIT = 56 * 1024 * 1024


def _cparams(n_axes):
    return pltpu.CompilerParams(dimension_semantics=("arbitrary",) * n_axes, vmem_limit_bytes=VMEM_LIMIT)


def _dot(a, b):
    return jnp.dot(a, b, preferred_element_type=F32)


def _dot_t(a, b):
    return lax.dot_general(a, b, (((1,), (1,)), ((), ())), preferred_element_type=F32)


def _silu(x):
    return x / (1.0 + jnp.exp(-x))


def _prenorm(x, gain, shift, scale):
    ms = jnp.mean(x * x, axis=-1, keepdims=True)
    return (x * lax.rsqrt(ms + EPS) * gain) * (1.0 + scale) + shift


def _seg_rms(z, bd, inv_n):
    z2 = z * z
    hi = z2.astype(BF16)
    lo = (z2 - hi.astype(F32)).astype(BF16)
    ss = _dot(hi, bd) + _dot(lo, bd)
    return lax.rsqrt(ss * inv_n + EPS)


def _rope(z, cosw, sinw, first_half, half):
    w = z.shape[-1]
    up = pltpu.roll(z, w - half, axis=1)
    dn = pltpu.roll(z, half, axis=1)
    return z * cosw + jnp.where(first_half, up, dn) * sinw


def _ada_kernel(c_ref, w_ref, b_ref, o_ref):
    s = _silu(c_ref[...]).astype(BF16)
    o_ref[0] = _dot(s, w_ref[0].astype(BF16)) + b_ref[0]


def _ada_all(cvec, ada_w, ada_b):
    rows = cvec.shape[0]
    tn = 512
    return pl.pallas_call(
        _ada_kernel,
        grid=(DEPTH, 6 * D // tn),
        in_specs=[pl.BlockSpec((rows, D), lambda i, j: (0, 0)),
                  pl.BlockSpec((1, D, tn), lambda i, j: (i, 0, j)),
                  pl.BlockSpec((1, 1, tn), lambda i, j: (i, 0, j))],
        out_specs=pl.BlockSpec((1, rows, tn), lambda i, j: (i, 0, j)),
        out_shape=jax.ShapeDtypeStruct((DEPTH, rows, 6 * D), F32),
        compiler_params=_cparams(2), name="adaln",
    )(cvec, ada_w, ada_b.reshape(DEPTH, 1, 6 * D))


def _samples_per_step(bsz):
    return 2 if bsz % 2 == 0 else 1


def _mod_spec(n=1):
    return pl.BlockSpec((n, 1, 6, D), lambda b, t: (b, jnp.minimum(t, 1), 0, 0))


def _row_spec(width, n=1):
    return pl.BlockSpec((n, TM, width), lambda b, t: (b, t, 0))


def _rows_of(ref, prep=lambda s, x: x):
    return jnp.concatenate([prep(s, ref[s]) for s in range(ref.shape[0])], axis=0)


def _store_rows(ref, cols, val):
    for s in range(ref.shape[0]):
        ref[s, :, cols] = val[s * TM:(s + 1) * TM]


def _const_spec(shape):
    return pl.BlockSpec(shape, lambda b, t: (0,) * len(shape))


def _tab_spec(width):
    return pl.BlockSpec((TM, width), lambda b, t: (t, 0))


def _qkv_kernel(x_ref, mod_ref, g1_ref, w_ref, bd_ref, gq_ref, gk_ref, cos_ref, sin_ref,
                q_ref, k_ref, v_ref, *, nq, nk, rope):
    n = x_ref.shape[0]
    g1 = g1_ref[...]
    h = _rows_of(x_ref, lambda s, x: _prenorm(x, g1, mod_ref[s, 0][0:1], mod_ref[s, 0][1:2])).astype(BF16)
    acc = _dot(h, w_ref[...])
    bd = bd_ref[...]
    lane = lax.broadcasted_iota(jnp.int32, (1, 256), 1)
    first_half = (lane % 64) < 32
    cosw, sinw = jnp.concatenate([cos_ref[...]] * n, axis=0), jnp.concatenate([sin_ref[...]] * n, axis=0)

    def finish(z, gain):
        z = z * _seg_rms(z, bd, 1.0 / 64) * gain
        if rope:
            z = _rope(z, cosw, sinw, first_half, 32)
        return z.astype(BF16)

    for j in range(nq // 256):
        cols = slice(j * 256, (j + 1) * 256)
        _store_rows(q_ref, cols, finish(acc[:, cols], gq_ref[...]))
    for j in range(nk // 256):
        cols = slice(j * 256, (j + 1) * 256)
        _store_rows(k_ref, cols, finish(acc[:, nq + j * 256:nq + (j + 1) * 256], gk_ref[...]))
    _store_rows(v_ref, slice(None), acc[:, nq + nk:].astype(BF16))


def _qkv_project(x, modtab, g1, w, bd, gq, gk, cosw, sinw, nq, nk, rope):
    bsz = x.shape[0]
    n = _samples_per_step(bsz)
    return pl.pallas_call(
        functools.partial(_qkv_kernel, nq=nq, nk=nk, rope=rope),
        grid=(bsz // n, NTILES),
        in_specs=[_row_spec(D, n), _mod_spec(n), _const_spec((1, D)), _const_spec((D, nq + 2 * nk)),
                  _const_spec((256, 256)), _const_spec((1, 256)), _const_spec((1, 256)),
                  _tab_spec(256), _tab_spec(256)],
        out_specs=[_row_spec(nq, n), _row_spec(nk, n), _row_spec(nk, n)],
        out_shape=[jax.ShapeDtypeStruct((bsz, NT, nq), BF16), jax.ShapeDtypeStruct((bsz, NT, nk), BF16),
                   jax.ShapeDtypeStruct((bsz, NT, nk), BF16)],
        compiler_params=_cparams(2), name="qkv_project",
    )(x, modtab, g1, w, bd, gq, gk, cosw, sinw)


def _mla_kernel(x_ref, mod_ref, g1_ref, w1_ref, gql_ref, gkl_ref, gkr_ref, kcos_ref, ksin_ref,
                wuq_ref, wk_ref, wv_ref, bd_ref, qgain_ref, qinv_ref, qcos_ref, qsin_ref, kgain_ref, kmask_ref,
                q_ref, k_ref, v_ref):
    n = x_ref.shape[0]
    g1 = g1_ref[...]
    tile = lambda ref: jnp.concatenate([ref[...]] * n, axis=0)
    h = _rows_of(x_ref, lambda s, x: _prenorm(x, g1, mod_ref[s, 0][0:1], mod_ref[s, 0][1:2])).astype(BF16)
    a = _dot(h, w1_ref[...])
    cq = a[:, :B_Q_RANK]
    cq = (cq * lax.rsqrt(jnp.mean(cq * cq, axis=-1, keepdims=True) + EPS) * gql_ref[...]).astype(BF16)
    ckv = a[:, B_Q_RANK:B_Q_RANK + B_KV_RANK]
    ckv = (ckv * lax.rsqrt(jnp.mean(ckv * ckv, axis=-1, keepdims=True) + EPS) * gkl_ref[...]).astype(BF16)
    kr = a[:, B_Q_RANK + B_KV_RANK:]
    kr = kr * lax.rsqrt(jnp.sum(kr * kr, axis=-1, keepdims=True) * (1.0 / B_ROPE) + EPS) * gkr_ref[...]
    lane128 = lax.broadcasted_iota(jnp.int32, (1, LANES), 1)
    kr = _rope(kr, tile(kcos_ref), tile(ksin_ref), (lane128 % 32) < 16, 16).astype(BF16)

    bd = bd_ref[...]
    lane = lax.broadcasted_iota(jnp.int32, (1, 256), 1)
    first_half = (lane % 32) < 16
    qcos, qsin = tile(qcos_ref), tile(qsin_ref)
    q2 = _dot(cq, wuq_ref[...])
    k2 = _dot(jnp.concatenate([ckv, kr], axis=1), wk_ref[...])
    kmask = kmask_ref[...] > 0.0
    for j in range(q2.shape[1] // 256):
        sl = slice(j * 256, (j + 1) * 256)
        z = q2[:, sl]
        z = z * _seg_rms(z, bd, qinv_ref[...]) * qgain_ref[...]
        _store_rows(q_ref, sl, _rope(z, qcos, qsin, first_half, 16).astype(BF16))
        z = k2[:, sl]
        _store_rows(k_ref, sl, (z * jnp.where(kmask, _seg_rms(z, bd, 1.0 / 64) * kgain_ref[...], 1.0)).astype(BF16))
    _store_rows(v_ref, slice(None), _dot(ckv, wv_ref[...]).astype(BF16))


def _mla_project(x, modtab, g1, p):
    bsz = x.shape[0]
    n = _samples_per_step(bsz)
    hq = B_HEADS * LANES
    return pl.pallas_call(
        _mla_kernel,
        grid=(bsz // n, NTILES),
        in_specs=[_row_spec(D, n), _mod_spec(n), _const_spec((1, D)), _const_spec((D, 768)),
                  _const_spec((1, B_Q_RANK)), _const_spec((1, B_KV_RANK)), _const_spec((1, LANES)),
                  _tab_spec(LANES), _tab_spec(LANES),
                  _const_spec((B_Q_RANK, hq)), _const_spec((B_KV_RANK + LANES, hq)),
                  _const_spec((B_KV_RANK, B_HEADS * B_V)), _const_spec((256, 256)),
                  _const_spec((1, 256)), _const_spec((1, 256)), _tab_spec(256), _tab_spec(256),
                  _const_spec((1, 256)), _const_spec((1, 256))],
        out_specs=[_row_spec(hq, n), _row_spec(hq, n), _row_spec(B_HEADS * B_V, n)],
        out_shape=[jax.ShapeDtypeStruct((bsz, NT, hq), BF16), jax.ShapeDtypeStruct((bsz, NT, hq), BF16),
                   jax.ShapeDtypeStruct((bsz, NT, B_HEADS * B_V), BF16)],
        compiler_params=_cparams(2), name="mla_project",
    )(x, modtab, g1, p["w1"], p["gql"], p["gkl"], p["gkr"], p["kcos"], p["ksin"], p["wuq"], p["wk"], p["wv"],
      p["bd"], p["qgain"], p["qinv"], p["qcos"], p["qsin"], p["kgain"], p["kmask"])


def _softmax_pv(sa, sb, v):
    outs = []
    for s in (sa, sb):
        p = jnp.exp2(s - jnp.max(s, axis=-1, keepdims=True))
        l = jnp.sum(p, axis=-1, keepdims=True)
        outs.append(_dot(p.astype(BF16), v) / l)
    lane = lax.broadcasted_iota(jnp.int32, (1, LANES), 1)
    return jnp.where(lane < 64, outs[0], outs[1])


def _split_heads(q):
    lane = lax.broadcasted_iota(jnp.int32, (1, LANES), 1)
    zero = jnp.zeros_like(q)
    return jnp.where(lane < 64, q, zero), jnp.where(lane >= 64, q, zero)


PAIRS_PER_STEP = 4


DENSE_PAIRS_PER_STEP = 8


def _pair_attn_kernel(q_ref, k_ref, v_ref, o_ref, *, wide, q_pairs_per_kv):
    t = pl.program_id(2)

    def run(nkeys):
        for i in range(DENSE_PAIRS_PER_STEP):
            kv = i // q_pairs_per_kv
            v = v_ref[0, :nkeys, LANES * kv:LANES * (kv + 1)]
            if wide:
                q = q_ref[0, :, 256 * i:256 * (i + 1)]
                k = k_ref[0, :nkeys, 256 * kv:256 * (kv + 1)]
                sa, sb = _dot_t(q[:, :LANES], k[:, :LANES]), _dot_t(q[:, LANES:], k[:, LANES:])
            else:
                qa, qb = _split_heads(q_ref[0, :, LANES * i:LANES * (i + 1)])
                k = k_ref[0, :nkeys, LANES * kv:LANES * (kv + 1)]
                sa, sb = _dot_t(qa, k), _dot_t(qb, k)
            o_ref[0, :, LANES * i:LANES * (i + 1)] = _softmax_pv(sa, sb, v).astype(BF16)

    @pl.when(t == 0)
    def _():
        run(CTX)

    @pl.when(t > 0)
    def _():
        run(NT)


def _pair_attention(q, k, v, *, wide, q_pairs_per_kv):
    bsz = q.shape[0]
    qw = 256 if wide else LANES
    n_steps = q.shape[2] // (qw * DENSE_PAIRS_PER_STEP)
    kv_pairs = DENSE_PAIRS_PER_STEP // q_pairs_per_kv
    return pl.pallas_call(
        functools.partial(_pair_attn_kernel, wide=wide, q_pairs_per_kv=q_pairs_per_kv),
        grid=(bsz, n_steps, NTILES),
        in_specs=[pl.BlockSpec((1, TM, qw * DENSE_PAIRS_PER_STEP), lambda b, p, t: (b, t, p)),
                  pl.BlockSpec((1, NT, qw * kv_pairs), lambda b, p, t: (b, 0, p)),
                  pl.BlockSpec((1, NT, LANES * kv_pairs), lambda b, p, t: (b, 0, p))],
        out_specs=pl.BlockSpec((1, TM, LANES * DENSE_PAIRS_PER_STEP), lambda b, p, t: (b, t, p)),
        out_shape=jax.ShapeDtypeStruct((bsz, NT, n_steps * DENSE_PAIRS_PER_STEP * LANES), BF16),
        compiler_params=_cparams(3), name="pair_attention",
    )(q, k, v)


def _na_attn_kernel(q_ref, kc_ref, k0_ref, k1_ref, k2_ref, vc_ref, v0_ref, v1_ref, v2_ref, bias_ref, o_ref):
    t = pl.program_id(1)

    @pl.when(t == 0)
    def _():
        for i in range(PAIRS_PER_STEP):
            sl = slice(LANES * i, LANES * (i + 1))
            qa, qb = _split_heads(q_ref[0, :, sl])
            k = kc_ref[0, :, sl]
            o_ref[0, :, sl] = _softmax_pv(_dot_t(qa, k), _dot_t(qb, k), vc_ref[0, :, sl]).astype(BF16)

    @pl.when(t > 0)
    def _():
        for i in range(PAIRS_PER_STEP):
            sl = slice(LANES * i, LANES * (i + 1))
            qa, qb = _split_heads(q_ref[0, :, sl])
            k = jnp.concatenate([k0_ref[0, :, sl], k1_ref[0, :, sl], k2_ref[0, :, sl], kc_ref[0, :, sl]], axis=0)
            v = jnp.concatenate([v0_ref[0, :, sl], v1_ref[0, :, sl], v2_ref[0, :, sl], vc_ref[0, :, sl]], axis=0)
            sa = _dot_t(qa, k) + bias_ref[0, 2 * i]
            sb = _dot_t(qb, k) + bias_ref[0, 2 * i + 1]
            o_ref[0, :, sl] = _softmax_pv(sa, sb, v).astype(BF16)


def _na_band_tile(t):
    return 1 + jnp.clip(t - 2, 0, LAT_TILES - BAND_TILES)


def _na_attention(q, k, v, bias):
    bsz = q.shape[0]
    n_steps = C_HEADS // (2 * PAIRS_PER_STEP)
    width = LANES * PAIRS_PER_STEP
    qspec = pl.BlockSpec((1, TM, width), lambda p, t, b: (b, t, p))
    cspec = pl.BlockSpec((1, TM, width), lambda p, t, b: (b, 0, p))
    bands = [pl.BlockSpec((1, TM, width), functools.partial(lambda p, t, b, i: (b, _na_band_tile(t) + i, p), i=i))
             for i in range(BAND_TILES)]
    variant = lambda t: jnp.where(t <= 1, 0, jnp.where(t == LAT_TILES, 2, 1))
    return pl.pallas_call(
        _na_attn_kernel,
        grid=(n_steps, NTILES, bsz),
        in_specs=[qspec, cspec] + bands + [cspec] + bands
                 + [pl.BlockSpec((1, 2 * PAIRS_PER_STEP, TM, (BAND_TILES + 1) * TM),
                                 lambda p, t, b: (variant(t), p, 0, 0))],
        out_specs=qspec,
        out_shape=jax.ShapeDtypeStruct((bsz, NT, C_HEADS * C_HEAD_DIM), BF16),
        compiler_params=_cparams(3), name="na_attention",
    )(q, k, k, k, k, v, v, v, v, bias)


def _out_router_kernel(o_ref, wo_ref, x_ref, mod_ref, g2_ref, r_ref, xo_ref, h_ref, aff_ref):
    attn = _dot(_rows_of(o_ref), wo_ref[...])
    hs = []
    for s in range(x_ref.shape[0]):
        mod = mod_ref[s, 0]
        x = x_ref[s] + mod[2:3] * attn[s * TM:(s + 1) * TM]
        xo_ref[s] = x
        h = _prenorm(x, g2_ref[...], mod[3:4], mod[4:5]).astype(BF16)
        for r, word in enumerate(_pack_rows(h)):
            h_ref[s, pl.ds(r, TM, stride=HS_ROWS), :] = word
        hs.append(h)
    logits = _dot(jnp.concatenate(hs, axis=0), r_ref[...])
    lane = lax.broadcasted_iota(jnp.int32, (1, LANES), 1)
    logits = jnp.where(lane < N_EXPERTS, logits, NEG_INF)
    e = jnp.exp(logits - jnp.max(logits, axis=-1, keepdims=True))
    _store_rows(aff_ref, slice(None), e / jnp.sum(e, axis=-1, keepdims=True))


def _out_router(o, wo, x, modtab, g2, router):
    bsz = x.shape[0]
    n = _samples_per_step(bsz)
    return pl.pallas_call(
        _out_router_kernel,
        grid=(bsz // n, NTILES),
        in_specs=[_row_spec(D, n), _const_spec((D, D)), _row_spec(D, n), _mod_spec(n), _const_spec((1, D)),
                  _const_spec((D, LANES))],
        out_specs=[_row_spec(D, n), pl.BlockSpec((n, TM * HS_ROWS, LANES), lambda b, t: (b, t, 0)),
                   _row_spec(LANES, n)],
        out_shape=[jax.ShapeDtypeStruct((bsz, NT, D), F32),
                   jax.ShapeDtypeStruct((bsz, NT * HS_ROWS, LANES), jnp.int32),
                   jax.ShapeDtypeStruct((bsz, NT, LANES), F32)],
        input_output_aliases={2: 0},
        compiler_params=_cparams(2), name="out_router",
    )(o, wo, x, modtab, g2, router)


def _excl_cumsum_lanes(m, triu):
    out = []
    offset = jnp.zeros((m.shape[0], 1), F32)
    for i in range(m.shape[1] // TM):
        blk = m[:, i * TM:(i + 1) * TM]
        out.append(_dot(blk.astype(BF16), triu) + offset)
        offset = offset + jnp.sum(blk, axis=1, keepdims=True)
    return jnp.concatenate(out, axis=1) if len(out) > 1 else out[0]


def _kth_largest_bits(bits_list, caps):
    def count(bits, cand):
        return jnp.sum(jnp.where(bits >= cand, 1.0, 0.0), axis=1, keepdims=True)

    def step(i, thrs):
        b1 = jnp.left_shift(jnp.int32(1), 29 - 2 * i)
        b0 = jnp.left_shift(jnp.int32(1), 28 - 2 * i)
        out = []
        for bits, cap, thr in zip(bits_list, caps, thrs):
            c1, c2, c3 = count(bits, thr | b1), count(bits, thr | b0), count(bits, thr | b1 | b0)
            with_b1 = jnp.where(c3 >= cap, thr | b1 | b0, thr | b1)
            without = jnp.where(c2 >= cap, thr | b0, thr)
            out.append(jnp.where(c1 >= cap, with_b1, without))
        return tuple(out)

    top = jnp.int32(1 << 30)
    init = tuple(jnp.where(count(bits, top) >= cap, top, jnp.zeros((N_EXPERTS, 1), jnp.int32))
                 for bits, cap in zip(bits_list, caps))
    return lax.fori_loop(0, 15, step, init)


def _topk_slots(bits, thr, cap, triu):
    gt = bits > thr
    eq = bits == thr
    need = cap - jnp.sum(jnp.where(gt, 1.0, 0.0), axis=1, keepdims=True)
    eq_rank = _excl_cumsum_lanes(jnp.where(eq, 1.0, 0.0), triu)
    sel = gt | (eq & (eq_rank < need))
    rank = _excl_cumsum_lanes(jnp.where(sel, 1.0, 0.0), triu)
    return jnp.where(sel, rank, -1.0), rank


OFF_LANE = 3 * N_EXPERTS


def _token_values(aff, first_row):
    n = aff.shape[0]
    hi = aff.astype(BF16).astype(F32)
    r1 = aff - hi
    mid = r1.astype(BF16).astype(F32)
    lo = (r1 - mid).astype(BF16).astype(F32)
    off = (lax.broadcasted_iota(jnp.int32, (n, 1), 0) + first_row) * HS_ROWS
    lane = lax.broadcasted_iota(jnp.int32, (1, LANES), 1)
    vals = (hi + pltpu.roll(mid, N_EXPERTS, axis=1) + pltpu.roll(lo, 2 * N_EXPERTS, axis=1)
            + jnp.where(lane == OFF_LANE, (off >> 7).astype(F32), 0.0)
            + jnp.where(lane == OFF_LANE + 1, (off & (LANES - 1)).astype(F32), 0.0))
    return vals.astype(BF16)


def _topk_kernel(aff_ref, slot_ref, off_ref, gate_ref, win_ref):
    r = lax.broadcasted_iota(jnp.int32, (TM, TM), 0)
    c = lax.broadcasted_iota(jnp.int32, (TM, TM), 1)
    triu = jnp.where(r < c, 1.0, 0.0).astype(BF16)
    lane = lax.broadcasted_iota(jnp.int32, (1, LANES), 1)
    pad_rows = lambda x, fill: jnp.concatenate(
        [x, jnp.full((LANES - x.shape[0], x.shape[1]), fill, x.dtype)], axis=0) if x.shape[0] < LANES else x
    aff_t = aff_ref[0].T[:N_EXPERTS]
    segments = ((0, CTX, CAP_CTX, CAP_LAT), (CTX, NT, CAP_LAT, 0))
    bits = [pltpu.bitcast(aff_t[:, lo:hi], jnp.int32) for lo, hi, _, _ in segments]
    thrs = _kth_largest_bits(bits, [cap for _, _, cap, _ in segments])
    for (lo, hi, cap, dst), seg_bits, thr in zip(segments, bits, thrs):
        slot, rank = _topk_slots(seg_bits, thr, cap, triu)
        slot_ref[0, lo:hi] = pad_rows(slot, -1.0).T.astype(jnp.int32)
        if lo == CTX:
            starts = jnp.where(lane == LAT_TILES, float(cap), 0.0)
            for j in range(LAT_TILES):
                starts = starts + jnp.where(lane == j, rank[:, TM * j:TM * j + 1], 0.0)
            run = pad_rows(starts, 0.0).T.astype(jnp.int32)
            first = jnp.minimum((run[:LAT_TILES] >> 4) << 4, CAP_LAT - WIN)
            ends = pltpu.roll(run[1:LAT_TILES + 1], N_EXPERTS, axis=1)
            win = jnp.where(lane < N_EXPERTS, first, ends)
            for j in range(LAT_TILES):
                win_ref[0, j] = win[j:j + 1]
        vals = _token_values(aff_ref[0, lo:hi], lo)
        slot_ids = lax.broadcasted_iota(jnp.int32, (cap, 1), 0).astype(F32)
        for e in range(N_EXPERTS):
            hit = jnp.where(slot_ids == slot[e:e + 1, :], 1.0, 0.0).astype(BF16)
            picked = _dot(hit, vals)
            mine = ((lane & (N_EXPERTS - 1)) == e) & (lane < OFF_LANE)
            gate_ref[0, e, dst:dst + cap, :] = jnp.sum(jnp.where(mine, picked, 0.0), axis=1, keepdims=True)
            by_lane = pad_rows(picked, 0.0).T
            off = by_lane[OFF_LANE:OFF_LANE + 1] * float(LANES) + by_lane[OFF_LANE + 1:OFF_LANE + 2]
            off_ref[0, e, :, dst:dst + cap] = off[:, :cap].astype(jnp.int32)


def _topk(aff):
    bsz = aff.shape[0]
    tspec = pl.BlockSpec((1, NT, LANES), lambda b: (b, 0, 0))
    return pl.pallas_call(
        _topk_kernel, grid=(bsz,),
        in_specs=[tspec],
        out_specs=[tspec, pl.BlockSpec((1, N_EXPERTS, 1, CAP_ALL), lambda b: (b, 0, 0, 0)),
                   pl.BlockSpec((1, N_EXPERTS, CAP_ALL, 1), lambda b: (b, 0, 0, 0)),
                   pl.BlockSpec((1, LAT_TILES, 1, LANES), lambda b: (b, 0, 0, 0))],
        out_shape=[jax.ShapeDtypeStruct((bsz, NT, LANES), jnp.int32),
                   jax.ShapeDtypeStruct((bsz, N_EXPERTS, 1, CAP_ALL), jnp.int32),
                   jax.ShapeDtypeStruct((bsz, N_EXPERTS, CAP_ALL, 1), F32),
                   jax.ShapeDtypeStruct((bsz, LAT_TILES, 1, LANES), jnp.int32)],
        compiler_params=_cparams(1), name="expert_topk",
    )(aff)


def _pack_rows(hb):
    words = []
    for r in range(HS_ROWS):
        lo = pltpu.bitcast(hb[:, 256 * r:256 * r + LANES].astype(F32), jnp.int32)
        hi = pltpu.bitcast(hb[:, 256 * r + LANES:256 * (r + 1)].astype(F32), jnp.int32)
        words.append(lax.shift_right_logical(lo, jnp.int32(16)) | hi)
    return words


def _unpack_rows(words):
    cols = []
    for w in words:
        cols.append(pltpu.bitcast(lax.shift_left(w, jnp.int32(16)), F32).astype(BF16))
        cols.append(pltpu.bitcast(w & jnp.int32(-65536), F32).astype(BF16))
    return jnp.concatenate(cols, axis=1)


def _expert_kernel(hs_ref, off_ref, gate_ref, wg_ref, wu_ref, wd_ref, *refs, n_slots):
    out_refs, (tile_ref, wg_s, wu_s, wd_s) = refs[:-4], refs[-4:]

    @pl.when(pl.program_id(1) == 0)
    def _():
        wg_s[...] = wg_ref[0, 0].astype(BF16)
        wu_s[...] = wu_ref[0, 0].astype(BF16)
        wd_s[...] = wd_ref[0, 0].astype(BF16)

    for j in range(n_slots):
        src = pl.multiple_of(off_ref[0, 0, 0, j], HS_ROWS)
        tile_ref[pl.ds(j, HS_ROWS, stride=TILE_STRIDE), :] = hs_ref[0, pl.ds(src, HS_ROWS), :]
    xg = _unpack_rows([tile_ref[r * TILE_STRIDE:r * TILE_STRIDE + n_slots, :] for r in range(HS_ROWS)])
    hid = (_silu(_dot(xg, wg_s[...])) * _dot(xg, wu_s[...])).astype(BF16)
    y = _dot(hid, wd_s[...]) * gate_ref[0, 0, :n_slots]
    out_refs[0][0] = y[:CAP_LAT].astype(BF16)
    if n_slots > CAP_LAT:
        out_refs[1][0] = y[CAP_LAT:].astype(BF16)


def _experts(hs, off, gate, wg, wu, wd, layer, with_ctx):
    bsz = hs.shape[0]
    n_slots = CAP_ALL if with_ctx else CAP_LAT
    wspec = pl.BlockSpec((1, 1, D, EXPERT_FF), lambda e, b: (layer, e, 0, 0))
    out_specs = [pl.BlockSpec((1, CAP_LAT, D), lambda e, b: (b, e, 0))]
    out_shape = [jax.ShapeDtypeStruct((bsz, N_EXPERTS * CAP_LAT, D), BF16)]
    if with_ctx:
        out_specs.append(pl.BlockSpec((1, CAP_CTX, D), lambda e, b: (b, e, 0)))
        out_shape.append(jax.ShapeDtypeStruct((bsz, N_EXPERTS * CAP_CTX, D), BF16))
    return pl.pallas_call(
        functools.partial(_expert_kernel, n_slots=n_slots),
        grid=(N_EXPERTS, bsz),
        in_specs=[pl.BlockSpec((1, NT * HS_ROWS, LANES), lambda e, b: (b, 0, 0)),
                  pl.BlockSpec((1, 1, 1, CAP_ALL), lambda e, b: (b, e, 0, 0), memory_space=pltpu.SMEM),
                  pl.BlockSpec((1, 1, CAP_ALL, 1), lambda e, b: (b, e, 0, 0)),
                  wspec, wspec, pl.BlockSpec((1, 1, EXPERT_FF, D), lambda e, b: (layer, e, 0, 0))],
        out_specs=out_specs, out_shape=out_shape,
        scratch_shapes=[pltpu.VMEM((HS_ROWS * TILE_STRIDE, LANES), jnp.int32),
                        pltpu.VMEM((D, EXPERT_FF), BF16), pltpu.VMEM((D, EXPERT_FF), BF16),
                        pltpu.VMEM((EXPERT_FF, D), BF16)],
        compiler_params=_cparams(2), name="experts",
    )(hs, off, gate, wg, wu, wd)


WIN = 64
WIN_GROUP = 256 // WIN


def _combine_lat(slot_ref, yl_ref, win_ref, x_ref, gate, xo_ref):
    slot = slot_ref[0]
    fits = win_ref[0, 0, 0, N_EXPERTS] - win_ref[0, 0, 0, 0] <= WIN
    for e in range(1, N_EXPERTS):
        fits = jnp.logical_and(fits, win_ref[0, 0, 0, N_EXPERTS + e] - win_ref[0, 0, 0, e] <= WIN)

    @pl.when(fits)
    def _():
        cols = lax.broadcasted_iota(jnp.int32, (TM, WIN * WIN_GROUP), 1)
        acc = None
        for g in range(N_EXPERTS // WIN_GROUP):
            hit = cols < 0
            rows = []
            for j in range(WIN_GROUP):
                e = WIN_GROUP * g + j
                start = win_ref[0, 0, 0, e]
                s = slot[:, e:e + 1]
                hit = hit | (jnp.where(s >= 0, s - start + WIN * j, -1) == cols)
                rows.append(yl_ref[0, pl.ds(pl.multiple_of(e * CAP_LAT + start, 16), WIN), :])
            part = _dot(jnp.where(hit, 1.0, 0.0).astype(BF16), jnp.concatenate(rows, axis=0))
            acc = part if acc is None else acc + part
        xo_ref[0] = x_ref[0] + gate * acc

    @pl.when(jnp.logical_not(fits))
    def _():
        cols = lax.broadcasted_iota(jnp.int32, (TM, CAP_LAT), 1)
        onehot = jnp.concatenate(
            [jnp.where(slot[:, e:e + 1] == cols, 1.0, 0.0).astype(BF16) for e in range(N_EXPERTS)], axis=1)
        xo_ref[0] = x_ref[0] + gate * _dot(onehot, yl_ref[0])


def _combine_kernel(slot_ref, yl_ref, yc_ref, win_ref, x_ref, mod_ref, xo_ref):
    t = pl.program_id(1)
    gate = mod_ref[0, 0][5:6]

    @pl.when(t == 0)
    def _():
        slot = slot_ref[0]
        cols = lax.broadcasted_iota(jnp.int32, (TM, N_EXPERTS * CAP_CTX), 1)
        hit = cols < 0
        for e in range(N_EXPERTS):
            s = slot[:, e:e + 1]
            hit = hit | (jnp.where(s >= 0, s + e * CAP_CTX, -1) == cols)
        xo_ref[0] = x_ref[0] + gate * _dot(jnp.where(hit, 1.0, 0.0).astype(BF16), yc_ref[0])

    @pl.when(t > 0)
    def _():
        _combine_lat(slot_ref, yl_ref, win_ref, x_ref, gate, xo_ref)


def _win_spec(tile_of_step):
    return pl.BlockSpec((1, 1, 1, LANES), lambda b, t: (b, tile_of_step(t), 0, 0), memory_space=pltpu.SMEM)


def _combine(slot, yl, yc, win, x, modtab):
    bsz = x.shape[0]
    return pl.pallas_call(
        _combine_kernel,
        grid=(bsz, NTILES),
        in_specs=[_row_spec(LANES), pl.BlockSpec((1, N_EXPERTS * CAP_LAT, D), lambda b, t: (b, 0, 0)),
                  pl.BlockSpec((1, N_EXPERTS * CAP_CTX, D), lambda b, t: (b, 0, 0)),
                  _win_spec(lambda t: jnp.maximum(t - 1, 0)), _row_spec(D), _mod_spec()],
        out_specs=_row_spec(D),
        out_shape=jax.ShapeDtypeStruct((bsz, NT, D), F32),
        input_output_aliases={4: 0},
        compiler_params=_cparams(2), name="moe_combine",
    )(slot, yl, yc, win, x, modtab)


def _combine_last_kernel(slot_ref, yl_ref, win_ref, x_ref, mod_ref, xo_ref):
    _combine_lat(slot_ref, yl_ref, win_ref, x_ref, mod_ref[0, 0][5:6], xo_ref)


def _combine_last(slot, yl, win, x, modtab):
    bsz = x.shape[0]
    lat = lambda width: pl.BlockSpec((1, TM, width), lambda b, t: (b, t + 1, 0))
    return pl.pallas_call(
        _combine_last_kernel,
        grid=(bsz, LAT_TILES),
        in_specs=[lat(LANES), pl.BlockSpec((1, N_EXPERTS * CAP_LAT, D), lambda b, t: (b, 0, 0)),
                  _win_spec(lambda t: t), lat(D), pl.BlockSpec((1, 1, 6, D), lambda b, t: (b, 1, 0, 0))],
        out_specs=pl.BlockSpec((1, TM, D), lambda b, t: (b, t, 0)),
        out_shape=jax.ShapeDtypeStruct((bsz, SEQ, D), F32),
        compiler_params=_cparams(2), name="moe_combine_last",
    )(slot, yl, win, x, modtab)


def _rope_tables(rot_dim, width, lane_lo, lane_hi):
    n_freq = rot_dim // 4
    half = rot_dim // 2
    inv = jnp.float32(ROPE_THETA) ** (-jnp.arange(n_freq, dtype=F32) / n_freq)
    t = jnp.arange(SEQ, dtype=jnp.int32)
    row = (t // GRID_W).astype(F32)
    col = (t % GRID_W).astype(F32)
    ang = jnp.concatenate([row[:, None] * inv, col[:, None] * inv], axis=-1)
    cos, sin = jnp.cos(ang), jnp.sin(ang)
    lane = np.arange(width)
    inside = (lane % LANES >= lane_lo) & (lane % LANES < lane_hi)
    idx = lane % half
    sign = np.where(lane % rot_dim < half, -1.0, 1.0).astype(np.float32)
    cosw = jnp.where(inside[None, :], cos[:, idx], 1.0)
    sinw = jnp.where(inside[None, :], sin[:, idx] * sign[None, :], 0.0)
    ones = jnp.ones((CTX, width), F32)
    return jnp.concatenate([ones, cosw], axis=0), jnp.concatenate([0.0 * ones, sinw], axis=0)


def _block_diag_ones():
    i = np.arange(256)
    return jnp.asarray((i[:, None] // 64) == (i[None, :] // 64), dtype=BF16)


_GQA_ORDER = np.array([8 * kp + 4 * odd + i for kp in range(2) for i in range(4) for odd in range(2)])


def _na_bias(rpb):
    out = []
    rows = SEQ // GRID_W
    qrows, krows = TM // GRID_W, BAND_TILES * TM // GRID_W
    col = np.arange(GRID_W)
    cs = np.clip(col - NA_COLS // 2, 0, GRID_W - NA_COLS)
    ok_c = (col[None, :] >= cs[:, None]) & (col[None, :] < cs[:, None] + NA_COLS)
    dc = np.clip(col[None, :] - col[:, None] + NA_COLS - 1, 0, 2 * NA_COLS - 2)
    pick_c = (dc[:, :, None] == np.arange(2 * NA_COLS - 1)).astype(np.float32)
    for r0, bs in ((0, 0), (8, 4), (rows - 4, rows - 12)):
        r, kr = r0 + np.arange(qrows), bs + np.arange(krows)
        rs = np.clip(r - NA_ROWS // 2, 0, rows - NA_ROWS)
        ok_r = (kr[None, :] >= rs[:, None]) & (kr[None, :] < rs[:, None] + NA_ROWS)
        dr = kr[None, :] - r[:, None] + NA_ROWS - 1
        pick_r = (dr[:, :, None] == np.arange(2 * NA_ROWS - 1)).astype(np.float32)
        t1 = jnp.einsum('ard,hde->hare', pick_r, rpb.astype(F32), precision=lax.Precision.HIGHEST)
        band = jnp.einsum('hare,cse->hacrs', t1, pick_c, precision=lax.Precision.HIGHEST)
        ok = ok_r[:, None, :, None] & ok_c[None, :, None, :]
        band = jnp.where(ok[None], band * LOG2E, NEG_INF).reshape(C_HEADS, TM, BAND_TILES * TM)
        out.append(jnp.concatenate([band, jnp.zeros((C_HEADS, TM, TM), F32)], axis=-1))
    return jnp.stack(out)


def _gqa_params(w_qkv, qn, kn, wo):
    nq, nk = A_HEADS * A_HEAD_DIM, A_KV_HEADS * A_HEAD_DIM
    wq = w_qkv[:, :nq].reshape(D, A_HEADS, A_HEAD_DIM)[:, _GQA_ORDER].reshape(D, nq)
    w = jnp.concatenate([wq, w_qkv[:, nq:]], axis=1).astype(BF16)
    wo_p = wo.reshape(A_HEADS, A_HEAD_DIM, D)[_GQA_ORDER].reshape(nq, D).astype(BF16)
    scale = A_HEAD_DIM ** -0.5 * LOG2E
    return w, (jnp.tile(qn, 4) * scale)[None], jnp.tile(kn, 4)[None], wo_p


def _mla_params(w_dq, qn_lat, w_uq, w_dkv, kvn_lat, w_ukv, qn, kn, cossin_q, cossin_k):
    qd = B_NOPE + B_ROPE
    w1 = jnp.concatenate([w_dq, w_dkv, jnp.zeros((D, 768 - B_Q_RANK - B_KV_RANK - B_ROPE), F32)], axis=1)
    pad_q = jnp.zeros((B_Q_RANK, B_HEADS, LANES - qd), F32)
    wuq = jnp.concatenate([w_uq.reshape(B_Q_RANK, B_HEADS, qd), pad_q], axis=2).reshape(B_Q_RANK, B_HEADS * LANES)
    ukv = w_ukv.reshape(B_KV_RANK, B_HEADS, B_NOPE + B_V)
    wk_top = jnp.concatenate([ukv[:, :, :B_NOPE], jnp.zeros((B_KV_RANK, B_HEADS, LANES - B_NOPE), F32)], axis=2)
    route = np.zeros((LANES, B_HEADS, LANES), np.float32)
    route[np.arange(B_ROPE), :, B_NOPE + np.arange(B_ROPE)] = 1.0
    wk = jnp.concatenate([wk_top.reshape(B_KV_RANK, -1), jnp.asarray(route).reshape(LANES, -1)], axis=0)
    wv = ukv[:, :, B_NOPE:].reshape(B_KV_RANK, B_HEADS * B_V)
    scale = qd ** -0.5 * LOG2E
    zpad = jnp.zeros((LANES - qd,), F32)
    qgain = jnp.tile(jnp.concatenate([qn * scale, zpad]), 2)[None]
    qinv = jnp.tile(jnp.concatenate([jnp.full((B_NOPE,), 1.0 / B_NOPE), jnp.full((LANES - B_NOPE,), 1.0 / B_ROPE)]), 2)
    kgain = jnp.tile(jnp.concatenate([kn[:B_NOPE], jnp.zeros((LANES - B_NOPE,), F32)]), 2)[None]
    kmask = jnp.tile(jnp.concatenate([jnp.ones((B_NOPE,), F32), jnp.zeros((LANES - B_NOPE,), F32)]), 2)[None]
    gkr = jnp.concatenate([kn[B_NOPE:], jnp.zeros((LANES - B_ROPE,), F32)])[None]
    return dict(w1=w1.astype(BF16), gql=qn_lat[None], gkl=kvn_lat[None], gkr=gkr, kcos=cossin_k[0], ksin=cossin_k[1],
                wuq=wuq.astype(BF16), wk=wk.astype(BF16), wv=wv.astype(BF16), bd=_block_diag_ones(),
                qgain=qgain, qinv=qinv[None].astype(F32), qcos=cossin_q[0], qsin=cossin_q[1], kgain=kgain, kmask=kmask)


def kernel(x, c, ctx, c_ctx, ada_w, ada_b, norm1_w, norm2_w, a_wqkv, a_qnorm, a_knorm, a_wo, b_wdq, b_qnorm_lat, b_wuq, b_wdkv, b_kvnorm_lat, b_wukv, b_qnorm, b_knorm, b_wo, c_wqkv, c_qnorm, c_knorm, c_rpb, c_wo, moe_router, moe_wg, moe_wu, moe_wd):
    bsz = x.shape[0]
    assert x.shape[1:] == (SEQ, D) and ctx.shape[1:] == (CTX, D)
    mod_rows = -(-(bsz + 1) // 16) * 16
    cvec = jnp.concatenate([c, c_ctx[None], jnp.zeros((mod_rows - bsz - 1, D), F32)], axis=0)
    mods = _ada_all(cvec, ada_w, ada_b)
    xs = jnp.concatenate([ctx, x], axis=1)

    bd = _block_diag_ones()
    rope64 = _rope_tables(A_HEAD_DIM, 256, 0, LANES)
    rope_mla_q = _rope_tables(B_ROPE, 256, B_NOPE, B_NOPE + B_ROPE)
    rope_mla_k = _rope_tables(B_ROPE, LANES, 0, B_ROPE)
    router = jnp.pad(moe_router, ((0, 0), (0, 0), (0, LANES - N_EXPERTS))).astype(BF16)

    for i in range(DEPTH):
        last = i == DEPTH - 1
        kind, j = i % 3, i // 3
        m = mods[i]
        m_lat = m[:bsz].reshape(bsz, 6, D)
        m_ctx = jnp.broadcast_to(m[bsz].reshape(1, 6, D), (bsz, 6, D))
        modtab = jnp.stack([m_ctx, m_lat], axis=1)
        g1, g2 = norm1_w[i][None], norm2_w[i][None]

        if kind == 0:
            w, gq, gk, wo = _gqa_params(a_wqkv[j], a_qnorm[j], a_knorm[j], a_wo[j])
            q, k, v = _qkv_project(xs, modtab, g1, w, bd, gq, gk, rope64[0], rope64[1],
                                   A_HEADS * A_HEAD_DIM, A_KV_HEADS * A_HEAD_DIM, True)
            o = _pair_attention(q, k, v, wide=False, q_pairs_per_kv=4)
        elif kind == 1:
            p = _mla_params(b_wdq[j], b_qnorm_lat[j], b_wuq[j], b_wdkv[j], b_kvnorm_lat[j], b_wukv[j],
                            b_qnorm[j], b_knorm[j], rope_mla_q, rope_mla_k)
            q, k, v = _mla_project(xs, modtab, g1, p)
            o = _pair_attention(q, k, v, wide=True, q_pairs_per_kv=1)
            wo = b_wo[j].astype(BF16)
        else:
            hd = C_HEADS * C_HEAD_DIM
            gq = (jnp.tile(c_qnorm[j], 4) * (C_HEAD_DIM ** -0.5 * LOG2E))[None]
            gk = jnp.tile(c_knorm[j], 4)[None]
            q, k, v = _qkv_project(xs, modtab, g1, c_wqkv[j].astype(BF16), bd, gq, gk, rope64[0], rope64[1],
                                   hd, hd, False)
            o = _na_attention(q, k, v, _na_bias(c_rpb[j]))
            wo = c_wo[j].astype(BF16)

        xs, hs, aff = _out_router(o, wo, xs, modtab, g2, router[i])
        slot, off, gate, win = _topk(aff)
        ys = _experts(hs, off, gate,
                      moe_wg, moe_wu, moe_wd, layer=i, with_ctx=not last)
        if last:
            return _combine_last(slot, ys[0], win, xs, modtab)
        xs = _combine(slot, ys[0], ys[1], win, xs, modtab)
```

```python
import functools

import numpy as np
import jax
import jax.numpy as jnp
from jax import lax
from jax.experimental import pallas as pl
from jax.experimental.pallas import tpu as pltpu

F32 = jnp.float32
BF16 = jnp.bfloat16

D = 1024
SEQ = 2048
CTX = 256
NT = CTX + SEQ
DEPTH = 4
GRID_W = 64
ROPE_THETA = 10000.0
EPS = 1e-6
NEG_INF = -1e30
LOG2E = 1.4426950408889634

TM = 256
NTILES = NT // TM
LAT_TILES = SEQ // TM

A_HEADS, A_KV_HEADS, A_HEAD_DIM = 16, 4, 64
B_HEADS, B_Q_RANK, B_KV_RANK, B_NOPE, B_ROPE, B_V = 16, 384, 256, 64, 32, 64
C_HEADS, C_HEAD_DIM, NA_ROWS, NA_COLS = 16, 64, 8, 16
N_EXPERTS, EXPERT_FF = 16, 1024
CAP_LAT = 2 * SEQ // N_EXPERTS
CAP_CTX = 2 * CTX // N_EXPERTS
CAP_ALL = CAP_LAT + CAP_CTX
HS_ROWS = D // 256
TILE_STRIDE = 296
LANES = 128
BAND_TILES = 3

VMEM_LIMIT = 56 * 1024 * 1024


def _cparams(n_axes):
    return pltpu.CompilerParams(dimension_semantics=("arbitrary",) * n_axes, vmem_limit_bytes=VMEM_LIMIT)


def _dot(a, b):
    return jnp.dot(a, b, preferred_element_type=F32)


def _dot_t(a, b):
    return lax.dot_general(a, b, (((1,), (1,)), ((), ())), preferred_element_type=F32)


def _silu(x):
    return x / (1.0 + jnp.exp(-x))


def _prenorm(x, gain, shift, scale):
    ms = jnp.mean(x * x, axis=-1, keepdims=True)
    return (x * lax.rsqrt(ms + EPS) * gain) * (1.0 + scale) + shift


def _seg_rms(z, bd, inv_n):
    z2 = z * z
    hi = z2.astype(BF16)
    lo = (z2 - hi.astype(F32)).astype(BF16)
    ss = _dot(hi, bd) + _dot(lo, bd)
    return lax.rsqrt(ss * inv_n + EPS)


def _rope(z, cosw, sinw, first_half, half):
    w = z.shape[-1]
    up = pltpu.roll(z, w - half, axis=1)
    dn = pltpu.roll(z, half, axis=1)
    return z * cosw + jnp.where(first_half, up, dn) * sinw


def _ada_kernel(c_ref, w_ref, b_ref, o_ref):
    s = _silu(c_ref[...]).astype(BF16)
    o_ref[0] = _dot(s, w_ref[0].astype(BF16)) + b_ref[0]


def _ada_all(cvec, ada_w, ada_b):
    rows = cvec.shape[0]
    tn = 512
    return pl.pallas_call(
        _ada_kernel,
        grid=(DEPTH, 6 * D // tn),
        in_specs=[pl.BlockSpec((rows, D), lambda i, j: (0, 0)),
                  pl.BlockSpec((1, D, tn), lambda i, j: (i, 0, j)),
                  pl.BlockSpec((1, 1, tn), lambda i, j: (i, 0, j))],
        out_specs=pl.BlockSpec((1, rows, tn), lambda i, j: (i, 0, j)),
        out_shape=jax.ShapeDtypeStruct((DEPTH, rows, 6 * D), F32),
        compiler_params=_cparams(2), name="adaln",
    )(cvec, ada_w, ada_b.reshape(DEPTH, 1, 6 * D))


def _samples_per_step(bsz):
    return 2 if bsz % 2 == 0 else 1


def _mod_spec(n=1):
    return pl.BlockSpec((n, 1, 6, D), lambda b, t: (b, jnp.minimum(t, 1), 0, 0))


def _row_spec(width, n=1):
    return pl.BlockSpec((n, TM, width), lambda b, t: (b, t, 0))


def _rows_of(ref, prep=lambda s, x: x):
    return jnp.concatenate([prep(s, ref[s]) for s in range(ref.shape[0])], axis=0)


def _store_rows(ref, cols, val):
    for s in range(ref.shape[0]):
        ref[s, :, cols] = val[s * TM:(s + 1) * TM]


def _const_spec(shape):
    return pl.BlockSpec(shape, lambda b, t: (0,) * len(shape))


def _tab_spec(width):
    return pl.BlockSpec((TM, width), lambda b, t: (t, 0))


def _qkv_kernel(x_ref, mod_ref, g1_ref, w_ref, bd_ref, gq_ref, gk_ref, cos_ref, sin_ref,
                q_ref, k_ref, v_ref, *, nq, nk, rope):
    n = x_ref.shape[0]
    g1 = g1_ref[...]
    h = _rows_of(x_ref, lambda s, x: _prenorm(x, g1, mod_ref[s, 0][0:1], mod_ref[s, 0][1:2])).astype(BF16)
    acc = _dot(h, w_ref[...])
    bd = bd_ref[...]
    lane = lax.broadcasted_iota(jnp.int32, (1, 256), 1)
    first_half = (lane % 64) < 32
    cosw, sinw = jnp.concatenate([cos_ref[...]] * n, axis=0), jnp.concatenate([sin_ref[...]] * n, axis=0)

    def finish(z, gain):
        z = z * _seg_rms(z, bd, 1.0 / 64) * gain
        if rope:
            z = _rope(z, cosw, sinw, first_half, 32)
        return z.astype(BF16)

    for j in range(nq // 256):
        cols = slice(j * 256, (j + 1) * 256)
        _store_rows(q_ref, cols, finish(acc[:, cols], gq_ref[...]))
    for j in range(nk // 256):
        cols = slice(j * 256, (j + 1) * 256)
        _store_rows(k_ref, cols, finish(acc[:, nq + j * 256:nq + (j + 1) * 256], gk_ref[...]))
    _store_rows(v_ref, slice(None), acc[:, nq + nk:].astype(BF16))


def _qkv_project(x, modtab, g1, w, bd, gq, gk, cosw, sinw, nq, nk, rope):
    bsz = x.shape[0]
    n = _samples_per_step(bsz)
    return pl.pallas_call(
        functools.partial(_qkv_kernel, nq=nq, nk=nk, rope=rope),
        grid=(bsz // n, NTILES),
        in_specs=[_row_spec(D, n), _mod_spec(n), _const_spec((1, D)), _const_spec((D, nq + 2 * nk)),
                  _const_spec((256, 256)), _const_spec((1, 256)), _const_spec((1, 256)),
                  _tab_spec(256), _tab_spec(256)],
        out_specs=[_row_spec(nq, n), _row_spec(nk, n), _row_spec(nk, n)],
        out_shape=[jax.ShapeDtypeStruct((bsz, NT, nq), BF16), jax.ShapeDtypeStruct((bsz, NT, nk), BF16),
                   jax.ShapeDtypeStruct((bsz, NT, nk), BF16)],
        compiler_params=_cparams(2), name="qkv_project",
    )(x, modtab, g1, w, bd, gq, gk, cosw, sinw)


def _mla_kernel(x_ref, mod_ref, g1_ref, w1_ref, gql_ref, gkl_ref, gkr_ref, kcos_ref, ksin_ref,
                wuq_ref, wk_ref, wv_ref, bd_ref, qgain_ref, qinv_ref, qcos_ref, qsin_ref, kgain_ref, kmask_ref,
                q_ref, k_ref, v_ref):
    n = x_ref.shape[0]
    g1 = g1_ref[...]
    tile = lambda ref: jnp.concatenate([ref[...]] * n, axis=0)
    h = _rows_of(x_ref, lambda s, x: _prenorm(x, g1, mod_ref[s, 0][0:1], mod_ref[s, 0][1:2])).astype(BF16)
    a = _dot(h, w1_ref[...])
    cq = a[:, :B_Q_RANK]
    cq = (cq * lax.rsqrt(jnp.mean(cq * cq, axis=-1, keepdims=True) + EPS) * gql_ref[...]).astype(BF16)
    ckv = a[:, B_Q_RANK:B_Q_RANK + B_KV_RANK]
    ckv = (ckv * lax.rsqrt(jnp.mean(ckv * ckv, axis=-1, keepdims=True) + EPS) * gkl_ref[...]).astype(BF16)
    kr = a[:, B_Q_RANK + B_KV_RANK:]
    kr = kr * lax.rsqrt(jnp.sum(kr * kr, axis=-1, keepdims=True) * (1.0 / B_ROPE) + EPS) * gkr_ref[...]
    lane128 = lax.broadcasted_iota(jnp.int32, (1, LANES), 1)
    kr = _rope(kr, tile(kcos_ref), tile(ksin_ref), (lane128 % 32) < 16, 16).astype(BF16)

    bd = bd_ref[...]
    lane = lax.broadcasted_iota(jnp.int32, (1, 256), 1)
    first_half = (lane % 32) < 16
    qcos, qsin = tile(qcos_ref), tile(qsin_ref)
    q2 = _dot(cq, wuq_ref[...])
    k2 = _dot(jnp.concatenate([ckv, kr], axis=1), wk_ref[...])
    kmask = kmask_ref[...] > 0.0
    for j in range(q2.shape[1] // 256):
        sl = slice(j * 256, (j + 1) * 256)
        z = q2[:, sl]
        z = z * _seg_rms(z, bd, qinv_ref[...]) * qgain_ref[...]
        _store_rows(q_ref, sl, _rope(z, qcos, qsin, first_half, 16).astype(BF16))
        z = k2[:, sl]
        _store_rows(k_ref, sl, (z * jnp.where(kmask, _seg_rms(z, bd, 1.0 / 64) * kgain_ref[...], 1.0)).astype(BF16))
    _store_rows(v_ref, slice(None), _dot(ckv, wv_ref[...]).astype(BF16))


def _mla_project(x, modtab, g1, p):
    bsz = x.shape[0]
    n = _samples_per_step(bsz)
    hq = B_HEADS * LANES
    return pl.pallas_call(
        _mla_kernel,
        grid=(bsz // n, NTILES),
        in_specs=[_row_spec(D, n), _mod_spec(n), _const_spec((1, D)), _const_spec((D, 768)),
                  _const_spec((1, B_Q_RANK)), _const_spec((1, B_KV_RANK)), _const_spec((1, LANES)),
                  _tab_spec(LANES), _tab_spec(LANES),
                  _const_spec((B_Q_RANK, hq)), _const_spec((B_KV_RANK + LANES, hq)),
                  _const_spec((B_KV_RANK, B_HEADS * B_V)), _const_spec((256, 256)),
                  _const_spec((1, 256)), _const_spec((1, 256)), _tab_spec(256), _tab_spec(256),
                  _const_spec((1, 256)), _const_spec((1, 256))],
        out_specs=[_row_spec(hq, n), _row_spec(hq, n), _row_spec(B_HEADS * B_V, n)],
        out_shape=[jax.ShapeDtypeStruct((bsz, NT, hq), BF16), jax.ShapeDtypeStruct((bsz, NT, hq), BF16),
                   jax.ShapeDtypeStruct((bsz, NT, B_HEADS * B_V), BF16)],
        compiler_params=_cparams(2), name="mla_project",
    )(x, modtab, g1, p["w1"], p["gql"], p["gkl"], p["gkr"], p["kcos"], p["ksin"], p["wuq"], p["wk"], p["wv"],
      p["bd"], p["qgain"], p["qinv"], p["qcos"], p["qsin"], p["kgain"], p["kmask"])


def _softmax_pv(sa, sb, v, shift=True):
    outs = []
    for s in (sa, sb):
        p = jnp.exp2(s - jnp.max(s, axis=-1, keepdims=True)) if shift else jnp.exp2(s)
        l = jnp.sum(p, axis=-1, keepdims=True)
        outs.append(_dot(p.astype(BF16), v) / l)
    lane = lax.broadcasted_iota(jnp.int32, (1, LANES), 1)
    return jnp.where(lane < 64, outs[0], outs[1])


def _split_heads(q):
    lane = lax.broadcasted_iota(jnp.int32, (1, LANES), 1)
    zero = jnp.zeros_like(q)
    return jnp.where(lane < 64, q, zero), jnp.where(lane >= 64, q, zero)


PAIRS_PER_STEP = 4


DENSE_PAIRS_PER_STEP = 8


SCORE_BOUND = 60.0


def _score_bound(q_gain2, k_gain2, extra=0.0):
    return (1.05 * jnp.sqrt(q_gain2 * k_gain2) + extra).reshape(1).astype(F32)


def _pair_attn_kernel(bound_ref, q_ref, k_ref, v_ref, o_ref, *, wide, q_pairs_per_kv):
    t = pl.program_id(2)

    def run(nkeys, shift):
        for i in range(DENSE_PAIRS_PER_STEP):
            kv = i // q_pairs_per_kv
            v = v_ref[0, :nkeys, LANES * kv:LANES * (kv + 1)]
            if wide:
                q = q_ref[0, :, 256 * i:256 * (i + 1)]
                k = k_ref[0, :nkeys, 256 * kv:256 * (kv + 1)]
                sa, sb = _dot_t(q[:, :LANES], k[:, :LANES]), _dot_t(q[:, LANES:], k[:, LANES:])
            else:
                qa, qb = _split_heads(q_ref[0, :, LANES * i:LANES * (i + 1)])
                k = k_ref[0, :nkeys, LANES * kv:LANES * (kv + 1)]
                sa, sb = _dot_t(qa, k), _dot_t(qb, k)
            o_ref[0, :, LANES * i:LANES * (i + 1)] = _softmax_pv(sa, sb, v, shift).astype(BF16)

    small = bound_ref[0] <= SCORE_BOUND

    @pl.when(t == 0)
    def _():
        run(CTX, True)

    @pl.when((t > 0) & small)
    def _():
        run(NT, False)

    @pl.when((t > 0) & jnp.logical_not(small))
    def _():
        run(NT, True)


def _pair_attention(bound, q, k, v, *, wide, q_pairs_per_kv):
    bsz = q.shape[0]
    qw = 256 if wide else LANES
    n_steps = q.shape[2] // (qw * DENSE_PAIRS_PER_STEP)
    kv_pairs = DENSE_PAIRS_PER_STEP // q_pairs_per_kv
    return pl.pallas_call(
        functools.partial(_pair_attn_kernel, wide=wide, q_pairs_per_kv=q_pairs_per_kv),
        grid=(bsz, n_steps, NTILES),
        in_specs=[pl.BlockSpec(memory_space=pltpu.SMEM),
                  pl.BlockSpec((1, TM, qw * DENSE_PAIRS_PER_STEP), lambda b, p, t: (b, t, p)),
                  pl.BlockSpec((1, NT, qw * kv_pairs), lambda b, p, t: (b, 0, p)),
                  pl.BlockSpec((1, NT, LANES * kv_pairs), lambda b, p, t: (b, 0, p))],
        out_specs=pl.BlockSpec((1, TM, LANES * DENSE_PAIRS_PER_STEP), lambda b, p, t: (b, t, p)),
        out_shape=jax.ShapeDtypeStruct((bsz, NT, n_steps * DENSE_PAIRS_PER_STEP * LANES), BF16),
        compiler_params=_cparams(3), name="pair_attention",
    )(bound, q, k, v)


def _na_attn_kernel(bound_ref, q_ref, kc_ref, k0_ref, k1_ref, k2_ref, vc_ref, v0_ref, v1_ref, v2_ref, bias_ref,
                    o_ref):
    t = pl.program_id(1)
    small = bound_ref[0] <= SCORE_BOUND

    @pl.when(t == 0)
    def _():
        for i in range(PAIRS_PER_STEP):
            sl = slice(LANES * i, LANES * (i + 1))
            qa, qb = _split_heads(q_ref[0, :, sl])
            k = kc_ref[0, :, sl]
            o_ref[0, :, sl] = _softmax_pv(_dot_t(qa, k), _dot_t(qb, k), vc_ref[0, :, sl]).astype(BF16)

    def run(shift):
        for i in range(PAIRS_PER_STEP):
            sl = slice(LANES * i, LANES * (i + 1))
            qa, qb = _split_heads(q_ref[0, :, sl])
            k = jnp.concatenate([k0_ref[0, :, sl], k1_ref[0, :, sl], k2_ref[0, :, sl], kc_ref[0, :, sl]], axis=0)
            v = jnp.concatenate([v0_ref[0, :, sl], v1_ref[0, :, sl], v2_ref[0, :, sl], vc_ref[0, :, sl]], axis=0)
            sa = _dot_t(qa, k) + bias_ref[0, 2 * i]
            sb = _dot_t(qb, k) + bias_ref[0, 2 * i + 1]
            o_ref[0, :, sl] = _softmax_pv(sa, sb, v, shift).astype(BF16)

    @pl.when((t > 0) & small)
    def _():
        run(False)

    @pl.when((t > 0) & jnp.logical_not(small))
    def _():
        run(True)


def _na_band_tile(t):
    return 1 + jnp.clip(t - 2, 0, LAT_TILES - BAND_TILES)


def _na_attention(bound, q, k, v, bias):
    bsz = q.shape[0]
    n_steps = C_HEADS // (2 * PAIRS_PER_STEP)
    width = LANES * PAIRS_PER_STEP
    qspec = pl.BlockSpec((1, TM, width), lambda p, t, b: (b, t, p))
    cspec = pl.BlockSpec((1, TM, width), lambda p, t, b: (b, 0, p))
    bands = [pl.BlockSpec((1, TM, width), functools.partial(lambda p, t, b, i: (b, _na_band_tile(t) + i, p), i=i))
             for i in range(BAND_TILES)]
    variant = lambda t: jnp.where(t <= 1, 0, jnp.where(t == LAT_TILES, 2, 1))
    return pl.pallas_call(
        _na_attn_kernel,
        grid=(n_steps, NTILES, bsz),
        in_specs=[pl.BlockSpec(memory_space=pltpu.SMEM), qspec, cspec] + bands + [cspec] + bands
                 + [pl.BlockSpec((1, 2 * PAIRS_PER_STEP, TM, (BAND_TILES + 1) * TM),
                                 lambda p, t, b: (variant(t), p, 0, 0))],
        out_specs=qspec,
        out_shape=jax.ShapeDtypeStruct((bsz, NT, C_HEADS * C_HEAD_DIM), BF16),
        compiler_params=_cparams(3), name="na_attention",
    )(bound, q, k, k, k, k, v, v, v, v, bias)


def _out_router_kernel(o_ref, wo_ref, x_ref, mod_ref, g2_ref, r_ref, xo_ref, h_ref, aff_ref):
    attn = _dot(_rows_of(o_ref), wo_ref[...])
    hs = []
    for s in range(x_ref.shape[0]):
        mod = mod_ref[s, 0]
        x = x_ref[s] + mod[2:3] * attn[s * TM:(s + 1) * TM]
        xo_ref[s] = x
        h = _prenorm(x, g2_ref[...], mod[3:4], mod[4:5]).astype(BF16)
        for r, word in enumerate(_pack_rows(h)):
            h_ref[s, pl.ds(r, TM, stride=HS_ROWS), :] = word
        hs.append(h)
    logits = _dot(jnp.concatenate(hs, axis=0), r_ref[...])
    lane = lax.broadcasted_iota(jnp.int32, (1, LANES), 1)
    logits = jnp.where(lane < N_EXPERTS, logits, NEG_INF)
    e = jnp.exp(logits - jnp.max(logits, axis=-1, keepdims=True))
    _store_rows(aff_ref, slice(None), e / jnp.sum(e, axis=-1, keepdims=True))


def _out_router(o, wo, x, modtab, g2, router):
    bsz = x.shape[0]
    n = _samples_per_step(bsz)
    return pl.pallas_call(
        _out_router_kernel,
        grid=(bsz // n, NTILES),
        in_specs=[_row_spec(D, n), _const_spec((D, D)), _row_spec(D, n), _mod_spec(n), _const_spec((1, D)),
                  _const_spec((D, LANES))],
        out_specs=[_row_spec(D, n), pl.BlockSpec((n, TM * HS_ROWS, LANES), lambda b, t: (b, t, 0)),
                   _row_spec(LANES, n)],
        out_shape=[jax.ShapeDtypeStruct((bsz, NT, D), F32),
                   jax.ShapeDtypeStruct((bsz, NT * HS_ROWS, LANES), jnp.int32),
                   jax.ShapeDtypeStruct((bsz, NT, LANES), F32)],
        input_output_aliases={2: 0},
        compiler_params=_cparams(2), name="out_router",
    )(o, wo, x, modtab, g2, router)


def _excl_cumsum_lanes(m, triu):
    out = []
    offset = jnp.zeros((m.shape[0], 1), F32)
    for i in range(m.shape[1] // TM):
        blk = m[:, i * TM:(i + 1) * TM]
        out.append(_dot(blk.astype(BF16), triu) + offset)
        offset = offset + jnp.sum(blk, axis=1, keepdims=True)
    return jnp.concatenate(out, axis=1) if len(out) > 1 else out[0]


def _kth_largest_bits(bits_list, caps):
    def count(bits, cand):
        return jnp.sum(jnp.where(bits >= cand, 1.0, 0.0), axis=1, keepdims=True)

    def step(i, thrs):
        b1 = jnp.left_shift(jnp.int32(1), 29 - 2 * i)
        b0 = jnp.left_shift(jnp.int32(1), 28 - 2 * i)
        out = []
        for bits, cap, thr in zip(bits_list, caps, thrs):
            c1, c2, c3 = count(bits, thr | b1), count(bits, thr | b0), count(bits, thr | b1 | b0)
            with_b1 = jnp.where(c3 >= cap, thr | b1 | b0, thr | b1)
            without = jnp.where(c2 >= cap, thr | b0, thr)
            out.append(jnp.where(c1 >= cap, with_b1, without))
        return tuple(out)

    top = jnp.int32(1 << 30)
    init = tuple(jnp.where(count(bits, top) >= cap, top, jnp.zeros((N_EXPERTS, 1), jnp.int32))
                 for bits, cap in zip(bits_list, caps))
    return lax.fori_loop(0, 15, step, init)


def _topk_slots(bits, thr, cap, triu):
    gt = bits > thr
    eq = bits == thr
    need = cap - jnp.sum(jnp.where(gt, 1.0, 0.0), axis=1, keepdims=True)
    eq_rank = _excl_cumsum_lanes(jnp.where(eq, 1.0, 0.0), triu)
    sel = gt | (eq & (eq_rank < need))
    rank = _excl_cumsum_lanes(jnp.where(sel, 1.0, 0.0), triu)
    return jnp.where(sel, rank, -1.0), rank


OFF_LANE = 3 * N_EXPERTS


def _token_values(aff, first_row):
    n = aff.shape[0]
    hi = aff.astype(BF16).astype(F32)
    r1 = aff - hi
    mid = r1.astype(BF16).astype(F32)
    lo = (r1 - mid).astype(BF16).astype(F32)
    off = (lax.broadcasted_iota(jnp.int32, (n, 1), 0) + first_row) * HS_ROWS
    lane = lax.broadcasted_iota(jnp.int32, (1, LANES), 1)
    vals = (hi + pltpu.roll(mid, N_EXPERTS, axis=1) + pltpu.roll(lo, 2 * N_EXPERTS, axis=1)
            + jnp.where(lane == OFF_LANE, (off >> 7).astype(F32), 0.0)
            + jnp.where(lane == OFF_LANE + 1, (off & (LANES - 1)).astype(F32), 0.0))
    return vals.astype(BF16)


def _topk_kernel(aff_ref, slot_ref, off_ref, gate_ref, win_ref):
    r = lax.broadcasted_iota(jnp.int32, (TM, TM), 0)
    c = lax.broadcasted_iota(jnp.int32, (TM, TM), 1)
    triu = jnp.where(r < c, 1.0, 0.0).astype(BF16)
    lane = lax.broadcasted_iota(jnp.int32, (1, LANES), 1)
    pad_rows = lambda x, fill: jnp.concatenate(
        [x, jnp.full((LANES - x.shape[0], x.shape[1]), fill, x.dtype)], axis=0) if x.shape[0] < LANES else x
    aff_t = aff_ref[0].T[:N_EXPERTS]
    segments = ((0, CTX, CAP_CTX, CAP_LAT), (CTX, NT, CAP_LAT, 0))
    bits = [pltpu.bitcast(aff_t[:, lo:hi], jnp.int32) for lo, hi, _, _ in segments]
    thrs = _kth_largest_bits(bits, [cap for _, _, cap, _ in segments])
    for (lo, hi, cap, dst), seg_bits, thr in zip(segments, bits, thrs):
        slot, rank = _topk_slots(seg_bits, thr, cap, triu)
        slot_ref[0, lo:hi] = pad_rows(slot, -1.0).T.astype(jnp.int32)
        if lo == CTX:
            starts = jnp.where(lane == LAT_TILES, float(cap), 0.0)
            for j in range(LAT_TILES):
                starts = starts + jnp.where(lane == j, rank[:, TM * j:TM * j + 1], 0.0)
            run = pad_rows(starts, 0.0).T.astype(jnp.int32)
            first = jnp.minimum((run[:LAT_TILES] >> 4) << 4, CAP_LAT - WIN)
            ends = pltpu.roll(run[1:LAT_TILES + 1], N_EXPERTS, axis=1)
            win = jnp.where(lane < N_EXPERTS, first, ends)
            for j in range(LAT_TILES):
                win_ref[0, j] = win[j:j + 1]
        vals = _token_values(aff_ref[0, lo:hi], lo)
        slot_ids = lax.broadcasted_iota(jnp.int32, (cap, 1), 0).astype(F32)
        for e in range(N_EXPERTS):
            hit = jnp.where(slot_ids == slot[e:e + 1, :], 1.0, 0.0).astype(BF16)
            picked = _dot(hit, vals)
            mine = ((lane & (N_EXPERTS - 1)) == e) & (lane < OFF_LANE)
            gate_ref[0, e, dst:dst + cap, :] = jnp.sum(jnp.where(mine, picked, 0.0), axis=1, keepdims=True)
            by_lane = pad_rows(picked, 0.0).T
            off = by_lane[OFF_LANE:OFF_LANE + 1] * float(LANES) + by_lane[OFF_LANE + 1:OFF_LANE + 2]
            off_ref[0, e, :, dst:dst + cap] = off[:, :cap].astype(jnp.int32)


def _topk(aff):
    bsz = aff.shape[0]
    tspec = pl.BlockSpec((1, NT, LANES), lambda b: (b, 0, 0))
    return pl.pallas_call(
        _topk_kernel, grid=(bsz,),
        in_specs=[tspec],
        out_specs=[tspec, pl.BlockSpec((1, N_EXPERTS, 1, CAP_ALL), lambda b: (b, 0, 0, 0)),
                   pl.BlockSpec((1, N_EXPERTS, CAP_ALL, 1), lambda b: (b, 0, 0, 0)),
                   pl.BlockSpec((1, LAT_TILES, 1, LANES), lambda b: (b, 0, 0, 0))],
        out_shape=[jax.ShapeDtypeStruct((bsz, NT, LANES), jnp.int32),
                   jax.ShapeDtypeStruct((bsz, N_EXPERTS, 1, CAP_ALL), jnp.int32),
                   jax.ShapeDtypeStruct((bsz, N_EXPERTS, CAP_ALL, 1), F32),
                   jax.ShapeDtypeStruct((bsz, LAT_TILES, 1, LANES), jnp.int32)],
        compiler_params=_cparams(1), name="expert_topk",
    )(aff)


def _pack_rows(hb):
    words = []
    for r in range(HS_ROWS):
        lo = pltpu.bitcast(hb[:, 256 * r:256 * r + LANES].astype(F32), jnp.int32)
        hi = pltpu.bitcast(hb[:, 256 * r + LANES:256 * (r + 1)].astype(F32), jnp.int32)
        words.append(lax.shift_right_logical(lo, jnp.int32(16)) | hi)
    return words


def _unpack_rows(words):
    cols = []
    for w in words:
        cols.append(pltpu.bitcast(lax.shift_left(w, jnp.int32(16)), F32).astype(BF16))
        cols.append(pltpu.bitcast(w & jnp.int32(-65536), F32).astype(BF16))
    return jnp.concatenate(cols, axis=1)


def _expert_kernel(hs_ref, off_ref, gate_ref, wg_ref, wu_ref, wd_ref, *refs, n_slots):
    out_refs, (tile_ref, wg_s, wu_s, wd_s) = refs[:-4], refs[-4:]

    @pl.when(pl.program_id(1) == 0)
    def _():
        wg_s[...] = wg_ref[0, 0].astype(BF16)
        wu_s[...] = wu_ref[0, 0].astype(BF16)
        wd_s[...] = wd_ref[0, 0].astype(BF16)

    for j in range(n_slots):
        src = pl.multiple_of(off_ref[0, 0, 0, j], HS_ROWS)
        tile_ref[pl.ds(j, HS_ROWS, stride=TILE_STRIDE), :] = hs_ref[0, pl.ds(src, HS_ROWS), :]
    xg = _unpack_rows([tile_ref[r * TILE_STRIDE:r * TILE_STRIDE + n_slots, :] for r in range(HS_ROWS)])
    hid = (_silu(_dot(xg, wg_s[...])) * _dot(xg, wu_s[...])).astype(BF16)
    y = _dot(hid, wd_s[...]) * gate_ref[0, 0, :n_slots]
    out_refs[0][0] = y[:CAP_LAT].astype(BF16)
    if n_slots > CAP_LAT:
        out_refs[1][0] = y[CAP_LAT:].astype(BF16)


def _experts(hs, off, gate, wg, wu, wd, layer, with_ctx):
    bsz = hs.shape[0]
    n_slots = CAP_ALL if with_ctx else CAP_LAT
    wspec = pl.BlockSpec((1, 1, D, EXPERT_FF), lambda e, b: (layer, e, 0, 0))
    out_specs = [pl.BlockSpec((1, CAP_LAT, D), lambda e, b: (b, e, 0))]
    out_shape = [jax.ShapeDtypeStruct((bsz, N_EXPERTS * CAP_LAT, D), BF16)]
    if with_ctx:
        out_specs.append(pl.BlockSpec((1, CAP_CTX, D), lambda e, b: (b, e, 0)))
        out_shape.append(jax.ShapeDtypeStruct((bsz, N_EXPERTS * CAP_CTX, D), BF16))
    return pl.pallas_call(
        functools.partial(_expert_kernel, n_slots=n_slots),
        grid=(N_EXPERTS, bsz),
        in_specs=[pl.BlockSpec((1, NT * HS_ROWS, LANES), lambda e, b: (b, 0, 0)),
                  pl.BlockSpec((1, 1, 1, CAP_ALL), lambda e, b: (b, e, 0, 0), memory_space=pltpu.SMEM),
                  pl.BlockSpec((1, 1, CAP_ALL, 1), lambda e, b: (b, e, 0, 0)),
                  wspec, wspec, pl.BlockSpec((1, 1, EXPERT_FF, D), lambda e, b: (layer, e, 0, 0))],
        out_specs=out_specs, out_shape=out_shape,
        scratch_shapes=[pltpu.VMEM((HS_ROWS * TILE_STRIDE, LANES), jnp.int32),
                        pltpu.VMEM((D, EXPERT_FF), BF16), pltpu.VMEM((D, EXPERT_FF), BF16),
                        pltpu.VMEM((EXPERT_FF, D), BF16)],
        compiler_params=_cparams(2), name="experts",
    )(hs, off, gate, wg, wu, wd)


WIN = 64
WIN_GROUP = 256 // WIN


def _combine_lat(slot_ref, yl_ref, win_ref, x_ref, gate, xo_ref):
    slot = slot_ref[0]
    fits = win_ref[0, 0, 0, N_EXPERTS] - win_ref[0, 0, 0, 0] <= WIN
    for e in range(1, N_EXPERTS):
        fits = jnp.logical_and(fits, win_ref[0, 0, 0, N_EXPERTS + e] - win_ref[0, 0, 0, e] <= WIN)

    @pl.when(fits)
    def _():
        cols = lax.broadcasted_iota(jnp.int32, (TM, WIN * WIN_GROUP), 1)
        acc = None
        for g in range(N_EXPERTS // WIN_GROUP):
            hit = cols < 0
            rows = []
            for j in range(WIN_GROUP):
                e = WIN_GROUP * g + j
                start = win_ref[0, 0, 0, e]
                s = slot[:, e:e + 1]
                hit = hit | (jnp.where(s >= 0, s - start + WIN * j, -1) == cols)
                rows.append(yl_ref[0, pl.ds(pl.multiple_of(e * CAP_LAT + start, 16), WIN), :])
            part = _dot(jnp.where(hit, 1.0, 0.0).astype(BF16), jnp.concatenate(rows, axis=0))
            acc = part if acc is None else acc + part
        xo_ref[0] = x_ref[0] + gate * acc

    @pl.when(jnp.logical_not(fits))
    def _():
        cols = lax.broadcasted_iota(jnp.int32, (TM, CAP_LAT), 1)
        onehot = jnp.concatenate(
            [jnp.where(slot[:, e:e + 1] == cols, 1.0, 0.0).astype(BF16) for e in range(N_EXPERTS)], axis=1)
        xo_ref[0] = x_ref[0] + gate * _dot(onehot, yl_ref[0])


def _combine_kernel(slot_ref, yl_ref, yc_ref, win_ref, x_ref, mod_ref, xo_ref):
    t = pl.program_id(1)
    gate = mod_ref[0, 0][5:6]

    @pl.when(t == 0)
    def _():
        slot = slot_ref[0]
        cols = lax.broadcasted_iota(jnp.int32, (TM, N_EXPERTS * CAP_CTX), 1)
        hit = cols < 0
        for e in range(N_EXPERTS):
            s = slot[:, e:e + 1]
            hit = hit | (jnp.where(s >= 0, s + e * CAP_CTX, -1) == cols)
        xo_ref[0] = x_ref[0] + gate * _dot(jnp.where(hit, 1.0, 0.0).astype(BF16), yc_ref[0])

    @pl.when(t > 0)
    def _():
        _combine_lat(slot_ref, yl_ref, win_ref, x_ref, gate, xo_ref)


def _win_spec(tile_of_step):
    return pl.BlockSpec((1, 1, 1, LANES), lambda b, t: (b, tile_of_step(t), 0, 0), memory_space=pltpu.SMEM)


def _combine(slot, yl, yc, win, x, modtab):
    bsz = x.shape[0]
    return pl.pallas_call(
        _combine_kernel,
        grid=(bsz, NTILES),
        in_specs=[_row_spec(LANES), pl.BlockSpec((1, N_EXPERTS * CAP_LAT, D), lambda b, t: (b, 0, 0)),
                  pl.BlockSpec((1, N_EXPERTS * CAP_CTX, D), lambda b, t: (b, 0, 0)),
                  _win_spec(lambda t: jnp.maximum(t - 1, 0)), _row_spec(D), _mod_spec()],
        out_specs=_row_spec(D),
        out_shape=jax.ShapeDtypeStruct((bsz, NT, D), F32),
        input_output_aliases={4: 0},
        compiler_params=_cparams(2), name="moe_combine",
    )(slot, yl, yc, win, x, modtab)


def _combine_last_kernel(slot_ref, yl_ref, win_ref, x_ref, mod_ref, xo_ref):
    _combine_lat(slot_ref, yl_ref, win_ref, x_ref, mod_ref[0, 0][5:6], xo_ref)


def _combine_last(slot, yl, win, x, modtab):
    bsz = x.shape[0]
    lat = lambda width: pl.BlockSpec((1, TM, width), lambda b, t: (b, t + 1, 0))
    return pl.pallas_call(
        _combine_last_kernel,
        grid=(bsz, LAT_TILES),
        in_specs=[lat(LANES), pl.BlockSpec((1, N_EXPERTS * CAP_LAT, D), lambda b, t: (b, 0, 0)),
                  _win_spec(lambda t: t), lat(D), pl.BlockSpec((1, 1, 6, D), lambda b, t: (b, 1, 0, 0))],
        out_specs=pl.BlockSpec((1, TM, D), lambda b, t: (b, t, 0)),
        out_shape=jax.ShapeDtypeStruct((bsz, SEQ, D), F32),
        compiler_params=_cparams(2), name="moe_combine_last",
    )(slot, yl, win, x, modtab)


def _rope_tables(rot_dim, width, lane_lo, lane_hi):
    n_freq = rot_dim // 4
    half = rot_dim // 2
    inv = jnp.float32(ROPE_THETA) ** (-jnp.arange(n_freq, dtype=F32) / n_freq)
    t = jnp.arange(SEQ, dtype=jnp.int32)
    row = (t // GRID_W).astype(F32)
    col = (t % GRID_W).astype(F32)
    ang = jnp.concatenate([row[:, None] * inv, col[:, None] * inv], axis=-1)
    cos, sin = jnp.cos(ang), jnp.sin(ang)
    lane = np.arange(width)
    inside = (lane % LANES >= lane_lo) & (lane % LANES < lane_hi)
    idx = lane % half
    sign = np.where(lane % rot_dim < half, -1.0, 1.0).astype(np.float32)
    cosw = jnp.where(inside[None, :], cos[:, idx], 1.0)
    sinw = jnp.where(inside[None, :], sin[:, idx] * sign[None, :], 0.0)
    ones = jnp.ones((CTX, width), F32)
    return jnp.concatenate([ones, cosw], axis=0), jnp.concatenate([0.0 * ones, sinw], axis=0)


def _block_diag_ones():
    i = np.arange(256)
    return jnp.asarray((i[:, None] // 64) == (i[None, :] // 64), dtype=BF16)


_GQA_ORDER = np.array([8 * kp + 4 * odd + i for kp in range(2) for i in range(4) for odd in range(2)])


def _na_bias(rpb):
    out = []
    rows = SEQ // GRID_W
    qrows, krows = TM // GRID_W, BAND_TILES * TM // GRID_W
    col = np.arange(GRID_W)
    cs = np.clip(col - NA_COLS // 2, 0, GRID_W - NA_COLS)
    ok_c = (col[None, :] >= cs[:, None]) & (col[None, :] < cs[:, None] + NA_COLS)
    dc = np.clip(col[None, :] - col[:, None] + NA_COLS - 1, 0, 2 * NA_COLS - 2)
    pick_c = (dc[:, :, None] == np.arange(2 * NA_COLS - 1)).astype(np.float32)
    for r0, bs in ((0, 0), (8, 4), (rows - 4, rows - 12)):
        r, kr = r0 + np.arange(qrows), bs + np.arange(krows)
        rs = np.clip(r - NA_ROWS // 2, 0, rows - NA_ROWS)
        ok_r = (kr[None, :] >= rs[:, None]) & (kr[None, :] < rs[:, None] + NA_ROWS)
        dr = kr[None, :] - r[:, None] + NA_ROWS - 1
        pick_r = (dr[:, :, None] == np.arange(2 * NA_ROWS - 1)).astype(np.float32)
        t1 = jnp.einsum('ard,hde->hare', pick_r, rpb.astype(F32), precision=lax.Precision.HIGHEST)
        band = jnp.einsum('hare,cse->hacrs', t1, pick_c, precision=lax.Precision.HIGHEST)
        ok = ok_r[:, None, :, None] & ok_c[None, :, None, :]
        band = jnp.where(ok[None], band * LOG2E, NEG_INF).reshape(C_HEADS, TM, BAND_TILES * TM)
        out.append(jnp.concatenate([band, jnp.zeros((C_HEADS, TM, TM), F32)], axis=-1))
    return jnp.stack(out)


def _gqa_params(w_qkv, qn, kn, wo):
    nq, nk = A_HEADS * A_HEAD_DIM, A_KV_HEADS * A_HEAD_DIM
    wq = w_qkv[:, :nq].reshape(D, A_HEADS, A_HEAD_DIM)[:, _GQA_ORDER].reshape(D, nq)
    w = jnp.concatenate([wq, w_qkv[:, nq:]], axis=1).astype(BF16)
    wo_p = wo.reshape(A_HEADS, A_HEAD_DIM, D)[_GQA_ORDER].reshape(nq, D).astype(BF16)
    scale = A_HEAD_DIM ** -0.5 * LOG2E
    return w, (jnp.tile(qn, 4) * scale)[None], jnp.tile(kn, 4)[None], wo_p


def _mla_params(w_dq, qn_lat, w_uq, w_dkv, kvn_lat, w_ukv, qn, kn, cossin_q, cossin_k):
    qd = B_NOPE + B_ROPE
    w1 = jnp.concatenate([w_dq, w_dkv, jnp.zeros((D, 768 - B_Q_RANK - B_KV_RANK - B_ROPE), F32)], axis=1)
    pad_q = jnp.zeros((B_Q_RANK, B_HEADS, LANES - qd), F32)
    wuq = jnp.concatenate([w_uq.reshape(B_Q_RANK, B_HEADS, qd), pad_q], axis=2).reshape(B_Q_RANK, B_HEADS * LANES)
    ukv = w_ukv.reshape(B_KV_RANK, B_HEADS, B_NOPE + B_V)
    wk_top = jnp.concatenate([ukv[:, :, :B_NOPE], jnp.zeros((B_KV_RANK, B_HEADS, LANES - B_NOPE), F32)], axis=2)
    route = np.zeros((LANES, B_HEADS, LANES), np.float32)
    route[np.arange(B_ROPE), :, B_NOPE + np.arange(B_ROPE)] = 1.0
    wk = jnp.concatenate([wk_top.reshape(B_KV_RANK, -1), jnp.asarray(route).reshape(LANES, -1)], axis=0)
    wv = ukv[:, :, B_NOPE:].reshape(B_KV_RANK, B_HEADS * B_V)
    scale = qd ** -0.5 * LOG2E
    zpad = jnp.zeros((LANES - qd,), F32)
    qgain = jnp.tile(jnp.concatenate([qn * scale, zpad]), 2)[None]
    qinv = jnp.tile(jnp.concatenate([jnp.full((B_NOPE,), 1.0 / B_NOPE), jnp.full((LANES - B_NOPE,), 1.0 / B_ROPE)]), 2)
    kgain = jnp.tile(jnp.concatenate([kn[:B_NOPE], jnp.zeros((LANES - B_NOPE,), F32)]), 2)[None]
    kmask = jnp.tile(jnp.concatenate([jnp.ones((B_NOPE,), F32), jnp.zeros((LANES - B_NOPE,), F32)]), 2)[None]
    gkr = jnp.concatenate([kn[B_NOPE:], jnp.zeros((LANES - B_ROPE,), F32)])[None]
    return dict(w1=w1.astype(BF16), gql=qn_lat[None], gkl=kvn_lat[None], gkr=gkr, kcos=cossin_k[0], ksin=cossin_k[1],
                wuq=wuq.astype(BF16), wk=wk.astype(BF16), wv=wv.astype(BF16), bd=_block_diag_ones(),
                qgain=qgain, qinv=qinv[None].astype(F32), qcos=cossin_q[0], qsin=cossin_q[1], kgain=kgain, kmask=kmask)


def kernel(x, c, ctx, c_ctx, ada_w, ada_b, norm1_w, norm2_w, a_wqkv, a_qnorm, a_knorm, a_wo, b_wdq, b_qnorm_lat, b_wuq, b_wdkv, b_kvnorm_lat, b_wukv, b_qnorm, b_knorm, b_wo, c_wqkv, c_qnorm, c_knorm, c_rpb, c_wo, moe_router, moe_wg, moe_wu, moe_wd):
    bsz = x.shape[0]
    assert x.shape[1:] == (SEQ, D) and ctx.shape[1:] == (CTX, D)
    mod_rows = -(-(bsz + 1) // 16) * 16
    cvec = jnp.concatenate([c, c_ctx[None], jnp.zeros((mod_rows - bsz - 1, D), F32)], axis=0)
    mods = _ada_all(cvec, ada_w, ada_b)
    xs = jnp.concatenate([ctx, x], axis=1)

    bd = _block_diag_ones()
    rope64 = _rope_tables(A_HEAD_DIM, 256, 0, LANES)
    rope_mla_q = _rope_tables(B_ROPE, 256, B_NOPE, B_NOPE + B_ROPE)
    rope_mla_k = _rope_tables(B_ROPE, LANES, 0, B_ROPE)
    router = jnp.pad(moe_router, ((0, 0), (0, 0), (0, LANES - N_EXPERTS))).astype(BF16)

    for i in range(DEPTH):
        last = i == DEPTH - 1
        kind, j = i % 3, i // 3
        m = mods[i]
        m_lat = m[:bsz].reshape(bsz, 6, D)
        m_ctx = jnp.broadcast_to(m[bsz].reshape(1, 6, D), (bsz, 6, D))
        modtab = jnp.stack([m_ctx, m_lat], axis=1)
        g1, g2 = norm1_w[i][None], norm2_w[i][None]

        if kind == 0:
            w, gq, gk, wo = _gqa_params(a_wqkv[j], a_qnorm[j], a_knorm[j], a_wo[j])
            q, k, v = _qkv_project(xs, modtab, g1, w, bd, gq, gk, rope64[0], rope64[1],
                                   A_HEADS * A_HEAD_DIM, A_KV_HEADS * A_HEAD_DIM, True)
            bound = _score_bound(A_HEAD_DIM * jnp.max(gq * gq), A_HEAD_DIM * jnp.max(gk * gk))
            o = _pair_attention(bound, q, k, v, wide=False, q_pairs_per_kv=4)
        elif kind == 1:
            p = _mla_params(b_wdq[j], b_qnorm_lat[j], b_wuq[j], b_wdkv[j], b_kvnorm_lat[j], b_wukv[j],
                            b_qnorm[j], b_knorm[j], rope_mla_q, rope_mla_k)
            q, k, v = _mla_project(xs, modtab, g1, p)
            sq_max = lambda g: jnp.max(g * g)
            qg, kg = p["qgain"][0, :LANES], b_knorm[j]
            bound = _score_bound(B_NOPE * sq_max(qg[:B_NOPE]) + B_ROPE * sq_max(qg[B_NOPE:]),
                                 B_NOPE * sq_max(kg[:B_NOPE]) + B_ROPE * sq_max(kg[B_NOPE:]))
            o = _pair_attention(bound, q, k, v, wide=True, q_pairs_per_kv=1)
            wo = b_wo[j].astype(BF16)
        else:
            hd = C_HEADS * C_HEAD_DIM
            gq = (jnp.tile(c_qnorm[j], 4) * (C_HEAD_DIM ** -0.5 * LOG2E))[None]
            gk = jnp.tile(c_knorm[j], 4)[None]
            q, k, v = _qkv_project(xs, modtab, g1, c_wqkv[j].astype(BF16), bd, gq, gk, rope64[0], rope64[1],
                                   hd, hd, False)
            bound = _score_bound(C_HEAD_DIM * jnp.max(gq * gq), C_HEAD_DIM * jnp.max(gk * gk),
                                 extra=LOG2E * jnp.max(jnp.abs(c_rpb[j])))
            o = _na_attention(bound, q, k, v, _na_bias(c_rpb[j]))
            wo = c_wo[j].astype(BF16)

        xs, hs, aff = _out_router(o, wo, xs, modtab, g2, router[i])
        slot, off, gate, win = _topk(aff)
        ys = _experts(hs, off, gate,
                      moe_wg, moe_wu, moe_wd, layer=i, with_ctx=not last)
        if last:
            return _combine_last(slot, ys[0], win, xs, modtab)
        xs = _combine(slot, ys[0], ys[1], win, xs, modtab)
```

```python
import functools

import numpy as np
import jax
import jax.numpy as jnp
from jax import lax
from jax.experimental import pallas as pl
from jax.experimental.pallas import tpu as pltpu

F32 = jnp.float32
BF16 = jnp.bfloat16

D = 1024
SEQ = 2048
CTX = 256
NT = CTX + SEQ
DEPTH = 4
GRID_W = 64
ROPE_THETA = 10000.0
EPS = 1e-6
NEG_INF = -1e30
LOG2E = 1.4426950408889634

TM = 256
NTILES = NT // TM
LAT_TILES = SEQ // TM

A_HEADS, A_KV_HEADS, A_HEAD_DIM = 16, 4, 64
B_HEADS, B_Q_RANK, B_KV_RANK, B_NOPE, B_ROPE, B_V = 16, 384, 256, 64, 32, 64
C_HEADS, C_HEAD_DIM, NA_ROWS, NA_COLS = 16, 64, 8, 16
N_EXPERTS, EXPERT_FF = 16, 1024
CAP_LAT = 2 * SEQ // N_EXPERTS
CAP_CTX = 2 * CTX // N_EXPERTS
CAP_ALL = CAP_LAT + CAP_CTX
HS_ROWS = D // 256
TILE_STRIDE = 296
LANES = 128
BAND_TILES = 3

VMEM_LIMIT = 56 * 1024 * 1024


def _cparams(n_axes):
    return pltpu.CompilerParams(dimension_semantics=("arbitrary",) * n_axes, vmem_limit_bytes=VMEM_LIMIT)


def _dot(a, b):
    return jnp.dot(a, b, preferred_element_type=F32)


def _dot_t(a, b):
    return lax.dot_general(a, b, (((1,), (1,)), ((), ())), preferred_element_type=F32)


def _silu(x):
    return x / (1.0 + jnp.exp(-x))


def _prenorm(x, gain, shift, scale):
    ms = jnp.mean(x * x, axis=-1, keepdims=True)
    return (x * lax.rsqrt(ms + EPS) * gain) * (1.0 + scale) + shift


def _seg_rms(z, bd, inv_n):
    ss = _dot((z * z).astype(BF16), bd)
    return lax.rsqrt(ss * inv_n + EPS)


def _rope(z, cosw, sinw, first_half, half):
    w = z.shape[-1]
    up = pltpu.roll(z, w - half, axis=1)
    dn = pltpu.roll(z, half, axis=1)
    return z * cosw + jnp.where(first_half, up, dn) * sinw


def _ada_kernel(c_ref, w_ref, b_ref, o_ref):
    s = _silu(c_ref[...]).astype(BF16)
    o_ref[0] = _dot(s, w_ref[0].astype(BF16)) + b_ref[0]


def _ada_all(cvec, ada_w, ada_b):
    rows = cvec.shape[0]
    tn = 512
    return pl.pallas_call(
        _ada_kernel,
        grid=(DEPTH, 6 * D // tn),
        in_specs=[pl.BlockSpec((rows, D), lambda i, j: (0, 0)),
                  pl.BlockSpec((1, D, tn), lambda i, j: (i, 0, j)),
                  pl.BlockSpec((1, 1, tn), lambda i, j: (i, 0, j))],
        out_specs=pl.BlockSpec((1, rows, tn), lambda i, j: (i, 0, j)),
        out_shape=jax.ShapeDtypeStruct((DEPTH, rows, 6 * D), F32),
        compiler_params=_cparams(2), name="adaln",
    )(cvec, ada_w, ada_b.reshape(DEPTH, 1, 6 * D))


def _samples_per_step(bsz):
    return 2 if bsz % 2 == 0 else 1


def _mod_spec(n=1):
    return pl.BlockSpec((n, 1, 6, D), lambda b, t: (b, jnp.minimum(t, 1), 0, 0))


def _row_spec(width, n=1):
    return pl.BlockSpec((n, TM, width), lambda b, t: (b, t, 0))


def _rows_of(ref, prep=lambda s, x: x):
    return jnp.concatenate([prep(s, ref[s]) for s in range(ref.shape[0])], axis=0)


def _store_rows(ref, cols, val):
    for s in range(ref.shape[0]):
        ref[s, :, cols] = val[s * TM:(s + 1) * TM]


def _const_spec(shape):
    return pl.BlockSpec(shape, lambda b, t: (0,) * len(shape))


def _tab_spec(width):
    return pl.BlockSpec((TM, width), lambda b, t: (t, 0))


def _qkv_kernel(x_ref, mod_ref, g1_ref, w_ref, bd_ref, gq_ref, gk_ref, cos_ref, sin_ref,
                q_ref, k_ref, v_ref, *, nq, nk, rope):
    n = x_ref.shape[0]
    g1 = g1_ref[...]
    h = _rows_of(x_ref, lambda s, x: _prenorm(x, g1, mod_ref[s, 0][0:1], mod_ref[s, 0][1:2])).astype(BF16)
    acc = _dot(h, w_ref[...])
    bd = bd_ref[...]
    lane = lax.broadcasted_iota(jnp.int32, (1, 256), 1)
    first_half = (lane % 64) < 32
    cosw, sinw = jnp.concatenate([cos_ref[...]] * n, axis=0), jnp.concatenate([sin_ref[...]] * n, axis=0)

    def finish(z, gain):
        z = z * _seg_rms(z, bd, 1.0 / 64) * gain
        if rope:
            z = _rope(z, cosw, sinw, first_half, 32)
        return z.astype(BF16)

    for j in range(nq // 256):
        cols = slice(j * 256, (j + 1) * 256)
        _store_rows(q_ref, cols, finish(acc[:, cols], gq_ref[...]))
    for j in range(nk // 256):
        cols = slice(j * 256, (j + 1) * 256)
        _store_rows(k_ref, cols, finish(acc[:, nq + j * 256:nq + (j + 1) * 256], gk_ref[...]))
    _store_rows(v_ref, slice(None), acc[:, nq + nk:].astype(BF16))


def _qkv_project(x, modtab, g1, w, bd, gq, gk, cosw, sinw, nq, nk, rope):
    bsz = x.shape[0]
    n = _samples_per_step(bsz)
    return pl.pallas_call(
        functools.partial(_qkv_kernel, nq=nq, nk=nk, rope=rope),
        grid=(bsz // n, NTILES),
        in_specs=[_row_spec(D, n), _mod_spec(n), _const_spec((1, D)), _const_spec((D, nq + 2 * nk)),
                  _const_spec((256, 256)), _const_spec((1, 256)), _const_spec((1, 256)),
                  _tab_spec(256), _tab_spec(256)],
        out_specs=[_row_spec(nq, n), _row_spec(nk, n), _row_spec(nk, n)],
        out_shape=[jax.ShapeDtypeStruct((bsz, NT, nq), BF16), jax.ShapeDtypeStruct((bsz, NT, nk), BF16),
                   jax.ShapeDtypeStruct((bsz, NT, nk), BF16)],
        compiler_params=_cparams(2), name="qkv_project",
    )(x, modtab, g1, w, bd, gq, gk, cosw, sinw)


def _mla_kernel(x_ref, mod_ref, g1_ref, w1_ref, gql_ref, gkl_ref, gkr_ref, kcos_ref, ksin_ref,
                wuq_ref, wk_ref, wv_ref, bd_ref, qgain_ref, qinv_ref, qcos_ref, qsin_ref, kgain_ref, kmask_ref,
                q_ref, k_ref, v_ref):
    n = x_ref.shape[0]
    g1 = g1_ref[...]
    tile = lambda ref: jnp.concatenate([ref[...]] * n, axis=0)
    h = _rows_of(x_ref, lambda s, x: _prenorm(x, g1, mod_ref[s, 0][0:1], mod_ref[s, 0][1:2])).astype(BF16)
    a = _dot(h, w1_ref[...])
    cq = a[:, :B_Q_RANK]
    cq = (cq * lax.rsqrt(jnp.mean(cq * cq, axis=-1, keepdims=True) + EPS) * gql_ref[...]).astype(BF16)
    ckv = a[:, B_Q_RANK:B_Q_RANK + B_KV_RANK]
    ckv = (ckv * lax.rsqrt(jnp.mean(ckv * ckv, axis=-1, keepdims=True) + EPS) * gkl_ref[...]).astype(BF16)
    kr = a[:, B_Q_RANK + B_KV_RANK:]
    kr = kr * lax.rsqrt(jnp.sum(kr * kr, axis=-1, keepdims=True) * (1.0 / B_ROPE) + EPS) * gkr_ref[...]
    lane128 = lax.broadcasted_iota(jnp.int32, (1, LANES), 1)
    kr = _rope(kr, tile(kcos_ref), tile(ksin_ref), (lane128 % 32) < 16, 16).astype(BF16)

    bd = bd_ref[...]
    lane = lax.broadcasted_iota(jnp.int32, (1, 256), 1)
    first_half = (lane % 32) < 16
    qcos, qsin = tile(qcos_ref), tile(qsin_ref)
    q2 = _dot(cq, wuq_ref[...])
    k2 = _dot(jnp.concatenate([ckv, kr], axis=1), wk_ref[...])
    kmask = kmask_ref[...] > 0.0
    for j in range(q2.shape[1] // 256):
        sl = slice(j * 256, (j + 1) * 256)
        z = q2[:, sl]
        z = z * _seg_rms(z, bd, qinv_ref[...]) * qgain_ref[...]
        _store_rows(q_ref, sl, _rope(z, qcos, qsin, first_half, 16).astype(BF16))
        z = k2[:, sl]
        _store_rows(k_ref, sl, (z * jnp.where(kmask, _seg_rms(z, bd, 1.0 / 64) * kgain_ref[...], 1.0)).astype(BF16))
    _store_rows(v_ref, slice(None), _dot(ckv, wv_ref[...]).astype(BF16))


def _mla_project(x, modtab, g1, p):
    bsz = x.shape[0]
    n = _samples_per_step(bsz)
    hq = B_HEADS * LANES
    return pl.pallas_call(
        _mla_kernel,
        grid=(bsz // n, NTILES),
        in_specs=[_row_spec(D, n), _mod_spec(n), _const_spec((1, D)), _const_spec((D, 768)),
                  _const_spec((1, B_Q_RANK)), _const_spec((1, B_KV_RANK)), _const_spec((1, LANES)),
                  _tab_spec(LANES), _tab_spec(LANES),
                  _const_spec((B_Q_RANK, hq)), _const_spec((B_KV_RANK + LANES, hq)),
                  _const_spec((B_KV_RANK, B_HEADS * B_V)), _const_spec((256, 256)),
                  _const_spec((1, 256)), _const_spec((1, 256)), _tab_spec(256), _tab_spec(256),
                  _const_spec((1, 256)), _const_spec((1, 256))],
        out_specs=[_row_spec(hq, n), _row_spec(hq, n), _row_spec(B_HEADS * B_V, n)],
        out_shape=[jax.ShapeDtypeStruct((bsz, NT, hq), BF16), jax.ShapeDtypeStruct((bsz, NT, hq), BF16),
                   jax.ShapeDtypeStruct((bsz, NT, B_HEADS * B_V), BF16)],
        compiler_params=_cparams(2), name="mla_project",
    )(x, modtab, g1, p["w1"], p["gql"], p["gkl"], p["gkr"], p["kcos"], p["ksin"], p["wuq"], p["wk"], p["wv"],
      p["bd"], p["qgain"], p["qinv"], p["qcos"], p["qsin"], p["kgain"], p["kmask"])


def _softmax_pv(sa, sb, v, shift=True):
    outs = []
    for s in (sa, sb):
        p = jnp.exp2(s - jnp.max(s, axis=-1, keepdims=True)) if shift else jnp.exp2(s)
        l = jnp.sum(p, axis=-1, keepdims=True)
        outs.append(_dot(p.astype(BF16), v) / l)
    lane = lax.broadcasted_iota(jnp.int32, (1, LANES), 1)
    return jnp.where(lane < 64, outs[0], outs[1])


def _split_heads(q):
    lane = lax.broadcasted_iota(jnp.int32, (1, LANES), 1)
    zero = jnp.zeros_like(q)
    return jnp.where(lane < 64, q, zero), jnp.where(lane >= 64, q, zero)


PAIRS_PER_STEP = 4


DENSE_PAIRS_PER_STEP = 8


SCORE_BOUND = 60.0


def _score_bound(q_gain2, k_gain2, extra=0.0):
    return (1.05 * jnp.sqrt(q_gain2 * k_gain2) + extra).reshape(1).astype(F32)


def _pair_attn_kernel(bound_ref, q_ref, k_ref, v_ref, o_ref, *, wide, q_pairs_per_kv):
    t = pl.program_id(2)

    def run(nkeys, shift):
        for i in range(DENSE_PAIRS_PER_STEP):
            kv = i // q_pairs_per_kv
            v = v_ref[0, :nkeys, LANES * kv:LANES * (kv + 1)]
            if wide:
                q = q_ref[0, :, 256 * i:256 * (i + 1)]
                k = k_ref[0, :nkeys, 256 * kv:256 * (kv + 1)]
                sa, sb = _dot_t(q[:, :LANES], k[:, :LANES]), _dot_t(q[:, LANES:], k[:, LANES:])
            else:
                qa, qb = _split_heads(q_ref[0, :, LANES * i:LANES * (i + 1)])
                k = k_ref[0, :nkeys, LANES * kv:LANES * (kv + 1)]
                sa, sb = _dot_t(qa, k), _dot_t(qb, k)
            o_ref[0, :, LANES * i:LANES * (i + 1)] = _softmax_pv(sa, sb, v, shift).astype(BF16)

    small = bound_ref[0] <= SCORE_BOUND

    @pl.when(t == 0)
    def _():
        run(CTX, True)

    @pl.when((t > 0) & small)
    def _():
        run(NT, False)

    @pl.when((t > 0) & jnp.logical_not(small))
    def _():
        run(NT, True)


def _pair_attention(bound, q, k, v, *, wide, q_pairs_per_kv):
    bsz = q.shape[0]
    qw = 256 if wide else LANES
    n_steps = q.shape[2] // (qw * DENSE_PAIRS_PER_STEP)
    kv_pairs = DENSE_PAIRS_PER_STEP // q_pairs_per_kv
    return pl.pallas_call(
        functools.partial(_pair_attn_kernel, wide=wide, q_pairs_per_kv=q_pairs_per_kv),
        grid=(bsz, n_steps, NTILES),
        in_specs=[pl.BlockSpec(memory_space=pltpu.SMEM),
                  pl.BlockSpec((1, TM, qw * DENSE_PAIRS_PER_STEP), lambda b, p, t: (b, t, p)),
                  pl.BlockSpec((1, NT, qw * kv_pairs), lambda b, p, t: (b, 0, p)),
                  pl.BlockSpec((1, NT, LANES * kv_pairs), lambda b, p, t: (b, 0, p))],
        out_specs=pl.BlockSpec((1, TM, LANES * DENSE_PAIRS_PER_STEP), lambda b, p, t: (b, t, p)),
        out_shape=jax.ShapeDtypeStruct((bsz, NT, n_steps * DENSE_PAIRS_PER_STEP * LANES), BF16),
        compiler_params=_cparams(3), name="pair_attention",
    )(bound, q, k, v)


def _na_attn_kernel(bound_ref, q_ref, kc_ref, k0_ref, k1_ref, k2_ref, vc_ref, v0_ref, v1_ref, v2_ref, bias_ref,
                    o_ref):
    t = pl.program_id(1)
    small = bound_ref[0] <= SCORE_BOUND

    @pl.when(t == 0)
    def _():
        for i in range(PAIRS_PER_STEP):
            sl = slice(LANES * i, LANES * (i + 1))
            qa, qb = _split_heads(q_ref[0, :, sl])
            k = kc_ref[0, :, sl]
            o_ref[0, :, sl] = _softmax_pv(_dot_t(qa, k), _dot_t(qb, k), vc_ref[0, :, sl]).astype(BF16)

    def run(shift):
        for i in range(PAIRS_PER_STEP):
            sl = slice(LANES * i, LANES * (i + 1))
            qa, qb = _split_heads(q_ref[0, :, sl])
            k = jnp.concatenate([k0_ref[0, :, sl], k1_ref[0, :, sl], k2_ref[0, :, sl], kc_ref[0, :, sl]], axis=0)
            v = jnp.concatenate([v0_ref[0, :, sl], v1_ref[0, :, sl], v2_ref[0, :, sl], vc_ref[0, :, sl]], axis=0)
            sa = _dot_t(qa, k) + bias_ref[0, 2 * i]
            sb = _dot_t(qb, k) + bias_ref[0, 2 * i + 1]
            o_ref[0, :, sl] = _softmax_pv(sa, sb, v, shift).astype(BF16)

    @pl.when((t > 0) & small)
    def _():
        run(False)

    @pl.when((t > 0) & jnp.logical_not(small))
    def _():
        run(True)


def _na_band_tile(t):
    return 1 + jnp.clip(t - 2, 0, LAT_TILES - BAND_TILES)


def _na_attention(bound, q, k, v, bias):
    bsz = q.shape[0]
    n_steps = C_HEADS // (2 * PAIRS_PER_STEP)
    width = LANES * PAIRS_PER_STEP
    qspec = pl.BlockSpec((1, TM, width), lambda p, t, b: (b, t, p))
    cspec = pl.BlockSpec((1, TM, width), lambda p, t, b: (b, 0, p))
    bands = [pl.BlockSpec((1, TM, width), functools.partial(lambda p, t, b, i: (b, _na_band_tile(t) + i, p), i=i))
             for i in range(BAND_TILES)]
    variant = lambda t: jnp.where(t <= 1, 0, jnp.where(t == LAT_TILES, 2, 1))
    return pl.pallas_call(
        _na_attn_kernel,
        grid=(n_steps, NTILES, bsz),
        in_specs=[pl.BlockSpec(memory_space=pltpu.SMEM), qspec, cspec] + bands + [cspec] + bands
                 + [pl.BlockSpec((1, 2 * PAIRS_PER_STEP, TM, (BAND_TILES + 1) * TM),
                                 lambda p, t, b: (variant(t), p, 0, 0))],
        out_specs=qspec,
        out_shape=jax.ShapeDtypeStruct((bsz, NT, C_HEADS * C_HEAD_DIM), BF16),
        compiler_params=_cparams(3), name="na_attention",
    )(bound, q, k, k, k, k, v, v, v, v, bias)


def _out_router_kernel(o_ref, wo_ref, x_ref, mod_ref, g2_ref, r_ref, xo_ref, h_ref, aff_ref):
    attn = _dot(_rows_of(o_ref), wo_ref[...])
    hs = []
    for s in range(x_ref.shape[0]):
        mod = mod_ref[s, 0]
        x = x_ref[s] + mod[2:3] * attn[s * TM:(s + 1) * TM]
        xo_ref[s] = x
        h = _prenorm(x, g2_ref[...], mod[3:4], mod[4:5]).astype(BF16)
        for r, word in enumerate(_pack_rows(h)):
            h_ref[s, pl.ds(r, TM, stride=HS_ROWS), :] = word
        hs.append(h)
    logits = _dot(jnp.concatenate(hs, axis=0), r_ref[...])
    lane = lax.broadcasted_iota(jnp.int32, (1, LANES), 1)
    logits = jnp.where(lane < N_EXPERTS, logits, NEG_INF)
    e = jnp.exp(logits - jnp.max(logits, axis=-1, keepdims=True))
    _store_rows(aff_ref, slice(None), e / jnp.sum(e, axis=-1, keepdims=True))


def _out_router(o, wo, x, modtab, g2, router):
    bsz = x.shape[0]
    n = _samples_per_step(bsz)
    return pl.pallas_call(
        _out_router_kernel,
        grid=(bsz // n, NTILES),
        in_specs=[_row_spec(D, n), _const_spec((D, D)), _row_spec(D, n), _mod_spec(n), _const_spec((1, D)),
                  _const_spec((D, LANES))],
        out_specs=[_row_spec(D, n), pl.BlockSpec((n, TM * HS_ROWS, LANES), lambda b, t: (b, t, 0)),
                   _row_spec(LANES, n)],
        out_shape=[jax.ShapeDtypeStruct((bsz, NT, D), F32),
                   jax.ShapeDtypeStruct((bsz, NT * HS_ROWS, LANES), jnp.int32),
                   jax.ShapeDtypeStruct((bsz, NT, LANES), F32)],
        input_output_aliases={2: 0},
        compiler_params=_cparams(2), name="out_router",
    )(o, wo, x, modtab, g2, router)


def _excl_cumsum_lanes(m, triu):
    out = []
    offset = jnp.zeros((m.shape[0], 1), F32)
    for i in range(m.shape[1] // TM):
        blk = m[:, i * TM:(i + 1) * TM]
        out.append(_dot(blk.astype(BF16), triu) + offset)
        offset = offset + jnp.sum(blk, axis=1, keepdims=True)
    return jnp.concatenate(out, axis=1) if len(out) > 1 else out[0]


def _kth_largest_bits(bits_list, caps):
    def count(bits, cand):
        return jnp.sum(jnp.where(bits >= cand, 1.0, 0.0), axis=1, keepdims=True)

    def step(i, thrs):
        b1 = jnp.left_shift(jnp.int32(1), 29 - 2 * i)
        b0 = jnp.left_shift(jnp.int32(1), 28 - 2 * i)
        out = []
        for bits, cap, thr in zip(bits_list, caps, thrs):
            c1, c2, c3 = count(bits, thr | b1), count(bits, thr | b0), count(bits, thr | b1 | b0)
            with_b1 = jnp.where(c3 >= cap, thr | b1 | b0, thr | b1)
            without = jnp.where(c2 >= cap, thr | b0, thr)
            out.append(jnp.where(c1 >= cap, with_b1, without))
        return tuple(out)

    top = jnp.int32(1 << 30)
    init = tuple(jnp.where(count(bits, top) >= cap, top, jnp.zeros((N_EXPERTS, 1), jnp.int32))
                 for bits, cap in zip(bits_list, caps))
    return lax.fori_loop(0, 15, step, init)


def _topk_slots(bits, thr, cap, triu):
    gt = bits > thr
    eq = bits == thr
    need = cap - jnp.sum(jnp.where(gt, 1.0, 0.0), axis=1, keepdims=True)
    eq_rank = _excl_cumsum_lanes(jnp.where(eq, 1.0, 0.0), triu)
    sel = gt | (eq & (eq_rank < need))
    rank = _excl_cumsum_lanes(jnp.where(sel, 1.0, 0.0), triu)
    return jnp.where(sel, rank, -1.0), rank


OFF_LANE = 3 * N_EXPERTS


def _token_values(aff, first_row):
    n = aff.shape[0]
    hi = aff.astype(BF16).astype(F32)
    r1 = aff - hi
    mid = r1.astype(BF16).astype(F32)
    lo = (r1 - mid).astype(BF16).astype(F32)
    off = (lax.broadcasted_iota(jnp.int32, (n, 1), 0) + first_row) * HS_ROWS
    lane = lax.broadcasted_iota(jnp.int32, (1, LANES), 1)
    vals = (hi + pltpu.roll(mid, N_EXPERTS, axis=1) + pltpu.roll(lo, 2 * N_EXPERTS, axis=1)
            + jnp.where(lane == OFF_LANE, (off >> 7).astype(F32), 0.0)
            + jnp.where(lane == OFF_LANE + 1, (off & (LANES - 1)).astype(F32), 0.0))
    return vals.astype(BF16)


def _topk_kernel(aff_ref, slot_ref, off_ref, gate_ref, win_ref):
    r = lax.broadcasted_iota(jnp.int32, (TM, TM), 0)
    c = lax.broadcasted_iota(jnp.int32, (TM, TM), 1)
    triu = jnp.where(r < c, 1.0, 0.0).astype(BF16)
    lane = lax.broadcasted_iota(jnp.int32, (1, LANES), 1)
    pad_rows = lambda x, fill: jnp.concatenate(
        [x, jnp.full((LANES - x.shape[0], x.shape[1]), fill, x.dtype)], axis=0) if x.shape[0] < LANES else x
    aff_t = aff_ref[0].T[:N_EXPERTS]
    segments = ((0, CTX, CAP_CTX, CAP_LAT), (CTX, NT, CAP_LAT, 0))
    bits = [pltpu.bitcast(aff_t[:, lo:hi], jnp.int32) for lo, hi, _, _ in segments]
    thrs = _kth_largest_bits(bits, [cap for _, _, cap, _ in segments])
    for (lo, hi, cap, dst), seg_bits, thr in zip(segments, bits, thrs):
        slot, rank = _topk_slots(seg_bits, thr, cap, triu)
        slot_ref[0, lo:hi] = pad_rows(slot, -1.0).T.astype(jnp.int32)
        if lo == CTX:
            starts = jnp.where(lane == LAT_TILES, float(cap), 0.0)
            for j in range(LAT_TILES):
                starts = starts + jnp.where(lane == j, rank[:, TM * j:TM * j + 1], 0.0)
            run = pad_rows(starts, 0.0).T.astype(jnp.int32)
            first = jnp.minimum((run[:LAT_TILES] >> 4) << 4, CAP_LAT - WIN)
            ends = pltpu.roll(run[1:LAT_TILES + 1], N_EXPERTS, axis=1)
            win = jnp.where(lane < N_EXPERTS, first, ends)
            for j in range(LAT_TILES):
                win_ref[0, j] = win[j:j + 1]
        vals = _token_values(aff_ref[0, lo:hi], lo)
        slot_ids = lax.broadcasted_iota(jnp.int32, (cap, 1), 0).astype(F32)
        for e in range(N_EXPERTS):
            hit = jnp.where(slot_ids == slot[e:e + 1, :], 1.0, 0.0).astype(BF16)
            picked = _dot(hit, vals)
            mine = ((lane & (N_EXPERTS - 1)) == e) & (lane < OFF_LANE)
            gate_ref[0, e, dst:dst + cap, :] = jnp.sum(jnp.where(mine, picked, 0.0), axis=1, keepdims=True)
            by_lane = pad_rows(picked, 0.0).T
            off = by_lane[OFF_LANE:OFF_LANE + 1] * float(LANES) + by_lane[OFF_LANE + 1:OFF_LANE + 2]
            off_ref[0, e, :, dst:dst + cap] = off[:, :cap].astype(jnp.int32)


def _topk(aff):
    bsz = aff.shape[0]
    tspec = pl.BlockSpec((1, NT, LANES), lambda b: (b, 0, 0))
    return pl.pallas_call(
        _topk_kernel, grid=(bsz,),
        in_specs=[tspec],
        out_specs=[tspec, pl.BlockSpec((1, N_EXPERTS, 1, CAP_ALL), lambda b: (b, 0, 0, 0)),
                   pl.BlockSpec((1, N_EXPERTS, CAP_ALL, 1), lambda b: (b, 0, 0, 0)),
                   pl.BlockSpec((1, LAT_TILES, 1, LANES), lambda b: (b, 0, 0, 0))],
        out_shape=[jax.ShapeDtypeStruct((bsz, NT, LANES), jnp.int32),
                   jax.ShapeDtypeStruct((bsz, N_EXPERTS, 1, CAP_ALL), jnp.int32),
                   jax.ShapeDtypeStruct((bsz, N_EXPERTS, CAP_ALL, 1), F32),
                   jax.ShapeDtypeStruct((bsz, LAT_TILES, 1, LANES), jnp.int32)],
        compiler_params=_cparams(1), name="expert_topk",
    )(aff)


def _pack_rows(hb):
    words = []
    for r in range(HS_ROWS):
        lo = pltpu.bitcast(hb[:, 256 * r:256 * r + LANES].astype(F32), jnp.int32)
        hi = pltpu.bitcast(hb[:, 256 * r + LANES:256 * (r + 1)].astype(F32), jnp.int32)
        words.append(lax.shift_right_logical(lo, jnp.int32(16)) | hi)
    return words


def _unpack_rows(words):
    cols = []
    for w in words:
        cols.append(pltpu.bitcast(lax.shift_left(w, jnp.int32(16)), F32).astype(BF16))
        cols.append(pltpu.bitcast(w & jnp.int32(-65536), F32).astype(BF16))
    return jnp.concatenate(cols, axis=1)


def _expert_kernel(hs_ref, off_ref, gate_ref, wg_ref, wu_ref, wd_ref, *refs, n_slots):
    out_refs, (tile_ref, wg_s, wu_s, wd_s) = refs[:-4], refs[-4:]

    @pl.when(pl.program_id(1) == 0)
    def _():
        wg_s[...] = wg_ref[0, 0].astype(BF16)
        wu_s[...] = wu_ref[0, 0].astype(BF16)
        wd_s[...] = wd_ref[0, 0].astype(BF16)

    for j in range(n_slots):
        src = pl.multiple_of(off_ref[0, 0, 0, j], HS_ROWS)
        tile_ref[pl.ds(j, HS_ROWS, stride=TILE_STRIDE), :] = hs_ref[0, pl.ds(src, HS_ROWS), :]
    xg = _unpack_rows([tile_ref[r * TILE_STRIDE:r * TILE_STRIDE + n_slots, :] for r in range(HS_ROWS)])
    hid = (_silu(_dot(xg, wg_s[...])) * _dot(xg, wu_s[...])).astype(BF16)
    y = _dot(hid, wd_s[...]) * gate_ref[0, 0, :n_slots]
    out_refs[0][0] = y[:CAP_LAT].astype(BF16)
    if n_slots > CAP_LAT:
        out_refs[1][0] = y[CAP_LAT:].astype(BF16)


def _experts(hs, off, gate, wg, wu, wd, layer, with_ctx):
    bsz = hs.shape[0]
    n_slots = CAP_ALL if with_ctx else CAP_LAT
    wspec = pl.BlockSpec((1, 1, D, EXPERT_FF), lambda e, b: (layer, e, 0, 0))
    out_specs = [pl.BlockSpec((1, CAP_LAT, D), lambda e, b: (b, e, 0))]
    out_shape = [jax.ShapeDtypeStruct((bsz, N_EXPERTS * CAP_LAT, D), BF16)]
    if with_ctx:
        out_specs.append(pl.BlockSpec((1, CAP_CTX, D), lambda e, b: (b, e, 0)))
        out_shape.append(jax.ShapeDtypeStruct((bsz, N_EXPERTS * CAP_CTX, D), BF16))
    return pl.pallas_call(
        functools.partial(_expert_kernel, n_slots=n_slots),
        grid=(N_EXPERTS, bsz),
        in_specs=[pl.BlockSpec((1, NT * HS_ROWS, LANES), lambda e, b: (b, 0, 0)),
                  pl.BlockSpec((1, 1, 1, CAP_ALL), lambda e, b: (b, e, 0, 0), memory_space=pltpu.SMEM),
                  pl.BlockSpec((1, 1, CAP_ALL, 1), lambda e, b: (b, e, 0, 0)),
                  wspec, wspec, pl.BlockSpec((1, 1, EXPERT_FF, D), lambda e, b: (layer, e, 0, 0))],
        out_specs=out_specs, out_shape=out_shape,
        scratch_shapes=[pltpu.VMEM((HS_ROWS * TILE_STRIDE, LANES), jnp.int32),
                        pltpu.VMEM((D, EXPERT_FF), BF16), pltpu.VMEM((D, EXPERT_FF), BF16),
                        pltpu.VMEM((EXPERT_FF, D), BF16)],
        compiler_params=_cparams(2), name="experts",
    )(hs, off, gate, wg, wu, wd)


WIN = 64
WIN_GROUP = 256 // WIN


def _combine_lat(slot_ref, yl_ref, win_ref, x_ref, gate, xo_ref):
    slot = slot_ref[0]
    fits = win_ref[0, 0, 0, N_EXPERTS] - win_ref[0, 0, 0, 0] <= WIN
    for e in range(1, N_EXPERTS):
        fits = jnp.logical_and(fits, win_ref[0, 0, 0, N_EXPERTS + e] - win_ref[0, 0, 0, e] <= WIN)

    @pl.when(fits)
    def _():
        cols = lax.broadcasted_iota(jnp.int32, (TM, WIN * WIN_GROUP), 1)
        acc = None
        for g in range(N_EXPERTS // WIN_GROUP):
            hit = cols < 0
            rows = []
            for j in range(WIN_GROUP):
                e = WIN_GROUP * g + j
                start = win_ref[0, 0, 0, e]
                s = slot[:, e:e + 1]
                hit = hit | (jnp.where(s >= 0, s - start + WIN * j, -1) == cols)
                rows.append(yl_ref[0, pl.ds(pl.multiple_of(e * CAP_LAT + start, 16), WIN), :])
            part = _dot(jnp.where(hit, 1.0, 0.0).astype(BF16), jnp.concatenate(rows, axis=0))
            acc = part if acc is None else acc + part
        xo_ref[0] = x_ref[0] + gate * acc

    @pl.when(jnp.logical_not(fits))
    def _():
        cols = lax.broadcasted_iota(jnp.int32, (TM, CAP_LAT), 1)
        onehot = jnp.concatenate(
            [jnp.where(slot[:, e:e + 1] == cols, 1.0, 0.0).astype(BF16) for e in range(N_EXPERTS)], axis=1)
        xo_ref[0] = x_ref[0] + gate * _dot(onehot, yl_ref[0])


def _combine_kernel(slot_ref, yl_ref, yc_ref, win_ref, x_ref, mod_ref, xo_ref):
    t = pl.program_id(1)
    gate = mod_ref[0, 0][5:6]

    @pl.when(t == 0)
    def _():
        slot = slot_ref[0]
        cols = lax.broadcasted_iota(jnp.int32, (TM, N_EXPERTS * CAP_CTX), 1)
        hit = cols < 0
        for e in range(N_EXPERTS):
            s = slot[:, e:e + 1]
            hit = hit | (jnp.where(s >= 0, s + e * CAP_CTX, -1) == cols)
        xo_ref[0] = x_ref[0] + gate * _dot(jnp.where(hit, 1.0, 0.0).astype(BF16), yc_ref[0])

    @pl.when(t > 0)
    def _():
        _combine_lat(slot_ref, yl_ref, win_ref, x_ref, gate, xo_ref)


def _win_spec(tile_of_step):
    return pl.BlockSpec((1, 1, 1, LANES), lambda b, t: (b, tile_of_step(t), 0, 0), memory_space=pltpu.SMEM)


def _combine(slot, yl, yc, win, x, modtab):
    bsz = x.shape[0]
    return pl.pallas_call(
        _combine_kernel,
        grid=(bsz, NTILES),
        in_specs=[_row_spec(LANES), pl.BlockSpec((1, N_EXPERTS * CAP_LAT, D), lambda b, t: (b, 0, 0)),
                  pl.BlockSpec((1, N_EXPERTS * CAP_CTX, D), lambda b, t: (b, 0, 0)),
                  _win_spec(lambda t: jnp.maximum(t - 1, 0)), _row_spec(D), _mod_spec()],
        out_specs=_row_spec(D),
        out_shape=jax.ShapeDtypeStruct((bsz, NT, D), F32),
        input_output_aliases={4: 0},
        compiler_params=_cparams(2), name="moe_combine",
    )(slot, yl, yc, win, x, modtab)


def _combine_last_kernel(slot_ref, yl_ref, win_ref, x_ref, mod_ref, xo_ref):
    _combine_lat(slot_ref, yl_ref, win_ref, x_ref, mod_ref[0, 0][5:6], xo_ref)


def _combine_last(slot, yl, win, x, modtab):
    bsz = x.shape[0]
    lat = lambda width: pl.BlockSpec((1, TM, width), lambda b, t: (b, t + 1, 0))
    return pl.pallas_call(
        _combine_last_kernel,
        grid=(bsz, LAT_TILES),
        in_specs=[lat(LANES), pl.BlockSpec((1, N_EXPERTS * CAP_LAT, D), lambda b, t: (b, 0, 0)),
                  _win_spec(lambda t: t), lat(D), pl.BlockSpec((1, 1, 6, D), lambda b, t: (b, 1, 0, 0))],
        out_specs=pl.BlockSpec((1, TM, D), lambda b, t: (b, t, 0)),
        out_shape=jax.ShapeDtypeStruct((bsz, SEQ, D), F32),
        compiler_params=_cparams(2), name="moe_combine_last",
    )(slot, yl, win, x, modtab)


def _rope_tables(rot_dim, width, lane_lo, lane_hi):
    n_freq = rot_dim // 4
    half = rot_dim // 2
    inv = jnp.float32(ROPE_THETA) ** (-jnp.arange(n_freq, dtype=F32) / n_freq)
    t = jnp.arange(SEQ, dtype=jnp.int32)
    row = (t // GRID_W).astype(F32)
    col = (t % GRID_W).astype(F32)
    ang = jnp.concatenate([row[:, None] * inv, col[:, None] * inv], axis=-1)
    cos, sin = jnp.cos(ang), jnp.sin(ang)
    lane = np.arange(width)
    inside = (lane % LANES >= lane_lo) & (lane % LANES < lane_hi)
    idx = lane % half
    sign = np.where(lane % rot_dim < half, -1.0, 1.0).astype(np.float32)
    cosw = jnp.where(inside[None, :], cos[:, idx], 1.0)
    sinw = jnp.where(inside[None, :], sin[:, idx] * sign[None, :], 0.0)
    ones = jnp.ones((CTX, width), F32)
    return jnp.concatenate([ones, cosw], axis=0), jnp.concatenate([0.0 * ones, sinw], axis=0)


def _block_diag_ones():
    i = np.arange(256)
    return jnp.asarray((i[:, None] // 64) == (i[None, :] // 64), dtype=BF16)


_GQA_ORDER = np.array([8 * kp + 4 * odd + i for kp in range(2) for i in range(4) for odd in range(2)])


def _na_bias(rpb):
    out = []
    rows = SEQ // GRID_W
    qrows, krows = TM // GRID_W, BAND_TILES * TM // GRID_W
    col = np.arange(GRID_W)
    cs = np.clip(col - NA_COLS // 2, 0, GRID_W - NA_COLS)
    ok_c = (col[None, :] >= cs[:, None]) & (col[None, :] < cs[:, None] + NA_COLS)
    dc = np.clip(col[None, :] - col[:, None] + NA_COLS - 1, 0, 2 * NA_COLS - 2)
    pick_c = (dc[:, :, None] == np.arange(2 * NA_COLS - 1)).astype(np.float32)
    n_dc = 2 * NA_COLS - 1
    pick_c2 = np.zeros((GRID_W, 2, GRID_W, 2, n_dc), np.float32)
    for rr in range(2):
        pick_c2[:, rr, :, rr, :] = pick_c
    pick_c2 = pick_c2.reshape(GRID_W, 2 * GRID_W, 2 * n_dc)
    for r0, bs in ((0, 0), (8, 4), (rows - 4, rows - 12)):
        r, kr = r0 + np.arange(qrows), bs + np.arange(krows)
        rs = np.clip(r - NA_ROWS // 2, 0, rows - NA_ROWS)
        ok_r = (kr[None, :] >= rs[:, None]) & (kr[None, :] < rs[:, None] + NA_ROWS)
        dr = kr[None, :] - r[:, None] + NA_ROWS - 1
        pick_r = (dr[:, :, None] == np.arange(2 * NA_ROWS - 1)).astype(np.float32)
        t1 = jnp.einsum('ard,hde->hare', pick_r, rpb.astype(F32) * LOG2E, precision=lax.Precision.HIGHEST)
        t2 = t1.reshape(C_HEADS, qrows, krows // 2, 2 * n_dc)
        band = jnp.einsum('hapf,cuf->hacpu', t2, pick_c2, precision=lax.Precision.HIGHEST)
        ok = (ok_r[:, None, :, None] & ok_c[None, :, None, :]).reshape(1, qrows, GRID_W, krows // 2, 2 * GRID_W)
        band = jnp.where(ok, band, NEG_INF).reshape(C_HEADS, TM, BAND_TILES * TM)
        out.append(jnp.concatenate([band, jnp.zeros((C_HEADS, TM, TM), F32)], axis=-1))
    return jnp.stack(out)


def _gqa_params(w_qkv, qn, kn, wo):
    nq, nk = A_HEADS * A_HEAD_DIM, A_KV_HEADS * A_HEAD_DIM
    wq = w_qkv[:, :nq].reshape(D, A_HEADS, A_HEAD_DIM)[:, _GQA_ORDER].reshape(D, nq)
    w = jnp.concatenate([wq, w_qkv[:, nq:]], axis=1).astype(BF16)
    wo_p = wo.reshape(A_HEADS, A_HEAD_DIM, D)[_GQA_ORDER].reshape(nq, D).astype(BF16)
    scale = A_HEAD_DIM ** -0.5 * LOG2E
    return w, (jnp.tile(qn, 4) * scale)[None], jnp.tile(kn, 4)[None], wo_p


def _mla_params(w_dq, qn_lat, w_uq, w_dkv, kvn_lat, w_ukv, qn, kn, cossin_q, cossin_k):
    qd = B_NOPE + B_ROPE
    w1 = jnp.concatenate([w_dq, w_dkv, jnp.zeros((D, 768 - B_Q_RANK - B_KV_RANK - B_ROPE), F32)], axis=1)
    pad_q = jnp.zeros((B_Q_RANK, B_HEADS, LANES - qd), F32)
    wuq = jnp.concatenate([w_uq.reshape(B_Q_RANK, B_HEADS, qd), pad_q], axis=2).reshape(B_Q_RANK, B_HEADS * LANES)
    ukv = w_ukv.reshape(B_KV_RANK, B_HEADS, B_NOPE + B_V)
    wk_top = jnp.concatenate([ukv[:, :, :B_NOPE], jnp.zeros((B_KV_RANK, B_HEADS, LANES - B_NOPE), F32)], axis=2)
    route = np.zeros((LANES, B_HEADS, LANES), np.float32)
    route[np.arange(B_ROPE), :, B_NOPE + np.arange(B_ROPE)] = 1.0
    wk = jnp.concatenate([wk_top.reshape(B_KV_RANK, -1), jnp.asarray(route).reshape(LANES, -1)], axis=0)
    wv = ukv[:, :, B_NOPE:].reshape(B_KV_RANK, B_HEADS * B_V)
    scale = qd ** -0.5 * LOG2E
    zpad = jnp.zeros((LANES - qd,), F32)
    qgain = jnp.tile(jnp.concatenate([qn * scale, zpad]), 2)[None]
    qinv = jnp.tile(jnp.concatenate([jnp.full((B_NOPE,), 1.0 / B_NOPE), jnp.full((LANES - B_NOPE,), 1.0 / B_ROPE)]), 2)
    kgain = jnp.tile(jnp.concatenate([kn[:B_NOPE], jnp.zeros((LANES - B_NOPE,), F32)]), 2)[None]
    kmask = jnp.tile(jnp.concatenate([jnp.ones((B_NOPE,), F32), jnp.zeros((LANES - B_NOPE,), F32)]), 2)[None]
    gkr = jnp.concatenate([kn[B_NOPE:], jnp.zeros((LANES - B_ROPE,), F32)])[None]
    return dict(w1=w1.astype(BF16), gql=qn_lat[None], gkl=kvn_lat[None], gkr=gkr, kcos=cossin_k[0], ksin=cossin_k[1],
                wuq=wuq.astype(BF16), wk=wk.astype(BF16), wv=wv.astype(BF16), bd=_block_diag_ones(),
                qgain=qgain, qinv=qinv[None].astype(F32), qcos=cossin_q[0], qsin=cossin_q[1], kgain=kgain, kmask=kmask)


def kernel(x, c, ctx, c_ctx, ada_w, ada_b, norm1_w, norm2_w, a_wqkv, a_qnorm, a_knorm, a_wo, b_wdq, b_qnorm_lat, b_wuq, b_wdkv, b_kvnorm_lat, b_wukv, b_qnorm, b_knorm, b_wo, c_wqkv, c_qnorm, c_knorm, c_rpb, c_wo, moe_router, moe_wg, moe_wu, moe_wd):
    bsz = x.shape[0]
    assert x.shape[1:] == (SEQ, D) and ctx.shape[1:] == (CTX, D)
    mod_rows = -(-(bsz + 1) // 16) * 16
    cvec = jnp.concatenate([c, c_ctx[None], jnp.zeros((mod_rows - bsz - 1, D), F32)], axis=0)
    mods = _ada_all(cvec, ada_w, ada_b)
    xs = jnp.concatenate([ctx, x], axis=1)

    bd = _block_diag_ones()
    rope64 = _rope_tables(A_HEAD_DIM, 256, 0, LANES)
    rope_mla_q = _rope_tables(B_ROPE, 256, B_NOPE, B_NOPE + B_ROPE)
    rope_mla_k = _rope_tables(B_ROPE, LANES, 0, B_ROPE)
    router = jnp.pad(moe_router, ((0, 0), (0, 0), (0, LANES - N_EXPERTS))).astype(BF16)

    for i in range(DEPTH):
        last = i == DEPTH - 1
        kind, j = i % 3, i // 3
        m = mods[i]
        m_lat = m[:bsz].reshape(bsz, 6, D)
        m_ctx = jnp.broadcast_to(m[bsz].reshape(1, 6, D), (bsz, 6, D))
        modtab = jnp.stack([m_ctx, m_lat], axis=1)
        g1, g2 = norm1_w[i][None], norm2_w[i][None]

        if kind == 0:
            w, gq, gk, wo = _gqa_params(a_wqkv[j], a_qnorm[j], a_knorm[j], a_wo[j])
            q, k, v = _qkv_project(xs, modtab, g1, w, bd, gq, gk, rope64[0], rope64[1],
                                   A_HEADS * A_HEAD_DIM, A_KV_HEADS * A_HEAD_DIM, True)
            bound = _score_bound(A_HEAD_DIM * jnp.max(gq * gq), A_HEAD_DIM * jnp.max(gk * gk))
            o = _pair_attention(bound, q, k, v, wide=False, q_pairs_per_kv=4)
        elif kind == 1:
            p = _mla_params(b_wdq[j], b_qnorm_lat[j], b_wuq[j], b_wdkv[j], b_kvnorm_lat[j], b_wukv[j],
                            b_qnorm[j], b_knorm[j], rope_mla_q, rope_mla_k)
            q, k, v = _mla_project(xs, modtab, g1, p)
            sq_max = lambda g: jnp.max(g * g)
            qg, kg = p["qgain"][0, :LANES], b_knorm[j]
            bound = _score_bound(B_NOPE * sq_max(qg[:B_NOPE]) + B_ROPE * sq_max(qg[B_NOPE:]),
                                 B_NOPE * sq_max(kg[:B_NOPE]) + B_ROPE * sq_max(kg[B_NOPE:]))
            o = _pair_attention(bound, q, k, v, wide=True, q_pairs_per_kv=1)
            wo = b_wo[j].astype(BF16)
        else:
            hd = C_HEADS * C_HEAD_DIM
            gq = (jnp.tile(c_qnorm[j], 4) * (C_HEAD_DIM ** -0.5 * LOG2E))[None]
            gk = jnp.tile(c_knorm[j], 4)[None]
            q, k, v = _qkv_project(xs, modtab, g1, c_wqkv[j].astype(BF16), bd, gq, gk, rope64[0], rope64[1],
                                   hd, hd, False)
            bound = _score_bound(C_HEAD_DIM * jnp.max(gq * gq), C_HEAD_DIM * jnp.max(gk * gk),
                                 extra=LOG2E * jnp.max(jnp.abs(c_rpb[j])))
            o = _na_attention(bound, q, k, v, _na_bias(c_rpb[j]))
            wo = c_wo[j].astype(BF16)

        xs, hs, aff = _out_router(o, wo, xs, modtab, g2, router[i])
        slot, off, gate, win = _topk(aff)
        ys = _experts(hs, off, gate,
                      moe_wg, moe_wu, moe_wd, layer=i, with_ctx=not last)
        if last:
            return _combine_last(slot, ys[0], win, xs, modtab)
        xs = _combine(slot, ys[0], ys[1], win, xs, modtab)
```

```python
import functools

import numpy as np
import jax
import jax.numpy as jnp
from jax import lax
from jax.experimental import pallas as pl
from jax.experimental.pallas import tpu as pltpu

F32 = jnp.float32
BF16 = jnp.bfloat16

D = 1024
SEQ = 2048
CTX = 256
NT = CTX + SEQ
DEPTH = 4
GRID_W = 64
ROPE_THETA = 10000.0
EPS = 1e-6
NEG_INF = -1e30
LOG2E = 1.4426950408889634

TM = 256
NTILES = NT // TM
LAT_TILES = SEQ // TM

A_HEADS, A_KV_HEADS, A_HEAD_DIM = 16, 4, 64
B_HEADS, B_Q_RANK, B_KV_RANK, B_NOPE, B_ROPE, B_V = 16, 384, 256, 64, 32, 64
C_HEADS, C_HEAD_DIM, NA_ROWS, NA_COLS = 16, 64, 8, 16
N_EXPERTS, EXPERT_FF = 16, 1024
CAP_LAT = 2 * SEQ // N_EXPERTS
CAP_CTX = 2 * CTX // N_EXPERTS
CAP_ALL = CAP_LAT + CAP_CTX
HS_ROWS = D // 256
TILE_STRIDE = 296
LANES = 128
BF16_ROWS = 16
BAND_TILES = 3

VMEM_LIMIT = 56 * 1024 * 1024


def _cparams(n_axes):
    return pltpu.CompilerParams(dimension_semantics=("arbitrary",) * n_axes, vmem_limit_bytes=VMEM_LIMIT)


def _dot(a, b):
    return jnp.dot(a, b, preferred_element_type=F32)


def _dot_t(a, b):
    return lax.dot_general(a, b, (((1,), (1,)), ((), ())), preferred_element_type=F32)


def _silu(x):
    return x / (1.0 + jnp.exp(-x))


def _prenorm(x, gain, shift, scale):
    ms = jnp.mean(x * x, axis=-1, keepdims=True)
    return (x * lax.rsqrt(ms + EPS) * gain) * (1.0 + scale) + shift


def _seg_rms(z, bd, inv_n):
    ss = _dot((z * z).astype(BF16), bd)
    return lax.rsqrt(ss * inv_n + EPS)


def _rope(z, cosw, sinw, first_half, half):
    w = z.shape[-1]
    up = pltpu.roll(z, w - half, axis=1)
    dn = pltpu.roll(z, half, axis=1)
    return z * cosw + jnp.where(first_half, up, dn) * sinw


def _ada_kernel(c_ref, w_ref, b_ref, o_ref):
    s = _silu(c_ref[...]).astype(BF16)
    o_ref[0] = _dot(s, w_ref[0].astype(BF16)) + b_ref[0]


def _ada_all(cvec, ada_w, ada_b):
    rows = cvec.shape[0]
    tn = 512
    return pl.pallas_call(
        _ada_kernel,
        grid=(DEPTH, 6 * D // tn),
        in_specs=[pl.BlockSpec((rows, D), lambda i, j: (0, 0)),
                  pl.BlockSpec((1, D, tn), lambda i, j: (i, 0, j)),
                  pl.BlockSpec((1, 1, tn), lambda i, j: (i, 0, j))],
        out_specs=pl.BlockSpec((1, rows, tn), lambda i, j: (i, 0, j)),
        out_shape=jax.ShapeDtypeStruct((DEPTH, rows, 6 * D), F32),
        compiler_params=_cparams(2), name="adaln",
    )(cvec, ada_w, ada_b.reshape(DEPTH, 1, 6 * D))


def _samples_per_step(bsz):
    return 2 if bsz % 2 == 0 else 1


def _mod_spec(n=1):
    return pl.BlockSpec((n, 1, 6, D), lambda b, t: (b, jnp.minimum(t, 1), 0, 0))


def _row_spec(width, n=1):
    return pl.BlockSpec((n, TM, width), lambda b, t: (b, t, 0))


def _rows_of(ref, prep=lambda s, x: x):
    return jnp.concatenate([prep(s, ref[s]) for s in range(ref.shape[0])], axis=0)


def _store_rows(ref, cols, val):
    for s in range(ref.shape[0]):
        ref[s, :, cols] = val[s * TM:(s + 1) * TM]


def _const_spec(shape):
    return pl.BlockSpec(shape, lambda b, t: (0,) * len(shape))


def _tab_spec(width):
    return pl.BlockSpec((TM, width), lambda b, t: (t, 0))


def _stream_in(x, n):
    if isinstance(x, tuple):
        return ([pl.BlockSpec((n, TM, D), lambda b, t: (b, 0, 0)),
                 pl.BlockSpec((n, TM, D), lambda b, t: (b, jnp.maximum(t - 1, 0), 0))], list(x))
    return [_row_spec(D, n)], [x]


def _stream_tile(x_refs, s):
    if len(x_refs) == 2:
        return jnp.where(pl.program_id(1) == 0, x_refs[0][s], x_refs[1][s])
    return x_refs[0][s]


def _qkv_kernel(*refs, nq, nk, rope, n_x):
    x_refs = refs[:n_x]
    mod_ref, g1_ref, w_ref, bd_ref, gq_ref, gk_ref, cos_ref, sin_ref, q_ref, k_ref, v_ref = refs[n_x:]
    n = q_ref.shape[0]
    g1 = g1_ref[...]
    h = jnp.concatenate([_prenorm(_stream_tile(x_refs, s), g1, mod_ref[s, 0][0:1], mod_ref[s, 0][1:2])
                         for s in range(n)], axis=0).astype(BF16)
    acc = _dot(h, w_ref[...])
    bd = bd_ref[...]
    lane = lax.broadcasted_iota(jnp.int32, (1, 256), 1)
    first_half = (lane % 64) < 32
    cosw, sinw = jnp.concatenate([cos_ref[...]] * n, axis=0), jnp.concatenate([sin_ref[...]] * n, axis=0)

    def finish(z, gain):
        z = z * _seg_rms(z, bd, 1.0 / 64) * gain
        if rope:
            z = _rope(z, cosw, sinw, first_half, 32)
        return z.astype(BF16)

    for j in range(nq // 256):
        cols = slice(j * 256, (j + 1) * 256)
        _store_rows(q_ref, cols, finish(acc[:, cols], gq_ref[...]))
    for j in range(nk // 256):
        cols = slice(j * 256, (j + 1) * 256)
        _store_rows(k_ref, cols, finish(acc[:, nq + j * 256:nq + (j + 1) * 256], gk_ref[...]))
    _store_rows(v_ref, slice(None), acc[:, nq + nk:].astype(BF16))


def _qkv_project(x, modtab, g1, w, bd, gq, gk, cosw, sinw, nq, nk, rope):
    bsz = modtab.shape[0]
    n = _samples_per_step(bsz)
    x_specs, x_args = _stream_in(x, n)
    return pl.pallas_call(
        functools.partial(_qkv_kernel, nq=nq, nk=nk, rope=rope, n_x=len(x_args)),
        grid=(bsz // n, NTILES),
        in_specs=x_specs + [_mod_spec(n), _const_spec((1, D)), _const_spec((D, nq + 2 * nk)),
                            _const_spec((256, 256)), _const_spec((1, 256)), _const_spec((1, 256)),
                            _tab_spec(256), _tab_spec(256)],
        out_specs=[_row_spec(nq, n), _row_spec(nk, n), _row_spec(nk, n)],
        out_shape=[jax.ShapeDtypeStruct((bsz, NT, nq), BF16), jax.ShapeDtypeStruct((bsz, NT, nk), BF16),
                   jax.ShapeDtypeStruct((bsz, NT, nk), BF16)],
        compiler_params=_cparams(2), name="qkv_project",
    )(*x_args, modtab, g1, w, bd, gq, gk, cosw, sinw)


def _mla_kernel(x_ref, mod_ref, g1_ref, w1_ref, gql_ref, gkl_ref, gkr_ref, kcos_ref, ksin_ref,
                wuq_ref, wk_ref, wv_ref, bd_ref, qgain_ref, qinv_ref, qcos_ref, qsin_ref, kgain_ref, kmask_ref,
                q_ref, k_ref, v_ref):
    n = x_ref.shape[0]
    g1 = g1_ref[...]
    tile = lambda ref: jnp.concatenate([ref[...]] * n, axis=0)
    h = _rows_of(x_ref, lambda s, x: _prenorm(x, g1, mod_ref[s, 0][0:1], mod_ref[s, 0][1:2])).astype(BF16)
    a = _dot(h, w1_ref[...])
    cq = a[:, :B_Q_RANK]
    cq = (cq * lax.rsqrt(jnp.mean(cq * cq, axis=-1, keepdims=True) + EPS) * gql_ref[...]).astype(BF16)
    ckv = a[:, B_Q_RANK:B_Q_RANK + B_KV_RANK]
    ckv = (ckv * lax.rsqrt(jnp.mean(ckv * ckv, axis=-1, keepdims=True) + EPS) * gkl_ref[...]).astype(BF16)
    kr = a[:, B_Q_RANK + B_KV_RANK:]
    kr = kr * lax.rsqrt(jnp.sum(kr * kr, axis=-1, keepdims=True) * (1.0 / B_ROPE) + EPS) * gkr_ref[...]
    lane128 = lax.broadcasted_iota(jnp.int32, (1, LANES), 1)
    kr = _rope(kr, tile(kcos_ref), tile(ksin_ref), (lane128 % 32) < 16, 16).astype(BF16)

    bd = bd_ref[...]
    lane = lax.broadcasted_iota(jnp.int32, (1, 256), 1)
    first_half = (lane % 32) < 16
    qcos, qsin = tile(qcos_ref), tile(qsin_ref)
    q2 = _dot(cq, wuq_ref[...])
    k2 = _dot(jnp.concatenate([ckv, kr], axis=1), wk_ref[...])
    kmask = kmask_ref[...] > 0.0
    for j in range(q2.shape[1] // 256):
        sl = slice(j * 256, (j + 1) * 256)
        z = q2[:, sl]
        z = z * _seg_rms(z, bd, qinv_ref[...]) * qgain_ref[...]
        _store_rows(q_ref, sl, _rope(z, qcos, qsin, first_half, 16).astype(BF16))
        z = k2[:, sl]
        _store_rows(k_ref, sl, (z * jnp.where(kmask, _seg_rms(z, bd, 1.0 / 64) * kgain_ref[...], 1.0)).astype(BF16))
    _store_rows(v_ref, slice(None), _dot(ckv, wv_ref[...]).astype(BF16))


def _mla_project(x, modtab, g1, p):
    bsz = x.shape[0]
    n = _samples_per_step(bsz)
    hq = B_HEADS * LANES
    return pl.pallas_call(
        _mla_kernel,
        grid=(bsz // n, NTILES),
        in_specs=[_row_spec(D, n), _mod_spec(n), _const_spec((1, D)), _const_spec((D, 768)),
                  _const_spec((1, B_Q_RANK)), _const_spec((1, B_KV_RANK)), _const_spec((1, LANES)),
                  _tab_spec(LANES), _tab_spec(LANES),
                  _const_spec((B_Q_RANK, hq)), _const_spec((B_KV_RANK + LANES, hq)),
                  _const_spec((B_KV_RANK, B_HEADS * B_V)), _const_spec((256, 256)),
                  _const_spec((1, 256)), _const_spec((1, 256)), _tab_spec(256), _tab_spec(256),
                  _const_spec((1, 256)), _const_spec((1, 256))],
        out_specs=[_row_spec(hq, n), _row_spec(hq, n), _row_spec(B_HEADS * B_V, n)],
        out_shape=[jax.ShapeDtypeStruct((bsz, NT, hq), BF16), jax.ShapeDtypeStruct((bsz, NT, hq), BF16),
                   jax.ShapeDtypeStruct((bsz, NT, B_HEADS * B_V), BF16)],
        compiler_params=_cparams(2), name="mla_project",
    )(x, modtab, g1, p["w1"], p["gql"], p["gkl"], p["gkr"], p["kcos"], p["ksin"], p["wuq"], p["wk"], p["wv"],
      p["bd"], p["qgain"], p["qinv"], p["qcos"], p["qsin"], p["kgain"], p["kmask"])


def _softmax_pv(sa, sb, v, shift=True):
    outs = []
    for s in (sa, sb):
        p = jnp.exp2(s - jnp.max(s, axis=-1, keepdims=True)) if shift else jnp.exp2(s)
        l = jnp.sum(p, axis=-1, keepdims=True)
        outs.append(_dot(p.astype(BF16), v) / l)
    lane = lax.broadcasted_iota(jnp.int32, (1, LANES), 1)
    return jnp.where(lane < 64, outs[0], outs[1])


def _split_heads(q):
    lane = lax.broadcasted_iota(jnp.int32, (1, LANES), 1)
    zero = jnp.zeros_like(q)
    return jnp.where(lane < 64, q, zero), jnp.where(lane >= 64, q, zero)


PAIRS_PER_STEP = 4


DENSE_PAIRS_PER_STEP = 8


SCORE_BOUND = 60.0


def _score_bound(q_gain2, k_gain2, extra=0.0):
    return (1.05 * jnp.sqrt(q_gain2 * k_gain2) + extra).reshape(1).astype(F32)


def _pair_attn_kernel(bound_ref, q_ref, k_ref, v_ref, o_ref, *, wide, q_pairs_per_kv):
    t = pl.program_id(2)

    def run(nkeys, shift):
        for i in range(DENSE_PAIRS_PER_STEP):
            kv = i // q_pairs_per_kv
            v = v_ref[0, :nkeys, LANES * kv:LANES * (kv + 1)]
            if wide:
                q = q_ref[0, :, 256 * i:256 * (i + 1)]
                k = k_ref[0, :nkeys, 256 * kv:256 * (kv + 1)]
                sa, sb = _dot_t(q[:, :LANES], k[:, :LANES]), _dot_t(q[:, LANES:], k[:, LANES:])
            else:
                qa, qb = _split_heads(q_ref[0, :, LANES * i:LANES * (i + 1)])
                k = k_ref[0, :nkeys, LANES * kv:LANES * (kv + 1)]
                sa, sb = _dot_t(qa, k), _dot_t(qb, k)
            o_ref[0, :, LANES * i:LANES * (i + 1)] = _softmax_pv(sa, sb, v, shift).astype(BF16)

    small = bound_ref[0] <= SCORE_BOUND

    @pl.when(t == 0)
    def _():
        run(CTX, True)

    @pl.when((t > 0) & small)
    def _():
        run(NT, False)

    @pl.when((t > 0) & jnp.logical_not(small))
    def _():
        run(NT, True)


def _pair_attention(bound, q, k, v, *, wide, q_pairs_per_kv):
    bsz = q.shape[0]
    qw = 256 if wide else LANES
    n_steps = q.shape[2] // (qw * DENSE_PAIRS_PER_STEP)
    kv_pairs = DENSE_PAIRS_PER_STEP // q_pairs_per_kv
    return pl.pallas_call(
        functools.partial(_pair_attn_kernel, wide=wide, q_pairs_per_kv=q_pairs_per_kv),
        grid=(bsz, n_steps, NTILES),
        in_specs=[pl.BlockSpec(memory_space=pltpu.SMEM),
                  pl.BlockSpec((1, TM, qw * DENSE_PAIRS_PER_STEP), lambda b, p, t: (b, t, p)),
                  pl.BlockSpec((1, NT, qw * kv_pairs), lambda b, p, t: (b, 0, p)),
                  pl.BlockSpec((1, NT, LANES * kv_pairs), lambda b, p, t: (b, 0, p))],
        out_specs=pl.BlockSpec((1, TM, LANES * DENSE_PAIRS_PER_STEP), lambda b, p, t: (b, t, p)),
        out_shape=jax.ShapeDtypeStruct((bsz, NT, n_steps * DENSE_PAIRS_PER_STEP * LANES), BF16),
        compiler_params=_cparams(3), name="pair_attention",
    )(bound, q, k, v)


def _na_attn_kernel(bound_ref, q_ref, kc_ref, k0_ref, k1_ref, k2_ref, vc_ref, v0_ref, v1_ref, v2_ref, bias_ref,
                    o_ref):
    t = pl.program_id(1)
    small = bound_ref[0] <= SCORE_BOUND

    @pl.when(t == 0)
    def _():
        for i in range(PAIRS_PER_STEP):
            sl = slice(LANES * i, LANES * (i + 1))
            qa, qb = _split_heads(q_ref[0, :, sl])
            k = kc_ref[0, :, sl]
            o_ref[0, :, sl] = _softmax_pv(_dot_t(qa, k), _dot_t(qb, k), vc_ref[0, :, sl]).astype(BF16)

    def run(shift):
        for i in range(PAIRS_PER_STEP):
            sl = slice(LANES * i, LANES * (i + 1))
            qa, qb = _split_heads(q_ref[0, :, sl])
            k = jnp.concatenate([k0_ref[0, :, sl], k1_ref[0, :, sl], k2_ref[0, :, sl], kc_ref[0, :, sl]], axis=0)
            v = jnp.concatenate([v0_ref[0, :, sl], v1_ref[0, :, sl], v2_ref[0, :, sl], vc_ref[0, :, sl]], axis=0)
            sa = _dot_t(qa, k) + bias_ref[0, 2 * i]
            sb = _dot_t(qb, k) + bias_ref[0, 2 * i + 1]
            o_ref[0, :, sl] = _softmax_pv(sa, sb, v, shift).astype(BF16)

    @pl.when((t > 0) & small)
    def _():
        run(False)

    @pl.when((t > 0) & jnp.logical_not(small))
    def _():
        run(True)


def _na_band_tile(t):
    return 1 + jnp.clip(t - 2, 0, LAT_TILES - BAND_TILES)


def _na_attention(bound, q, k, v, bias):
    bsz = q.shape[0]
    n_steps = C_HEADS // (2 * PAIRS_PER_STEP)
    width = LANES * PAIRS_PER_STEP
    qspec = pl.BlockSpec((1, TM, width), lambda p, t, b: (b, t, p))
    cspec = pl.BlockSpec((1, TM, width), lambda p, t, b: (b, 0, p))
    bands = [pl.BlockSpec((1, TM, width), functools.partial(lambda p, t, b, i: (b, _na_band_tile(t) + i, p), i=i))
             for i in range(BAND_TILES)]
    variant = lambda t: jnp.where(t <= 1, 0, jnp.where(t == LAT_TILES, 2, 1))
    return pl.pallas_call(
        _na_attn_kernel,
        grid=(n_steps, NTILES, bsz),
        in_specs=[pl.BlockSpec(memory_space=pltpu.SMEM), qspec, cspec] + bands + [cspec] + bands
                 + [pl.BlockSpec((1, 2 * PAIRS_PER_STEP, TM, (BAND_TILES + 1) * TM),
                                 lambda p, t, b: (variant(t), p, 0, 0))],
        out_specs=qspec,
        out_shape=jax.ShapeDtypeStruct((bsz, NT, C_HEADS * C_HEAD_DIM), BF16),
        compiler_params=_cparams(3), name="na_attention",
    )(bound, q, k, k, k, k, v, v, v, v, bias)


def _out_router_kernel(*refs, n_x):
    x_refs = refs[:n_x]
    o_ref, wo_ref, mod_ref, g2_ref, r_ref, xo_ref, h_ref, aff_ref = refs[n_x:]
    attn = _dot(_rows_of(o_ref), wo_ref[...])
    hs = []
    for s in range(xo_ref.shape[0]):
        mod = mod_ref[s, 0]
        x = _stream_tile(x_refs, s) + mod[2:3] * attn[s * TM:(s + 1) * TM]
        xo_ref[s] = x
        h = _prenorm(x, g2_ref[...], mod[3:4], mod[4:5]).astype(BF16)
        for r, word in enumerate(_pack_rows(h)):
            h_ref[s, pl.ds(r, TM, stride=HS_ROWS), :] = word
        hs.append(h)
    logits = _dot(jnp.concatenate(hs, axis=0), r_ref[...])
    lane = lax.broadcasted_iota(jnp.int32, (1, LANES), 1)
    logits = jnp.where(lane < N_EXPERTS, logits, NEG_INF)
    e = jnp.exp(logits - jnp.max(logits, axis=-1, keepdims=True))
    _store_rows(aff_ref, slice(None), e / jnp.sum(e, axis=-1, keepdims=True))


def _out_router(o, wo, x, modtab, g2, router):
    bsz = modtab.shape[0]
    n = _samples_per_step(bsz)
    x_specs, x_args = _stream_in(x, n)
    return pl.pallas_call(
        functools.partial(_out_router_kernel, n_x=len(x_args)),
        grid=(bsz // n, NTILES),
        in_specs=x_specs + [_row_spec(D, n), _const_spec((D, D)), _mod_spec(n), _const_spec((1, D)),
                            _const_spec((D, LANES))],
        out_specs=[_row_spec(D, n), pl.BlockSpec((n, TM * HS_ROWS, LANES), lambda b, t: (b, t, 0)),
                   _row_spec(LANES, n)],
        out_shape=[jax.ShapeDtypeStruct((bsz, NT, D), F32),
                   jax.ShapeDtypeStruct((bsz, NT * HS_ROWS, LANES), jnp.int32),
                   jax.ShapeDtypeStruct((bsz, NT, LANES), F32)],
        input_output_aliases={0: 0} if len(x_args) == 1 else {},
        compiler_params=_cparams(2), name="out_router",
    )(*x_args, o, wo, modtab, g2, router)


def _excl_cumsum_lanes(m, triu):
    out = []
    offset = jnp.zeros((m.shape[0], 1), F32)
    for i in range(m.shape[1] // TM):
        blk = m[:, i * TM:(i + 1) * TM]
        out.append(_dot(blk.astype(BF16), triu) + offset)
        offset = offset + jnp.sum(blk, axis=1, keepdims=True)
    return jnp.concatenate(out, axis=1) if len(out) > 1 else out[0]


def _kth_largest_bits(bits_list, caps):
    def count(bits, cand):
        return jnp.sum(jnp.where(bits >= cand, 1.0, 0.0), axis=1, keepdims=True)

    def step(i, thrs):
        b1 = jnp.left_shift(jnp.int32(1), 29 - 2 * i)
        b0 = jnp.left_shift(jnp.int32(1), 28 - 2 * i)
        out = []
        for bits, cap, thr in zip(bits_list, caps, thrs):
            c1, c2, c3 = count(bits, thr | b1), count(bits, thr | b0), count(bits, thr | b1 | b0)
            with_b1 = jnp.where(c3 >= cap, thr | b1 | b0, thr | b1)
            without = jnp.where(c2 >= cap, thr | b0, thr)
            out.append(jnp.where(c1 >= cap, with_b1, without))
        return tuple(out)

    top = jnp.int32(1 << 30)
    init = tuple(jnp.where(count(bits, top) >= cap, top, jnp.zeros((N_EXPERTS, 1), jnp.int32))
                 for bits, cap in zip(bits_list, caps))
    return lax.fori_loop(0, 15, step, init)


def _topk_slots(bits, thr, cap, triu):
    gt = bits > thr
    eq = bits == thr
    need = cap - jnp.sum(jnp.where(gt, 1.0, 0.0), axis=1, keepdims=True)
    eq_rank = _excl_cumsum_lanes(jnp.where(eq, 1.0, 0.0), triu)
    sel = gt | (eq & (eq_rank < need))
    rank = _excl_cumsum_lanes(jnp.where(sel, 1.0, 0.0), triu)
    return jnp.where(sel, rank, -1.0), rank


OFF_LANE = 3 * N_EXPERTS


def _token_values(aff, first_row):
    n = aff.shape[0]
    hi = aff.astype(BF16).astype(F32)
    r1 = aff - hi
    mid = r1.astype(BF16).astype(F32)
    lo = (r1 - mid).astype(BF16).astype(F32)
    off = (lax.broadcasted_iota(jnp.int32, (n, 1), 0) + first_row) * HS_ROWS
    lane = lax.broadcasted_iota(jnp.int32, (1, LANES), 1)
    vals = (hi + pltpu.roll(mid, N_EXPERTS, axis=1) + pltpu.roll(lo, 2 * N_EXPERTS, axis=1)
            + jnp.where(lane == OFF_LANE, (off >> 7).astype(F32), 0.0)
            + jnp.where(lane == OFF_LANE + 1, (off & (LANES - 1)).astype(F32), 0.0))
    return vals.astype(BF16)


def _topk_kernel(aff_ref, slot_ref, off_ref, gate_ref, win_ref):
    r = lax.broadcasted_iota(jnp.int32, (TM, TM), 0)
    c = lax.broadcasted_iota(jnp.int32, (TM, TM), 1)
    triu = jnp.where(r < c, 1.0, 0.0).astype(BF16)
    lane = lax.broadcasted_iota(jnp.int32, (1, LANES), 1)
    pad_rows = lambda x, fill: jnp.concatenate(
        [x, jnp.full((LANES - x.shape[0], x.shape[1]), fill, x.dtype)], axis=0) if x.shape[0] < LANES else x
    aff_t = aff_ref[0].T[:N_EXPERTS]
    segments = ((0, CTX, CAP_CTX, CAP_LAT), (CTX, NT, CAP_LAT, 0))
    bits = [pltpu.bitcast(aff_t[:, lo:hi], jnp.int32) for lo, hi, _, _ in segments]
    thrs = _kth_largest_bits(bits, [cap for _, _, cap, _ in segments])
    for (lo, hi, cap, dst), seg_bits, thr in zip(segments, bits, thrs):
        slot, rank = _topk_slots(seg_bits, thr, cap, triu)
        slot_ref[0, lo:hi] = pad_rows(slot, -1.0).T.astype(jnp.int32)
        if lo == CTX:
            starts = jnp.where(lane == LAT_TILES, float(cap), 0.0)
            for j in range(LAT_TILES):
                starts = starts + jnp.where(lane == j, rank[:, TM * j:TM * j + 1], 0.0)
            run = pad_rows(starts, 0.0).T.astype(jnp.int32)
            first = jnp.minimum(run[:LAT_TILES] & -BF16_ROWS, CAP_LAT - WIN)
            ends = pltpu.roll(run[1:LAT_TILES + 1], N_EXPERTS, axis=1)
            win = jnp.where(lane < N_EXPERTS, first, ends)
            for j in range(LAT_TILES):
                win_ref[0, j] = win[j:j + 1]
        vals = _token_values(aff_ref[0, lo:hi], lo)
        slot_ids = lax.broadcasted_iota(jnp.int32, (cap, 1), 0).astype(F32)
        for e in range(N_EXPERTS):
            hit = jnp.where(slot_ids == slot[e:e + 1, :], 1.0, 0.0).astype(BF16)
            picked = _dot(hit, vals)
            mine = ((lane & (N_EXPERTS - 1)) == e) & (lane < OFF_LANE)
            gate_ref[0, e, dst:dst + cap, :] = jnp.sum(jnp.where(mine, picked, 0.0), axis=1, keepdims=True)
            by_lane = pad_rows(picked, 0.0).T
            off = by_lane[OFF_LANE:OFF_LANE + 1] * float(LANES) + by_lane[OFF_LANE + 1:OFF_LANE + 2]
            off_ref[0, e, :, dst:dst + cap] = off[:, :cap].astype(jnp.int32)


def _topk(aff):
    bsz = aff.shape[0]
    tspec = pl.BlockSpec((1, NT, LANES), lambda b: (b, 0, 0))
    return pl.pallas_call(
        _topk_kernel, grid=(bsz,),
        in_specs=[tspec],
        out_specs=[tspec, pl.BlockSpec((1, N_EXPERTS, 1, CAP_ALL), lambda b: (b, 0, 0, 0)),
                   pl.BlockSpec((1, N_EXPERTS, CAP_ALL, 1), lambda b: (b, 0, 0, 0)),
                   pl.BlockSpec((1, LAT_TILES, 1, LANES), lambda b: (b, 0, 0, 0))],
        out_shape=[jax.ShapeDtypeStruct((bsz, NT, LANES), jnp.int32),
                   jax.ShapeDtypeStruct((bsz, N_EXPERTS, 1, CAP_ALL), jnp.int32),
                   jax.ShapeDtypeStruct((bsz, N_EXPERTS, CAP_ALL, 1), F32),
                   jax.ShapeDtypeStruct((bsz, LAT_TILES, 1, LANES), jnp.int32)],
        compiler_params=_cparams(1), name="expert_topk",
    )(aff)


def _pack_rows(hb):
    words = []
    for r in range(HS_ROWS):
        lo = pltpu.bitcast(hb[:, 256 * r:256 * r + LANES].astype(F32), jnp.int32)
        hi = pltpu.bitcast(hb[:, 256 * r + LANES:256 * (r + 1)].astype(F32), jnp.int32)
        words.append(lax.shift_right_logical(lo, jnp.int32(16)) | hi)
    return words


def _unpack_rows(words):
    cols = []
    for w in words:
        cols.append(pltpu.bitcast(lax.shift_left(w, jnp.int32(16)), F32).astype(BF16))
        cols.append(pltpu.bitcast(w & jnp.int32(-65536), F32).astype(BF16))
    return jnp.concatenate(cols, axis=1)


def _expert_kernel(hs_ref, off_ref, gate_ref, wg_ref, wu_ref, wd_ref, *refs, n_slots):
    out_refs, (tile_ref, wg_s, wu_s, wd_s) = refs[:-4], refs[-4:]

    @pl.when(pl.program_id(1) == 0)
    def _():
        wg_s[...] = wg_ref[0, 0].astype(BF16)
        wu_s[...] = wu_ref[0, 0].astype(BF16)
        wd_s[...] = wd_ref[0, 0].astype(BF16)

    for j in range(n_slots):
        src = pl.multiple_of(off_ref[0, 0, 0, j], HS_ROWS)
        tile_ref[pl.ds(j, HS_ROWS, stride=TILE_STRIDE), :] = hs_ref[0, pl.ds(src, HS_ROWS), :]
    xg = _unpack_rows([tile_ref[r * TILE_STRIDE:r * TILE_STRIDE + n_slots, :] for r in range(HS_ROWS)])
    hid = (_silu(_dot(xg, wg_s[...])) * _dot(xg, wu_s[...])).astype(BF16)
    y = _dot(hid, wd_s[...]) * gate_ref[0, 0, :n_slots]
    out_refs[0][0] = y[:CAP_LAT].astype(BF16)
    if n_slots > CAP_LAT:
        out_refs[1][0] = y[CAP_LAT:].astype(BF16)


def _experts(hs, off, gate, wg, wu, wd, layer, with_ctx):
    bsz = hs.shape[0]
    n_slots = CAP_ALL if with_ctx else CAP_LAT
    wspec = pl.BlockSpec((1, 1, D, EXPERT_FF), lambda e, b: (layer, e, 0, 0))
    out_specs = [pl.BlockSpec((1, CAP_LAT, D), lambda e, b: (b, e, 0))]
    out_shape = [jax.ShapeDtypeStruct((bsz, N_EXPERTS * CAP_LAT, D), BF16)]
    if with_ctx:
        out_specs.append(pl.BlockSpec((1, CAP_CTX, D), lambda e, b: (b, e, 0)))
        out_shape.append(jax.ShapeDtypeStruct((bsz, N_EXPERTS * CAP_CTX, D), BF16))
    return pl.pallas_call(
        functools.partial(_expert_kernel, n_slots=n_slots),
        grid=(N_EXPERTS, bsz),
        in_specs=[pl.BlockSpec((1, NT * HS_ROWS, LANES), lambda e, b: (b, 0, 0)),
                  pl.BlockSpec((1, 1, 1, CAP_ALL), lambda e, b: (b, e, 0, 0), memory_space=pltpu.SMEM),
                  pl.BlockSpec((1, 1, CAP_ALL, 1), lambda e, b: (b, e, 0, 0)),
                  wspec, wspec, pl.BlockSpec((1, 1, EXPERT_FF, D), lambda e, b: (layer, e, 0, 0))],
        out_specs=out_specs, out_shape=out_shape,
        scratch_shapes=[pltpu.VMEM((HS_ROWS * TILE_STRIDE, LANES), jnp.int32),
                        pltpu.VMEM((D, EXPERT_FF), BF16), pltpu.VMEM((D, EXPERT_FF), BF16),
                        pltpu.VMEM((EXPERT_FF, D), BF16)],
        compiler_params=_cparams(2), name="experts",
    )(hs, off, gate, wg, wu, wd)


WIN = 64
WIN_GROUP = 256 // WIN


def _combine_lat(slot_ref, yl_ref, win_ref, x_ref, gate, xo_ref):
    slot = slot_ref[0]
    fits = win_ref[0, 0, 0, N_EXPERTS] - win_ref[0, 0, 0, 0] <= WIN
    for e in range(1, N_EXPERTS):
        fits = jnp.logical_and(fits, win_ref[0, 0, 0, N_EXPERTS + e] - win_ref[0, 0, 0, e] <= WIN)

    @pl.when(fits)
    def _():
        cols = lax.broadcasted_iota(jnp.int32, (TM, WIN * WIN_GROUP), 1)
        acc = None
        for g in range(N_EXPERTS // WIN_GROUP):
            hit = cols < 0
            rows = []
            for j in range(WIN_GROUP):
                e = WIN_GROUP * g + j
                start = win_ref[0, 0, 0, e]
                s = slot[:, e:e + 1]
                hit = hit | (jnp.where(s >= 0, s - start + WIN * j, -1) == cols)
                rows.append(yl_ref[0, pl.ds(pl.multiple_of(e * CAP_LAT + start, BF16_ROWS), WIN), :])
            part = _dot(jnp.where(hit, 1.0, 0.0).astype(BF16), jnp.concatenate(rows, axis=0))
            acc = part if acc is None else acc + part
        xo_ref[0] = x_ref[0] + gate * acc

    @pl.when(jnp.logical_not(fits))
    def _():
        cols = lax.broadcasted_iota(jnp.int32, (TM, CAP_LAT), 1)
        onehot = jnp.concatenate(
            [jnp.where(slot[:, e:e + 1] == cols, 1.0, 0.0).astype(BF16) for e in range(N_EXPERTS)], axis=1)
        xo_ref[0] = x_ref[0] + gate * _dot(onehot, yl_ref[0])


def _combine_kernel(slot_ref, yl_ref, yc_ref, win_ref, x_ref, mod_ref, xo_ref):
    t = pl.program_id(1)
    gate = mod_ref[0, 0][5:6]

    @pl.when(t == 0)
    def _():
        slot = slot_ref[0]
        cols = lax.broadcasted_iota(jnp.int32, (TM, N_EXPERTS * CAP_CTX), 1)
        hit = cols < 0
        for e in range(N_EXPERTS):
            s = slot[:, e:e + 1]
            hit = hit | (jnp.where(s >= 0, s + e * CAP_CTX, -1) == cols)
        xo_ref[0] = x_ref[0] + gate * _dot(jnp.where(hit, 1.0, 0.0).astype(BF16), yc_ref[0])

    @pl.when(t > 0)
    def _():
        _combine_lat(slot_ref, yl_ref, win_ref, x_ref, gate, xo_ref)


def _win_spec(tile_of_step):
    return pl.BlockSpec((1, 1, 1, LANES), lambda b, t: (b, tile_of_step(t), 0, 0), memory_space=pltpu.SMEM)


def _combine(slot, yl, yc, win, x, modtab):
    bsz = x.shape[0]
    return pl.pallas_call(
        _combine_kernel,
        grid=(bsz, NTILES),
        in_specs=[_row_spec(LANES), pl.BlockSpec((1, N_EXPERTS * CAP_LAT, D), lambda b, t: (b, 0, 0)),
                  pl.BlockSpec((1, N_EXPERTS * CAP_CTX, D), lambda b, t: (b, 0, 0)),
                  _win_spec(lambda t: jnp.maximum(t - 1, 0)), _row_spec(D), _mod_spec()],
        out_specs=_row_spec(D),
        out_shape=jax.ShapeDtypeStruct((bsz, NT, D), F32),
        input_output_aliases={4: 0},
        compiler_params=_cparams(2), name="moe_combine",
    )(slot, yl, yc, win, x, modtab)


def _combine_last_kernel(slot_ref, yl_ref, win_ref, x_ref, mod_ref, xo_ref):
    _combine_lat(slot_ref, yl_ref, win_ref, x_ref, mod_ref[0, 0][5:6], xo_ref)


def _combine_last(slot, yl, win, x, modtab):
    bsz = x.shape[0]
    lat = lambda width: pl.BlockSpec((1, TM, width), lambda b, t: (b, t + 1, 0))
    return pl.pallas_call(
        _combine_last_kernel,
        grid=(bsz, LAT_TILES),
        in_specs=[lat(LANES), pl.BlockSpec((1, N_EXPERTS * CAP_LAT, D), lambda b, t: (b, 0, 0)),
                  _win_spec(lambda t: t), lat(D), pl.BlockSpec((1, 1, 6, D), lambda b, t: (b, 1, 0, 0))],
        out_specs=pl.BlockSpec((1, TM, D), lambda b, t: (b, t, 0)),
        out_shape=jax.ShapeDtypeStruct((bsz, SEQ, D), F32),
        compiler_params=_cparams(2), name="moe_combine_last",
    )(slot, yl, win, x, modtab)


def _rope_tables(rot_dim, width, lane_lo, lane_hi):
    n_freq = rot_dim // 4
    half = rot_dim // 2
    inv = jnp.float32(ROPE_THETA) ** (-jnp.arange(n_freq, dtype=F32) / n_freq)
    t = jnp.arange(SEQ, dtype=jnp.int32)
    row = (t // GRID_W).astype(F32)
    col = (t % GRID_W).astype(F32)
    ang = jnp.concatenate([row[:, None] * inv, col[:, None] * inv], axis=-1)
    cos, sin = jnp.cos(ang), jnp.sin(ang)
    lane = np.arange(width)
    inside = (lane % LANES >= lane_lo) & (lane % LANES < lane_hi)
    idx = lane % half
    sign = np.where(lane % rot_dim < half, -1.0, 1.0).astype(np.float32)
    cosw = jnp.where(inside[None, :], cos[:, idx], 1.0)
    sinw = jnp.where(inside[None, :], sin[:, idx] * sign[None, :], 0.0)
    ones = jnp.ones((CTX, width), F32)
    return jnp.concatenate([ones, cosw], axis=0), jnp.concatenate([0.0 * ones, sinw], axis=0)


def _block_diag_ones():
    i = np.arange(256)
    return jnp.asarray((i[:, None] // 64) == (i[None, :] // 64), dtype=BF16)


_GQA_ORDER = np.array([8 * kp + 4 * odd + i for kp in range(2) for i in range(4) for odd in range(2)])


def _na_bias(rpb):
    out = []
    rows = SEQ // GRID_W
    qrows, krows = TM // GRID_W, BAND_TILES * TM // GRID_W
    col = np.arange(GRID_W)
    cs = np.clip(col - NA_COLS // 2, 0, GRID_W - NA_COLS)
    ok_c = (col[None, :] >= cs[:, None]) & (col[None, :] < cs[:, None] + NA_COLS)
    dc = np.clip(col[None, :] - col[:, None] + NA_COLS - 1, 0, 2 * NA_COLS - 2)
    pick_c = (dc[:, :, None] == np.arange(2 * NA_COLS - 1)).astype(np.float32)
    n_dc = 2 * NA_COLS - 1
    pick_c2 = np.zeros((GRID_W, 2, GRID_W, 2, n_dc), np.float32)
    for rr in range(2):
        pick_c2[:, rr, :, rr, :] = pick_c
    pick_c2 = pick_c2.reshape(GRID_W, 2 * GRID_W, 2 * n_dc)
    for r0, bs in ((0, 0), (8, 4), (rows - 4, rows - 12)):
        r, kr = r0 + np.arange(qrows), bs + np.arange(krows)
        rs = np.clip(r - NA_ROWS // 2, 0, rows - NA_ROWS)
        ok_r = (kr[None, :] >= rs[:, None]) & (kr[None, :] < rs[:, None] + NA_ROWS)
        dr = kr[None, :] - r[:, None] + NA_ROWS - 1
        pick_r = (dr[:, :, None] == np.arange(2 * NA_ROWS - 1)).astype(np.float32)
        t1 = jnp.einsum('ard,hde->hare', pick_r, rpb.astype(F32) * LOG2E, precision=lax.Precision.HIGHEST)
        t2 = t1.reshape(C_HEADS, qrows, krows // 2, 2 * n_dc)
        band = jnp.einsum('hapf,cuf->hacpu', t2, pick_c2, precision=lax.Precision.HIGHEST)
        ok = (ok_r[:, None, :, None] & ok_c[None, :, None, :]).reshape(1, qrows, GRID_W, krows // 2, 2 * GRID_W)
        band = jnp.where(ok, band, NEG_INF).reshape(C_HEADS, TM, BAND_TILES * TM)
        out.append(jnp.concatenate([band, jnp.zeros((C_HEADS, TM, TM), F32)], axis=-1))
    return jnp.stack(out)


def _gqa_params(w_qkv, qn, kn, wo):
    nq, nk = A_HEADS * A_HEAD_DIM, A_KV_HEADS * A_HEAD_DIM
    wq = w_qkv[:, :nq].reshape(D, A_HEADS, A_HEAD_DIM)[:, _GQA_ORDER].reshape(D, nq)
    w = jnp.concatenate([wq, w_qkv[:, nq:]], axis=1).astype(BF16)
    wo_p = wo.reshape(A_HEADS, A_HEAD_DIM, D)[_GQA_ORDER].reshape(nq, D).astype(BF16)
    scale = A_HEAD_DIM ** -0.5 * LOG2E
    return w, (jnp.tile(qn, 4) * scale)[None], jnp.tile(kn, 4)[None], wo_p


def _mla_params(w_dq, qn_lat, w_uq, w_dkv, kvn_lat, w_ukv, qn, kn, cossin_q, cossin_k):
    qd = B_NOPE + B_ROPE
    w1 = jnp.concatenate([w_dq, w_dkv, jnp.zeros((D, 768 - B_Q_RANK - B_KV_RANK - B_ROPE), F32)], axis=1)
    pad_q = jnp.zeros((B_Q_RANK, B_HEADS, LANES - qd), F32)
    wuq = jnp.concatenate([w_uq.reshape(B_Q_RANK, B_HEADS, qd), pad_q], axis=2).reshape(B_Q_RANK, B_HEADS * LANES)
    ukv = w_ukv.reshape(B_KV_RANK, B_HEADS, B_NOPE + B_V)
    wk_top = jnp.concatenate([ukv[:, :, :B_NOPE], jnp.zeros((B_KV_RANK, B_HEADS, LANES - B_NOPE), F32)], axis=2)
    route = np.zeros((LANES, B_HEADS, LANES), np.float32)
    route[np.arange(B_ROPE), :, B_NOPE + np.arange(B_ROPE)] = 1.0
    wk = jnp.concatenate([wk_top.reshape(B_KV_RANK, -1), jnp.asarray(route).reshape(LANES, -1)], axis=0)
    wv = ukv[:, :, B_NOPE:].reshape(B_KV_RANK, B_HEADS * B_V)
    scale = qd ** -0.5 * LOG2E
    zpad = jnp.zeros((LANES - qd,), F32)
    qgain = jnp.tile(jnp.concatenate([qn * scale, zpad]), 2)[None]
    qinv = jnp.tile(jnp.concatenate([jnp.full((B_NOPE,), 1.0 / B_NOPE), jnp.full((LANES - B_NOPE,), 1.0 / B_ROPE)]), 2)
    kgain = jnp.tile(jnp.concatenate([kn[:B_NOPE], jnp.zeros((LANES - B_NOPE,), F32)]), 2)[None]
    kmask = jnp.tile(jnp.concatenate([jnp.ones((B_NOPE,), F32), jnp.zeros((LANES - B_NOPE,), F32)]), 2)[None]
    gkr = jnp.concatenate([kn[B_NOPE:], jnp.zeros((LANES - B_ROPE,), F32)])[None]
    return dict(w1=w1.astype(BF16), gql=qn_lat[None], gkl=kvn_lat[None], gkr=gkr, kcos=cossin_k[0], ksin=cossin_k[1],
                wuq=wuq.astype(BF16), wk=wk.astype(BF16), wv=wv.astype(BF16), bd=_block_diag_ones(),
                qgain=qgain, qinv=qinv[None].astype(F32), qcos=cossin_q[0], qsin=cossin_q[1], kgain=kgain, kmask=kmask)


def kernel(x, c, ctx, c_ctx, ada_w, ada_b, norm1_w, norm2_w, a_wqkv, a_qnorm, a_knorm, a_wo, b_wdq, b_qnorm_lat, b_wuq, b_wdkv, b_kvnorm_lat, b_wukv, b_qnorm, b_knorm, b_wo, c_wqkv, c_qnorm, c_knorm, c_rpb, c_wo, moe_router, moe_wg, moe_wu, moe_wd):
    bsz = x.shape[0]
    assert x.shape[1:] == (SEQ, D) and ctx.shape[1:] == (CTX, D)
    mod_rows = -(-(bsz + 1) // 16) * 16
    cvec = jnp.concatenate([c, c_ctx[None], jnp.zeros((mod_rows - bsz - 1, D), F32)], axis=0)
    mods = _ada_all(cvec, ada_w, ada_b)
    xs = (ctx, x)

    bd = _block_diag_ones()
    rope64 = _rope_tables(A_HEAD_DIM, 256, 0, LANES)
    rope_mla_q = _rope_tables(B_ROPE, 256, B_NOPE, B_NOPE + B_ROPE)
    rope_mla_k = _rope_tables(B_ROPE, LANES, 0, B_ROPE)
    router = jnp.pad(moe_router, ((0, 0), (0, 0), (0, LANES - N_EXPERTS))).astype(BF16)

    for i in range(DEPTH):
        last = i == DEPTH - 1
        kind, j = i % 3, i // 3
        m = mods[i]
        m_lat = m[:bsz].reshape(bsz, 6, D)
        m_ctx = jnp.broadcast_to(m[bsz].reshape(1, 6, D), (bsz, 6, D))
        modtab = jnp.stack([m_ctx, m_lat], axis=1)
        g1, g2 = norm1_w[i][None], norm2_w[i][None]

        if kind == 0:
            w, gq, gk, wo = _gqa_params(a_wqkv[j], a_qnorm[j], a_knorm[j], a_wo[j])
            q, k, v = _qkv_project(xs, modtab, g1, w, bd, gq, gk, rope64[0], rope64[1],
                                   A_HEADS * A_HEAD_DIM, A_KV_HEADS * A_HEAD_DIM, True)
            bound = _score_bound(A_HEAD_DIM * jnp.max(gq * gq), A_HEAD_DIM * jnp.max(gk * gk))
            o = _pair_attention(bound, q, k, v, wide=False, q_pairs_per_kv=4)
        elif kind == 1:
            p = _mla_params(b_wdq[j], b_qnorm_lat[j], b_wuq[j], b_wdkv[j], b_kvnorm_lat[j], b_wukv[j],
                            b_qnorm[j], b_knorm[j], rope_mla_q, rope_mla_k)
            q, k, v = _mla_project(xs, modtab, g1, p)
            sq_max = lambda g: jnp.max(g * g)
            qg, kg = p["qgain"][0, :LANES], b_knorm[j]
            bound = _score_bound(B_NOPE * sq_max(qg[:B_NOPE]) + B_ROPE * sq_max(qg[B_NOPE:]),
                                 B_NOPE * sq_max(kg[:B_NOPE]) + B_ROPE * sq_max(kg[B_NOPE:]))
            o = _pair_attention(bound, q, k, v, wide=True, q_pairs_per_kv=1)
            wo = b_wo[j].astype(BF16)
        else:
            hd = C_HEADS * C_HEAD_DIM
            gq = (jnp.tile(c_qnorm[j], 4) * (C_HEAD_DIM ** -0.5 * LOG2E))[None]
            gk = jnp.tile(c_knorm[j], 4)[None]
            q, k, v = _qkv_project(xs, modtab, g1, c_wqkv[j].astype(BF16), bd, gq, gk, rope64[0], rope64[1],
                                   hd, hd, False)
            bound = _score_bound(C_HEAD_DIM * jnp.max(gq * gq), C_HEAD_DIM * jnp.max(gk * gk),
                                 extra=LOG2E * jnp.max(jnp.abs(c_rpb[j])))
            o = _na_attention(bound, q, k, v, _na_bias(c_rpb[j]))
            wo = c_wo[j].astype(BF16)

        xs, hs, aff = _out_router(o, wo, xs, modtab, g2, router[i])
        slot, off, gate, win = _topk(aff)
        ys = _experts(hs, off, gate,
                      moe_wg, moe_wu, moe_wd, layer=i, with_ctx=not last)
        if last:
            return _combine_last(slot, ys[0], win, xs, modtab)
        xs = _combine(slot, ys[0], ys[1], win, xs, modtab)
```

```python
import functools

import numpy as np
import jax
import jax.numpy as jnp
from jax import lax
from jax.experimental import pallas as pl
from jax.experimental.pallas import tpu as pltpu

F32 = jnp.float32
BF16 = jnp.bfloat16

D = 1024
SEQ = 2048
CTX = 256
NT = CTX + SEQ
DEPTH = 4
GRID_W = 64
ROPE_THETA = 10000.0
EPS = 1e-6
NEG_INF = -1e30
LOG2E = 1.4426950408889634

TM = 256
NTILES = NT // TM
LAT_TILES = SEQ // TM

A_HEADS, A_KV_HEADS, A_HEAD_DIM = 16, 4, 64
B_HEADS, B_Q_RANK, B_KV_RANK, B_NOPE, B_ROPE, B_V = 16, 384, 256, 64, 32, 64
C_HEADS, C_HEAD_DIM, NA_ROWS, NA_COLS = 16, 64, 8, 16
N_EXPERTS, EXPERT_FF = 16, 1024
CAP_LAT = 2 * SEQ // N_EXPERTS
CAP_CTX = 2 * CTX // N_EXPERTS
CAP_ALL = CAP_LAT + CAP_CTX
HS_ROWS = D // 256
TILE_STRIDE = 296
LANES = 128
BF16_ROWS = 16
BAND_TILES = 3

VMEM_LIMIT = 56 * 1024 * 1024


def _cparams(n_axes):
    return pltpu.CompilerParams(dimension_semantics=("arbitrary",) * n_axes, vmem_limit_bytes=VMEM_LIMIT)


def _dot(a, b):
    return jnp.dot(a, b, preferred_element_type=F32)


def _dot_t(a, b):
    return lax.dot_general(a, b, (((1,), (1,)), ((), ())), preferred_element_type=F32)


def _silu(x):
    return x / (1.0 + jnp.exp(-x))


def _prenorm(x, gain, shift, scale):
    ms = jnp.mean(x * x, axis=-1, keepdims=True)
    return (x * lax.rsqrt(ms + EPS) * gain) * (1.0 + scale) + shift


def _seg_rms(z, bd, inv_n):
    ss = _dot((z * z).astype(BF16), bd)
    return lax.rsqrt(ss * inv_n + EPS)


def _rope(z, cosw, sinw, first_half, half):
    w = z.shape[-1]
    up = pltpu.roll(z, w - half, axis=1)
    dn = pltpu.roll(z, half, axis=1)
    return z * cosw + jnp.where(first_half, up, dn) * sinw


def _ada_kernel(c_ref, w_ref, b_ref, o_ref):
    s = _silu(c_ref[...]).astype(BF16)
    o_ref[0] = _dot(s, w_ref[0].astype(BF16)) + b_ref[0]


def _ada_all(cvec, ada_w, ada_b):
    rows = cvec.shape[0]
    tn = 512
    return pl.pallas_call(
        _ada_kernel,
        grid=(DEPTH, 6 * D // tn),
        in_specs=[pl.BlockSpec((rows, D), lambda i, j: (0, 0)),
                  pl.BlockSpec((1, D, tn), lambda i, j: (i, 0, j)),
                  pl.BlockSpec((1, 1, tn), lambda i, j: (i, 0, j))],
        out_specs=pl.BlockSpec((1, rows, tn), lambda i, j: (i, 0, j)),
        out_shape=jax.ShapeDtypeStruct((DEPTH, rows, 6 * D), F32),
        compiler_params=_cparams(2), name="adaln",
    )(cvec, ada_w, ada_b.reshape(DEPTH, 1, 6 * D))


def _samples_per_step(bsz):
    return 2 if bsz % 2 == 0 else 1


def _mod_spec(n=1):
    return pl.BlockSpec((n, 1, 6, D), lambda b, t: (b, jnp.minimum(t, 1), 0, 0))


def _row_spec(width, n=1):
    return pl.BlockSpec((n, TM, width), lambda b, t: (b, t, 0))


def _rows_of(ref, prep=lambda s, x: x):
    return jnp.concatenate([prep(s, ref[s]) for s in range(ref.shape[0])], axis=0)


def _store_rows(ref, cols, val):
    for s in range(ref.shape[0]):
        ref[s, :, cols] = val[s * TM:(s + 1) * TM]


def _const_spec(shape):
    return pl.BlockSpec(shape, lambda b, t: (0,) * len(shape))


def _tab_spec(width):
    return pl.BlockSpec((TM, width), lambda b, t: (t, 0))


def _stream_in(x, n):
    if isinstance(x, tuple):
        return ([pl.BlockSpec((n, TM, D), lambda b, t: (b, 0, 0)),
                 pl.BlockSpec((n, TM, D), lambda b, t: (b, jnp.maximum(t - 1, 0), 0))], list(x))
    return [_row_spec(D, n)], [x]


def _stream_tile(x_refs, s):
    if len(x_refs) == 2:
        return jnp.where(pl.program_id(1) == 0, x_refs[0][s], x_refs[1][s])
    return x_refs[0][s]


def _qkv_kernel(*refs, nq, nk, rope, n_x):
    x_refs = refs[:n_x]
    mod_ref, g1_ref, w_ref, bd_ref, gq_ref, gk_ref, cos_ref, sin_ref, q_ref, k_ref, v_ref = refs[n_x:]
    n = q_ref.shape[0]
    g1 = g1_ref[...]
    h = jnp.concatenate([_prenorm(_stream_tile(x_refs, s), g1, mod_ref[s, 0][0:1], mod_ref[s, 0][1:2])
                         for s in range(n)], axis=0).astype(BF16)
    acc = _dot(h, w_ref[...])
    bd = bd_ref[...]
    lane = lax.broadcasted_iota(jnp.int32, (1, 256), 1)
    first_half = (lane % 64) < 32
    cosw, sinw = jnp.concatenate([cos_ref[...]] * n, axis=0), jnp.concatenate([sin_ref[...]] * n, axis=0)

    def finish(z, gain):
        z = z * _seg_rms(z, bd, 1.0 / 64) * gain
        if rope:
            z = _rope(z, cosw, sinw, first_half, 32)
        return z.astype(BF16)

    for j in range(nq // 256):
        cols = slice(j * 256, (j + 1) * 256)
        _store_rows(q_ref, cols, finish(acc[:, cols], gq_ref[...]))
    for j in range(nk // 256):
        cols = slice(j * 256, (j + 1) * 256)
        _store_rows(k_ref, cols, finish(acc[:, nq + j * 256:nq + (j + 1) * 256], gk_ref[...]))
    _store_rows(v_ref, slice(None), acc[:, nq + nk:].astype(BF16))


def _qkv_project(x, modtab, g1, w, bd, gq, gk, cosw, sinw, nq, nk, rope):
    bsz = modtab.shape[0]
    n = _samples_per_step(bsz)
    x_specs, x_args = _stream_in(x, n)
    return pl.pallas_call(
        functools.partial(_qkv_kernel, nq=nq, nk=nk, rope=rope, n_x=len(x_args)),
        grid=(bsz // n, NTILES),
        in_specs=x_specs + [_mod_spec(n), _const_spec((1, D)), _const_spec((D, nq + 2 * nk)),
                            _const_spec((256, 256)), _const_spec((1, 256)), _const_spec((1, 256)),
                            _tab_spec(256), _tab_spec(256)],
        out_specs=[_row_spec(nq, n), _row_spec(nk, n), _row_spec(nk, n)],
        out_shape=[jax.ShapeDtypeStruct((bsz, NT, nq), BF16), jax.ShapeDtypeStruct((bsz, NT, nk), BF16),
                   jax.ShapeDtypeStruct((bsz, NT, nk), BF16)],
        compiler_params=_cparams(2), name="qkv_project",
    )(*x_args, modtab, g1, w, bd, gq, gk, cosw, sinw)


def _mla_kernel(x_ref, mod_ref, g1_ref, w1_ref, gql_ref, gkl_ref, gkr_ref, kcos_ref, ksin_ref,
                wuq_ref, wk_ref, wv_ref, bd_ref, qgain_ref, qinv_ref, qcos_ref, qsin_ref, kgain_ref, kmask_ref,
                q_ref, k_ref, v_ref):
    n = x_ref.shape[0]
    g1 = g1_ref[...]
    tile = lambda ref: jnp.concatenate([ref[...]] * n, axis=0)
    h = _rows_of(x_ref, lambda s, x: _prenorm(x, g1, mod_ref[s, 0][0:1], mod_ref[s, 0][1:2])).astype(BF16)
    a = _dot(h, w1_ref[...])
    cq = a[:, :B_Q_RANK]
    cq = (cq * lax.rsqrt(jnp.mean(cq * cq, axis=-1, keepdims=True) + EPS) * gql_ref[...]).astype(BF16)
    ckv = a[:, B_Q_RANK:B_Q_RANK + B_KV_RANK]
    ckv = (ckv * lax.rsqrt(jnp.mean(ckv * ckv, axis=-1, keepdims=True) + EPS) * gkl_ref[...]).astype(BF16)
    kr = a[:, B_Q_RANK + B_KV_RANK:]
    kr = kr * lax.rsqrt(jnp.sum(kr * kr, axis=-1, keepdims=True) * (1.0 / B_ROPE) + EPS) * gkr_ref[...]
    lane128 = lax.broadcasted_iota(jnp.int32, (1, LANES), 1)
    kr = _rope(kr, tile(kcos_ref), tile(ksin_ref), (lane128 % 32) < 16, 16).astype(BF16)

    bd = bd_ref[...]
    lane = lax.broadcasted_iota(jnp.int32, (1, 256), 1)
    first_half = (lane % 32) < 16
    qcos, qsin = tile(qcos_ref), tile(qsin_ref)
    q2 = _dot(cq, wuq_ref[...])
    k2 = _dot(jnp.concatenate([ckv, kr], axis=1), wk_ref[...])
    kmask = kmask_ref[...] > 0.0
    for j in range(q2.shape[1] // 256):
        sl = slice(j * 256, (j + 1) * 256)
        z = q2[:, sl]
        z = z * _seg_rms(z, bd, qinv_ref[...]) * qgain_ref[...]
        _store_rows(q_ref, sl, _rope(z, qcos, qsin, first_half, 16).astype(BF16))
        z = k2[:, sl]
        _store_rows(k_ref, sl, (z * jnp.where(kmask, _seg_rms(z, bd, 1.0 / 64) * kgain_ref[...], 1.0)).astype(BF16))
    _store_rows(v_ref, slice(None), _dot(ckv, wv_ref[...]).astype(BF16))


def _mla_project(x, modtab, g1, p):
    bsz = x.shape[0]
    n = _samples_per_step(bsz)
    hq = B_HEADS * LANES
    return pl.pallas_call(
        _mla_kernel,
        grid=(bsz // n, NTILES),
        in_specs=[_row_spec(D, n), _mod_spec(n), _const_spec((1, D)), _const_spec((D, 768)),
                  _const_spec((1, B_Q_RANK)), _const_spec((1, B_KV_RANK)), _const_spec((1, LANES)),
                  _tab_spec(LANES), _tab_spec(LANES),
                  _const_spec((B_Q_RANK, hq)), _const_spec((B_KV_RANK + LANES, hq)),
                  _const_spec((B_KV_RANK, B_HEADS * B_V)), _const_spec((256, 256)),
                  _const_spec((1, 256)), _const_spec((1, 256)), _tab_spec(256), _tab_spec(256),
                  _const_spec((1, 256)), _const_spec((1, 256))],
        out_specs=[_row_spec(hq, n), _row_spec(hq, n), _row_spec(B_HEADS * B_V, n)],
        out_shape=[jax.ShapeDtypeStruct((bsz, NT, hq), BF16), jax.ShapeDtypeStruct((bsz, NT, hq), BF16),
                   jax.ShapeDtypeStruct((bsz, NT, B_HEADS * B_V), BF16)],
        compiler_params=_cparams(2), name="mla_project",
    )(x, modtab, g1, p["w1"], p["gql"], p["gkl"], p["gkr"], p["kcos"], p["ksin"], p["wuq"], p["wk"], p["wv"],
      p["bd"], p["qgain"], p["qinv"], p["qcos"], p["qsin"], p["kgain"], p["kmask"])


def _softmax_pv(sa, sb, v, shift=True):
    outs = []
    for s in (sa, sb):
        p = jnp.exp2(s - jnp.max(s, axis=-1, keepdims=True)) if shift else jnp.exp2(s)
        l = jnp.sum(p, axis=-1, keepdims=True)
        outs.append(_dot(p.astype(BF16), v) / l)
    lane = lax.broadcasted_iota(jnp.int32, (1, LANES), 1)
    return jnp.where(lane < 64, outs[0], outs[1])


def _split_heads(q):
    lane = lax.broadcasted_iota(jnp.int32, (1, LANES), 1)
    zero = jnp.zeros_like(q)
    return jnp.where(lane < 64, q, zero), jnp.where(lane >= 64, q, zero)


PAIRS_PER_STEP = 4


DENSE_PAIRS_PER_STEP = 8


SCORE_BOUND = 60.0


def _score_bound(q_gain2, k_gain2, extra=0.0):
    return (1.05 * jnp.sqrt(q_gain2 * k_gain2) + extra).reshape(1).astype(F32)


def _pair_attn_kernel(bound_ref, q_ref, k_ref, v_ref, o_ref, *, wide, q_pairs_per_kv):
    t = pl.program_id(2)

    def run(nkeys, shift):
        for i in range(DENSE_PAIRS_PER_STEP):
            kv = i // q_pairs_per_kv
            v = v_ref[0, :nkeys, LANES * kv:LANES * (kv + 1)]
            if wide:
                q = q_ref[0, :, 256 * i:256 * (i + 1)]
                k = k_ref[0, :nkeys, 256 * kv:256 * (kv + 1)]
                sa, sb = _dot_t(q[:, :LANES], k[:, :LANES]), _dot_t(q[:, LANES:], k[:, LANES:])
            else:
                qa, qb = _split_heads(q_ref[0, :, LANES * i:LANES * (i + 1)])
                k = k_ref[0, :nkeys, LANES * kv:LANES * (kv + 1)]
                sa, sb = _dot_t(qa, k), _dot_t(qb, k)
            o_ref[0, :, LANES * i:LANES * (i + 1)] = _softmax_pv(sa, sb, v, shift).astype(BF16)

    small = bound_ref[0] <= SCORE_BOUND

    @pl.when(t == 0)
    def _():
        run(CTX, True)

    @pl.when((t > 0) & small)
    def _():
        run(NT, False)

    @pl.when((t > 0) & jnp.logical_not(small))
    def _():
        run(NT, True)


def _pair_attention(bound, q, k, v, *, wide, q_pairs_per_kv):
    bsz = q.shape[0]
    qw = 256 if wide else LANES
    n_steps = q.shape[2] // (qw * DENSE_PAIRS_PER_STEP)
    kv_pairs = DENSE_PAIRS_PER_STEP // q_pairs_per_kv
    return pl.pallas_call(
        functools.partial(_pair_attn_kernel, wide=wide, q_pairs_per_kv=q_pairs_per_kv),
        grid=(bsz, n_steps, NTILES),
        in_specs=[pl.BlockSpec(memory_space=pltpu.SMEM),
                  pl.BlockSpec((1, TM, qw * DENSE_PAIRS_PER_STEP), lambda b, p, t: (b, t, p)),
                  pl.BlockSpec((1, NT, qw * kv_pairs), lambda b, p, t: (b, 0, p)),
                  pl.BlockSpec((1, NT, LANES * kv_pairs), lambda b, p, t: (b, 0, p))],
        out_specs=pl.BlockSpec((1, TM, LANES * DENSE_PAIRS_PER_STEP), lambda b, p, t: (b, t, p)),
        out_shape=jax.ShapeDtypeStruct((bsz, NT, n_steps * DENSE_PAIRS_PER_STEP * LANES), BF16),
        compiler_params=_cparams(3), name="pair_attention",
    )(bound, q, k, v)


def _na_attn_kernel(bound_ref, q_ref, kc_ref, k0_ref, k1_ref, k2_ref, vc_ref, v0_ref, v1_ref, v2_ref, bias_ref,
                    o_ref):
    t = pl.program_id(1)
    small = bound_ref[0] <= SCORE_BOUND

    @pl.when(t == 0)
    def _():
        for i in range(PAIRS_PER_STEP):
            sl = slice(LANES * i, LANES * (i + 1))
            qa, qb = _split_heads(q_ref[0, :, sl])
            k = kc_ref[0, :, sl]
            o_ref[0, :, sl] = _softmax_pv(_dot_t(qa, k), _dot_t(qb, k), vc_ref[0, :, sl]).astype(BF16)

    def run(shift):
        for i in range(PAIRS_PER_STEP):
            sl = slice(LANES * i, LANES * (i + 1))
            qa, qb = _split_heads(q_ref[0, :, sl])
            k = jnp.concatenate([k0_ref[0, :, sl], k1_ref[0, :, sl], k2_ref[0, :, sl], kc_ref[0, :, sl]], axis=0)
            v = jnp.concatenate([v0_ref[0, :, sl], v1_ref[0, :, sl], v2_ref[0, :, sl], vc_ref[0, :, sl]], axis=0)
            sa = _dot_t(qa, k) + bias_ref[0, 2 * i]
            sb = _dot_t(qb, k) + bias_ref[0, 2 * i + 1]
            o_ref[0, :, sl] = _softmax_pv(sa, sb, v, shift).astype(BF16)

    @pl.when((t > 0) & small)
    def _():
        run(False)

    @pl.when((t > 0) & jnp.logical_not(small))
    def _():
        run(True)


def _na_band_tile(t):
    return 1 + jnp.clip(t - 2, 0, LAT_TILES - BAND_TILES)


def _na_attention(bound, q, k, v, bias):
    bsz = q.shape[0]
    n_steps = C_HEADS // (2 * PAIRS_PER_STEP)
    width = LANES * PAIRS_PER_STEP
    qspec = pl.BlockSpec((1, TM, width), lambda p, t, b: (b, t, p))
    cspec = pl.BlockSpec((1, TM, width), lambda p, t, b: (b, 0, p))
    bands = [pl.BlockSpec((1, TM, width), functools.partial(lambda p, t, b, i: (b, _na_band_tile(t) + i, p), i=i))
             for i in range(BAND_TILES)]
    variant = lambda t: jnp.where(t <= 1, 0, jnp.where(t == LAT_TILES, 2, 1))
    return pl.pallas_call(
        _na_attn_kernel,
        grid=(n_steps, NTILES, bsz),
        in_specs=[pl.BlockSpec(memory_space=pltpu.SMEM), qspec, cspec] + bands + [cspec] + bands
                 + [pl.BlockSpec((1, 2 * PAIRS_PER_STEP, TM, (BAND_TILES + 1) * TM),
                                 lambda p, t, b: (variant(t), p, 0, 0))],
        out_specs=qspec,
        out_shape=jax.ShapeDtypeStruct((bsz, NT, C_HEADS * C_HEAD_DIM), BF16),
        compiler_params=_cparams(3), name="na_attention",
    )(bound, q, k, k, k, k, v, v, v, v, bias)


def _out_router_kernel(*refs, n_x):
    x_refs = refs[:n_x]
    o_ref, wo_ref, mod_ref, g2_ref, r_ref, xo_ref, h_ref, aff_ref = refs[n_x:]
    attn = _dot(_rows_of(o_ref), wo_ref[...])
    hs = []
    for s in range(xo_ref.shape[0]):
        mod = mod_ref[s, 0]
        x = _stream_tile(x_refs, s) + mod[2:3] * attn[s * TM:(s + 1) * TM]
        xo_ref[s] = x
        h = _prenorm(x, g2_ref[...], mod[3:4], mod[4:5]).astype(BF16)
        for r, word in enumerate(_pack_rows(h)):
            h_ref[s, pl.ds(r, TM, stride=HS_ROWS), :] = word
        hs.append(h)
    logits = _dot(jnp.concatenate(hs, axis=0), r_ref[...])
    lane = lax.broadcasted_iota(jnp.int32, (1, LANES), 1)
    logits = jnp.where(lane < N_EXPERTS, logits, NEG_INF)
    e = jnp.exp(logits - jnp.max(logits, axis=-1, keepdims=True))
    _store_rows(aff_ref, slice(None), e / jnp.sum(e, axis=-1, keepdims=True))


def _out_router(o, wo, x, modtab, g2, router):
    bsz = modtab.shape[0]
    n = _samples_per_step(bsz)
    x_specs, x_args = _stream_in(x, n)
    return pl.pallas_call(
        functools.partial(_out_router_kernel, n_x=len(x_args)),
        grid=(bsz // n, NTILES),
        in_specs=x_specs + [_row_spec(D, n), _const_spec((D, D)), _mod_spec(n), _const_spec((1, D)),
                            _const_spec((D, LANES))],
        out_specs=[_row_spec(D, n), pl.BlockSpec((n, TM * HS_ROWS, LANES), lambda b, t: (b, t, 0)),
                   _row_spec(LANES, n)],
        out_shape=[jax.ShapeDtypeStruct((bsz, NT, D), F32),
                   jax.ShapeDtypeStruct((bsz, NT * HS_ROWS, LANES), jnp.int32),
                   jax.ShapeDtypeStruct((bsz, NT, LANES), F32)],
        input_output_aliases={0: 0} if len(x_args) == 1 else {},
        compiler_params=_cparams(2), name="out_router",
    )(*x_args, o, wo, modtab, g2, router)


def _excl_cumsum_lanes(m, triu):
    out = []
    offset = jnp.zeros((m.shape[0], 1), F32)
    for i in range(m.shape[1] // TM):
        blk = m[:, i * TM:(i + 1) * TM]
        out.append(_dot(blk.astype(BF16), triu) + offset)
        offset = offset + jnp.sum(blk, axis=1, keepdims=True)
    return jnp.concatenate(out, axis=1) if len(out) > 1 else out[0]


def _kth_largest_bits(bits_list, caps):
    def count(bits, cand):
        return jnp.sum(jnp.where(bits >= cand, 1.0, 0.0), axis=1, keepdims=True)

    def step(i, thrs):
        b1 = jnp.left_shift(jnp.int32(1), 29 - 2 * i)
        b0 = jnp.left_shift(jnp.int32(1), 28 - 2 * i)
        out = []
        for bits, cap, thr in zip(bits_list, caps, thrs):
            c1, c2, c3 = count(bits, thr | b1), count(bits, thr | b0), count(bits, thr | b1 | b0)
            with_b1 = jnp.where(c3 >= cap, thr | b1 | b0, thr | b1)
            without = jnp.where(c2 >= cap, thr | b0, thr)
            out.append(jnp.where(c1 >= cap, with_b1, without))
        return tuple(out)

    top = jnp.int32(1 << 30)
    init = tuple(jnp.where(count(bits, top) >= cap, top, jnp.zeros((N_EXPERTS, 1), jnp.int32))
                 for bits, cap in zip(bits_list, caps))
    return lax.fori_loop(0, 15, step, init)


def _topk_slots(bits, thr, cap, triu):
    gt = bits > thr
    eq = bits == thr
    need = cap - jnp.sum(jnp.where(gt, 1.0, 0.0), axis=1, keepdims=True)
    eq_rank = _excl_cumsum_lanes(jnp.where(eq, 1.0, 0.0), triu)
    sel = gt | (eq & (eq_rank < need))
    rank = _excl_cumsum_lanes(jnp.where(sel, 1.0, 0.0), triu)
    return jnp.where(sel, rank, -1.0), rank


OFF_LANE = 3 * N_EXPERTS


def _token_values(aff, first_row):
    n = aff.shape[0]
    hi = aff.astype(BF16).astype(F32)
    r1 = aff - hi
    mid = r1.astype(BF16).astype(F32)
    lo = (r1 - mid).astype(BF16).astype(F32)
    off = (lax.broadcasted_iota(jnp.int32, (n, 1), 0) + first_row) * HS_ROWS
    lane = lax.broadcasted_iota(jnp.int32, (1, LANES), 1)
    vals = (hi + pltpu.roll(mid, N_EXPERTS, axis=1) + pltpu.roll(lo, 2 * N_EXPERTS, axis=1)
            + jnp.where(lane == OFF_LANE, (off >> 7).astype(F32), 0.0)
            + jnp.where(lane == OFF_LANE + 1, (off & (LANES - 1)).astype(F32), 0.0))
    return vals.astype(BF16)


def _topk_kernel(aff_ref, slot_ref, off_ref, gate_ref, win_ref):
    r = lax.broadcasted_iota(jnp.int32, (TM, TM), 0)
    c = lax.broadcasted_iota(jnp.int32, (TM, TM), 1)
    triu = jnp.where(r < c, 1.0, 0.0).astype(BF16)
    lane = lax.broadcasted_iota(jnp.int32, (1, LANES), 1)
    pad_rows = lambda x, fill: jnp.concatenate(
        [x, jnp.full((LANES - x.shape[0], x.shape[1]), fill, x.dtype)], axis=0) if x.shape[0] < LANES else x
    aff_t = aff_ref[0].T[:N_EXPERTS]
    segments = ((0, CTX, CAP_CTX, CAP_LAT), (CTX, NT, CAP_LAT, 0))
    bits = [pltpu.bitcast(aff_t[:, lo:hi], jnp.int32) for lo, hi, _, _ in segments]
    thrs = _kth_largest_bits(bits, [cap for _, _, cap, _ in segments])
    for (lo, hi, cap, dst), seg_bits, thr in zip(segments, bits, thrs):
        slot, rank = _topk_slots(seg_bits, thr, cap, triu)
        slot_ref[0, lo:hi] = pad_rows(slot, -1.0).T.astype(jnp.int32)
        if lo == CTX:
            starts = jnp.where(lane == LAT_TILES, float(cap), 0.0)
            for j in range(LAT_TILES):
                starts = starts + jnp.where(lane == j, rank[:, TM * j:TM * j + 1], 0.0)
            run = pad_rows(starts, 0.0).T.astype(jnp.int32)
            first = jnp.minimum(run[:LAT_TILES] & -BF16_ROWS, CAP_LAT - WIN)
            ends = pltpu.roll(run[1:LAT_TILES + 1], N_EXPERTS, axis=1)
            win = jnp.where(lane < N_EXPERTS, first, ends)
            for j in range(LAT_TILES):
                win_ref[0, j] = win[j:j + 1]
        vals = _token_values(aff_ref[0, lo:hi], lo)
        slot_ids = lax.broadcasted_iota(jnp.int32, (cap, 1), 0).astype(F32)
        for e in range(N_EXPERTS):
            hit = jnp.where(slot_ids == slot[e:e + 1, :], 1.0, 0.0).astype(BF16)
            picked = _dot(hit, vals)
            mine = ((lane & (N_EXPERTS - 1)) == e) & (lane < OFF_LANE)
            gate_ref[0, e, dst:dst + cap, :] = jnp.sum(jnp.where(mine, picked, 0.0), axis=1, keepdims=True)
            by_lane = pad_rows(picked, 0.0).T
            off = by_lane[OFF_LANE:OFF_LANE + 1] * float(LANES) + by_lane[OFF_LANE + 1:OFF_LANE + 2]
            off_ref[0, e, :, dst:dst + cap] = off[:, :cap].astype(jnp.int32)


def _topk(aff):
    bsz = aff.shape[0]
    tspec = pl.BlockSpec((1, NT, LANES), lambda b: (b, 0, 0))
    return pl.pallas_call(
        _topk_kernel, grid=(bsz,),
        in_specs=[tspec],
        out_specs=[tspec, pl.BlockSpec((1, N_EXPERTS, 1, CAP_ALL), lambda b: (b, 0, 0, 0)),
                   pl.BlockSpec((1, N_EXPERTS, CAP_ALL, 1), lambda b: (b, 0, 0, 0)),
                   pl.BlockSpec((1, LAT_TILES, 1, LANES), lambda b: (b, 0, 0, 0))],
        out_shape=[jax.ShapeDtypeStruct((bsz, NT, LANES), jnp.int32),
                   jax.ShapeDtypeStruct((bsz, N_EXPERTS, 1, CAP_ALL), jnp.int32),
                   jax.ShapeDtypeStruct((bsz, N_EXPERTS, CAP_ALL, 1), F32),
                   jax.ShapeDtypeStruct((bsz, LAT_TILES, 1, LANES), jnp.int32)],
        compiler_params=_cparams(1), name="expert_topk",
    )(aff)


def _pack_rows(hb):
    words = []
    for r in range(HS_ROWS):
        lo = pltpu.bitcast(hb[:, 256 * r:256 * r + LANES].astype(F32), jnp.int32)
        hi = pltpu.bitcast(hb[:, 256 * r + LANES:256 * (r + 1)].astype(F32), jnp.int32)
        words.append(lax.shift_right_logical(lo, jnp.int32(16)) | hi)
    return words


def _unpack_rows(words):
    cols = []
    for w in words:
        cols.append(pltpu.bitcast(lax.shift_left(w, jnp.int32(16)), F32).astype(BF16))
        cols.append(pltpu.bitcast(w & jnp.int32(-65536), F32).astype(BF16))
    return jnp.concatenate(cols, axis=1)


def _gather_slots(hs_ref, off_ref, tile_ref, buf, n_slots):
    for j in range(n_slots):
        src = pl.multiple_of(off_ref[0, 0, 0, j], HS_ROWS)
        tile_ref[buf, pl.ds(j, HS_ROWS, stride=TILE_STRIDE), :] = hs_ref[0, pl.ds(src, HS_ROWS), :]


def _expert_kernel(hs0_ref, off0_ref, hsn_ref, offn_ref, gate_ref, wg_ref, wu_ref, wd_ref, *refs, n_slots):
    out_refs, (tile_ref, wg_s, wu_s, wd_s) = refs[:-4], refs[-4:]
    step = pl.program_id(0) * pl.num_programs(1) + pl.program_id(1)
    cur = step % 2

    @pl.when(step == 0)
    def _():
        _gather_slots(hs0_ref, off0_ref, tile_ref, 0, n_slots)

    @pl.when(pl.program_id(1) == 0)
    def _():
        wg_s[...] = wg_ref[0, 0].astype(BF16)
        wu_s[...] = wu_ref[0, 0].astype(BF16)
        wd_s[...] = wd_ref[0, 0].astype(BF16)

    xg = _unpack_rows([tile_ref[cur, r * TILE_STRIDE:r * TILE_STRIDE + n_slots, :] for r in range(HS_ROWS)])
    _gather_slots(hsn_ref, offn_ref, tile_ref, 1 - cur, n_slots)
    hid = (_silu(_dot(xg, wg_s[...])) * _dot(xg, wu_s[...])).astype(BF16)
    y = _dot(hid, wd_s[...]) * gate_ref[0, 0, :n_slots]
    out_refs[0][0] = y[:CAP_LAT].astype(BF16)
    if n_slots > CAP_LAT:
        out_refs[1][0] = y[CAP_LAT:].astype(BF16)


def _experts(hs, off, gate, wg, wu, wd, layer, with_ctx):
    bsz = hs.shape[0]
    n_slots = CAP_ALL if with_ctx else CAP_LAT
    wspec = pl.BlockSpec((1, 1, D, EXPERT_FF), lambda e, b: (layer, e, 0, 0))
    out_specs = [pl.BlockSpec((1, CAP_LAT, D), lambda e, b: (b, e, 0))]
    out_shape = [jax.ShapeDtypeStruct((bsz, N_EXPERTS * CAP_LAT, D), BF16)]
    if with_ctx:
        out_specs.append(pl.BlockSpec((1, CAP_CTX, D), lambda e, b: (b, e, 0)))
        out_shape.append(jax.ShapeDtypeStruct((bsz, N_EXPERTS * CAP_CTX, D), BF16))

    def nxt(e, b):
        i = jnp.minimum(e * bsz + b + 1, N_EXPERTS * bsz - 1)
        return i % bsz, i // bsz

    hs_block, off_block = (1, NT * HS_ROWS, LANES), (1, 1, 1, CAP_ALL)
    return pl.pallas_call(
        functools.partial(_expert_kernel, n_slots=n_slots),
        grid=(N_EXPERTS, bsz),
        in_specs=[pl.BlockSpec(hs_block, lambda e, b: (0, 0, 0), pipeline_mode=pl.Buffered(1)),
                  pl.BlockSpec(off_block, lambda e, b: (0, 0, 0, 0), memory_space=pltpu.SMEM),
                  pl.BlockSpec(hs_block, lambda e, b: (nxt(e, b)[0], 0, 0)),
                  pl.BlockSpec(off_block, lambda e, b: (*nxt(e, b), 0, 0), memory_space=pltpu.SMEM),
                  pl.BlockSpec((1, 1, CAP_ALL, 1), lambda e, b: (b, e, 0, 0)),
                  wspec, wspec, pl.BlockSpec((1, 1, EXPERT_FF, D), lambda e, b: (layer, e, 0, 0))],
        out_specs=out_specs, out_shape=out_shape,
        scratch_shapes=[pltpu.VMEM((2, HS_ROWS * TILE_STRIDE, LANES), jnp.int32),
                        pltpu.VMEM((D, EXPERT_FF), BF16), pltpu.VMEM((D, EXPERT_FF), BF16),
                        pltpu.VMEM((EXPERT_FF, D), BF16)],
        compiler_params=_cparams(2), name="experts",
    )(hs, off, hs, off, gate, wg, wu, wd)


WIN = 64
WIN_GROUP = 256 // WIN


def _combine_lat(slot_ref, yl_ref, win_ref, x_ref, gate, xo_ref):
    slot = slot_ref[0]
    fits = win_ref[0, 0, 0, N_EXPERTS] - win_ref[0, 0, 0, 0] <= WIN
    for e in range(1, N_EXPERTS):
        fits = jnp.logical_and(fits, win_ref[0, 0, 0, N_EXPERTS + e] - win_ref[0, 0, 0, e] <= WIN)

    @pl.when(fits)
    def _():
        cols = lax.broadcasted_iota(jnp.int32, (TM, WIN * WIN_GROUP), 1)
        acc = None
        for g in range(N_EXPERTS // WIN_GROUP):
            hit = cols < 0
            rows = []
            for j in range(WIN_GROUP):
                e = WIN_GROUP * g + j
                start = win_ref[0, 0, 0, e]
                s = slot[:, e:e + 1]
                hit = hit | (jnp.where(s >= 0, s - start + WIN * j, -1) == cols)
                rows.append(yl_ref[0, pl.ds(pl.multiple_of(e * CAP_LAT + start, BF16_ROWS), WIN), :])
            part = _dot(jnp.where(hit, 1.0, 0.0).astype(BF16), jnp.concatenate(rows, axis=0))
            acc = part if acc is None else acc + part
        xo_ref[0] = x_ref[0] + gate * acc

    @pl.when(jnp.logical_not(fits))
    def _():
        cols = lax.broadcasted_iota(jnp.int32, (TM, CAP_LAT), 1)
        onehot = jnp.concatenate(
            [jnp.where(slot[:, e:e + 1] == cols, 1.0, 0.0).astype(BF16) for e in range(N_EXPERTS)], axis=1)
        xo_ref[0] = x_ref[0] + gate * _dot(onehot, yl_ref[0])


def _combine_kernel(slot_ref, yl_ref, yc_ref, win_ref, x_ref, mod_ref, xo_ref):
    t = pl.program_id(1)
    gate = mod_ref[0, 0][5:6]

    @pl.when(t == 0)
    def _():
        slot = slot_ref[0]
        cols = lax.broadcasted_iota(jnp.int32, (TM, N_EXPERTS * CAP_CTX), 1)
        hit = cols < 0
        for e in range(N_EXPERTS):
            s = slot[:, e:e + 1]
            hit = hit | (jnp.where(s >= 0, s + e * CAP_CTX, -1) == cols)
        xo_ref[0] = x_ref[0] + gate * _dot(jnp.where(hit, 1.0, 0.0).astype(BF16), yc_ref[0])

    @pl.when(t > 0)
    def _():
        _combine_lat(slot_ref, yl_ref, win_ref, x_ref, gate, xo_ref)


def _win_spec(tile_of_step):
    return pl.BlockSpec((1, 1, 1, LANES), lambda b, t: (b, tile_of_step(t), 0, 0), memory_space=pltpu.SMEM)


def _combine(slot, yl, yc, win, x, modtab):
    bsz = x.shape[0]
    return pl.pallas_call(
        _combine_kernel,
        grid=(bsz, NTILES),
        in_specs=[_row_spec(LANES), pl.BlockSpec((1, N_EXPERTS * CAP_LAT, D), lambda b, t: (b, 0, 0)),
                  pl.BlockSpec((1, N_EXPERTS * CAP_CTX, D), lambda b, t: (b, 0, 0)),
                  _win_spec(lambda t: jnp.maximum(t - 1, 0)), _row_spec(D), _mod_spec()],
        out_specs=_row_spec(D),
        out_shape=jax.ShapeDtypeStruct((bsz, NT, D), F32),
        input_output_aliases={4: 0},
        compiler_params=_cparams(2), name="moe_combine",
    )(slot, yl, yc, win, x, modtab)


def _combine_last_kernel(slot_ref, yl_ref, win_ref, x_ref, mod_ref, xo_ref):
    _combine_lat(slot_ref, yl_ref, win_ref, x_ref, mod_ref[0, 0][5:6], xo_ref)


def _combine_last(slot, yl, win, x, modtab):
    bsz = x.shape[0]
    lat = lambda width: pl.BlockSpec((1, TM, width), lambda b, t: (b, t + 1, 0))
    return pl.pallas_call(
        _combine_last_kernel,
        grid=(bsz, LAT_TILES),
        in_specs=[lat(LANES), pl.BlockSpec((1, N_EXPERTS * CAP_LAT, D), lambda b, t: (b, 0, 0)),
                  _win_spec(lambda t: t), lat(D), pl.BlockSpec((1, 1, 6, D), lambda b, t: (b, 1, 0, 0))],
        out_specs=pl.BlockSpec((1, TM, D), lambda b, t: (b, t, 0)),
        out_shape=jax.ShapeDtypeStruct((bsz, SEQ, D), F32),
        compiler_params=_cparams(2), name="moe_combine_last",
    )(slot, yl, win, x, modtab)


def _rope_tables(rot_dim, width, lane_lo, lane_hi):
    n_freq = rot_dim // 4
    half = rot_dim // 2
    inv = jnp.float32(ROPE_THETA) ** (-jnp.arange(n_freq, dtype=F32) / n_freq)
    t = jnp.arange(SEQ, dtype=jnp.int32)
    row = (t // GRID_W).astype(F32)
    col = (t % GRID_W).astype(F32)
    ang = jnp.concatenate([row[:, None] * inv, col[:, None] * inv], axis=-1)
    cos, sin = jnp.cos(ang), jnp.sin(ang)
    lane = np.arange(width)
    inside = (lane % LANES >= lane_lo) & (lane % LANES < lane_hi)
    idx = lane % half
    sign = np.where(lane % rot_dim < half, -1.0, 1.0).astype(np.float32)
    cosw = jnp.where(inside[None, :], cos[:, idx], 1.0)
    sinw = jnp.where(inside[None, :], sin[:, idx] * sign[None, :], 0.0)
    ones = jnp.ones((CTX, width), F32)
    return jnp.concatenate([ones, cosw], axis=0), jnp.concatenate([0.0 * ones, sinw], axis=0)


def _block_diag_ones():
    i = np.arange(256)
    return jnp.asarray((i[:, None] // 64) == (i[None, :] // 64), dtype=BF16)


_GQA_ORDER = np.array([8 * kp + 4 * odd + i for kp in range(2) for i in range(4) for odd in range(2)])


def _na_bias(rpb):
    out = []
    rows = SEQ // GRID_W
    qrows, krows = TM // GRID_W, BAND_TILES * TM // GRID_W
    col = np.arange(GRID_W)
    cs = np.clip(col - NA_COLS // 2, 0, GRID_W - NA_COLS)
    ok_c = (col[None, :] >= cs[:, None]) & (col[None, :] < cs[:, None] + NA_COLS)
    dc = np.clip(col[None, :] - col[:, None] + NA_COLS - 1, 0, 2 * NA_COLS - 2)
    pick_c = (dc[:, :, None] == np.arange(2 * NA_COLS - 1)).astype(np.float32)
    n_dc = 2 * NA_COLS - 1
    pick_c2 = np.zeros((GRID_W, 2, GRID_W, 2, n_dc), np.float32)
    for rr in range(2):
        pick_c2[:, rr, :, rr, :] = pick_c
    pick_c2 = pick_c2.reshape(GRID_W, 2 * GRID_W, 2 * n_dc)
    for r0, bs in ((0, 0), (8, 4), (rows - 4, rows - 12)):
        r, kr = r0 + np.arange(qrows), bs + np.arange(krows)
        rs = np.clip(r - NA_ROWS // 2, 0, rows - NA_ROWS)
        ok_r = (kr[None, :] >= rs[:, None]) & (kr[None, :] < rs[:, None] + NA_ROWS)
        dr = kr[None, :] - r[:, None] + NA_ROWS - 1
        pick_r = (dr[:, :, None] == np.arange(2 * NA_ROWS - 1)).astype(np.float32)
        t1 = jnp.einsum('ard,hde->hare', pick_r, rpb.astype(F32) * LOG2E, precision=lax.Precision.HIGHEST)
        t2 = t1.reshape(C_HEADS, qrows, krows // 2, 2 * n_dc)
        band = jnp.einsum('hapf,cuf->hacpu', t2, pick_c2, precision=lax.Precision.HIGHEST)
        ok = (ok_r[:, None, :, None] & ok_c[None, :, None, :]).reshape(1, qrows, GRID_W, krows // 2, 2 * GRID_W)
        band = jnp.where(ok, band, NEG_INF).reshape(C_HEADS, TM, BAND_TILES * TM)
        out.append(jnp.concatenate([band, jnp.zeros((C_HEADS, TM, TM), F32)], axis=-1))
    return jnp.stack(out)


def _gqa_params(w_qkv, qn, kn, wo):
    nq, nk = A_HEADS * A_HEAD_DIM, A_KV_HEADS * A_HEAD_DIM
    wq = w_qkv[:, :nq].reshape(D, A_HEADS, A_HEAD_DIM)[:, _GQA_ORDER].reshape(D, nq)
    w = jnp.concatenate([wq, w_qkv[:, nq:]], axis=1).astype(BF16)
    wo_p = wo.reshape(A_HEADS, A_HEAD_DIM, D)[_GQA_ORDER].reshape(nq, D).astype(BF16)
    scale = A_HEAD_DIM ** -0.5 * LOG2E
    return w, (jnp.tile(qn, 4) * scale)[None], jnp.tile(kn, 4)[None], wo_p


def _mla_params(w_dq, qn_lat, w_uq, w_dkv, kvn_lat, w_ukv, qn, kn, cossin_q, cossin_k):
    qd = B_NOPE + B_ROPE
    w1 = jnp.concatenate([w_dq, w_dkv, jnp.zeros((D, 768 - B_Q_RANK - B_KV_RANK - B_ROPE), F32)], axis=1)
    pad_q = jnp.zeros((B_Q_RANK, B_HEADS, LANES - qd), F32)
    wuq = jnp.concatenate([w_uq.reshape(B_Q_RANK, B_HEADS, qd), pad_q], axis=2).reshape(B_Q_RANK, B_HEADS * LANES)
    ukv = w_ukv.reshape(B_KV_RANK, B_HEADS, B_NOPE + B_V)
    wk_top = jnp.concatenate([ukv[:, :, :B_NOPE], jnp.zeros((B_KV_RANK, B_HEADS, LANES - B_NOPE), F32)], axis=2)
    route = np.zeros((LANES, B_HEADS, LANES), np.float32)
    route[np.arange(B_ROPE), :, B_NOPE + np.arange(B_ROPE)] = 1.0
    wk = jnp.concatenate([wk_top.reshape(B_KV_RANK, -1), jnp.asarray(route).reshape(LANES, -1)], axis=0)
    wv = ukv[:, :, B_NOPE:].reshape(B_KV_RANK, B_HEADS * B_V)
    scale = qd ** -0.5 * LOG2E
    zpad = jnp.zeros((LANES - qd,), F32)
    qgain = jnp.tile(jnp.concatenate([qn * scale, zpad]), 2)[None]
    qinv = jnp.tile(jnp.concatenate([jnp.full((B_NOPE,), 1.0 / B_NOPE), jnp.full((LANES - B_NOPE,), 1.0 / B_ROPE)]), 2)
    kgain = jnp.tile(jnp.concatenate([kn[:B_NOPE], jnp.zeros((LANES - B_NOPE,), F32)]), 2)[None]
    kmask = jnp.tile(jnp.concatenate([jnp.ones((B_NOPE,), F32), jnp.zeros((LANES - B_NOPE,), F32)]), 2)[None]
    gkr = jnp.concatenate([kn[B_NOPE:], jnp.zeros((LANES - B_ROPE,), F32)])[None]
    return dict(w1=w1.astype(BF16), gql=qn_lat[None], gkl=kvn_lat[None], gkr=gkr, kcos=cossin_k[0], ksin=cossin_k[1],
                wuq=wuq.astype(BF16), wk=wk.astype(BF16), wv=wv.astype(BF16), bd=_block_diag_ones(),
                qgain=qgain, qinv=qinv[None].astype(F32), qcos=cossin_q[0], qsin=cossin_q[1], kgain=kgain, kmask=kmask)


def kernel(x, c, ctx, c_ctx, ada_w, ada_b, norm1_w, norm2_w, a_wqkv, a_qnorm, a_knorm, a_wo, b_wdq, b_qnorm_lat, b_wuq, b_wdkv, b_kvnorm_lat, b_wukv, b_qnorm, b_knorm, b_wo, c_wqkv, c_qnorm, c_knorm, c_rpb, c_wo, moe_router, moe_wg, moe_wu, moe_wd):
    bsz = x.shape[0]
    assert x.shape[1:] == (SEQ, D) and ctx.shape[1:] == (CTX, D)
    mod_rows = -(-(bsz + 1) // 16) * 16
    cvec = jnp.concatenate([c, c_ctx[None], jnp.zeros((mod_rows - bsz - 1, D), F32)], axis=0)
    mods = _ada_all(cvec, ada_w, ada_b)
    xs = (ctx, x)

    bd = _block_diag_ones()
    rope64 = _rope_tables(A_HEAD_DIM, 256, 0, LANES)
    rope_mla_q = _rope_tables(B_ROPE, 256, B_NOPE, B_NOPE + B_ROPE)
    rope_mla_k = _rope_tables(B_ROPE, LANES, 0, B_ROPE)
    router = jnp.pad(moe_router, ((0, 0), (0, 0), (0, LANES - N_EXPERTS))).astype(BF16)

    for i in range(DEPTH):
        last = i == DEPTH - 1
        kind, j = i % 3, i // 3
        m = mods[i]
        m_lat = m[:bsz].reshape(bsz, 6, D)
        m_ctx = jnp.broadcast_to(m[bsz].reshape(1, 6, D), (bsz, 6, D))
        modtab = jnp.stack([m_ctx, m_lat], axis=1)
        g1, g2 = norm1_w[i][None], norm2_w[i][None]

        if kind == 0:
            w, gq, gk, wo = _gqa_params(a_wqkv[j], a_qnorm[j], a_knorm[j], a_wo[j])
            q, k, v = _qkv_project(xs, modtab, g1, w, bd, gq, gk, rope64[0], rope64[1],
                                   A_HEADS * A_HEAD_DIM, A_KV_HEADS * A_HEAD_DIM, True)
            bound = _score_bound(A_HEAD_DIM * jnp.max(gq * gq), A_HEAD_DIM * jnp.max(gk * gk))
            o = _pair_attention(bound, q, k, v, wide=False, q_pairs_per_kv=4)
        elif kind == 1:
            p = _mla_params(b_wdq[j], b_qnorm_lat[j], b_wuq[j], b_wdkv[j], b_kvnorm_lat[j], b_wukv[j],
                            b_qnorm[j], b_knorm[j], rope_mla_q, rope_mla_k)
            q, k, v = _mla_project(xs, modtab, g1, p)
            sq_max = lambda g: jnp.max(g * g)
            qg, kg = p["qgain"][0, :LANES], b_knorm[j]
            bound = _score_bound(B_NOPE * sq_max(qg[:B_NOPE]) + B_ROPE * sq_max(qg[B_NOPE:]),
                                 B_NOPE * sq_max(kg[:B_NOPE]) + B_ROPE * sq_max(kg[B_NOPE:]))
            o = _pair_attention(bound, q, k, v, wide=True, q_pairs_per_kv=1)
            wo = b_wo[j].astype(BF16)
        else:
            hd = C_HEADS * C_HEAD_DIM
            gq = (jnp.tile(c_qnorm[j], 4) * (C_HEAD_DIM ** -0.5 * LOG2E))[None]
            gk = jnp.tile(c_knorm[j], 4)[None]
            q, k, v = _qkv_project(xs, modtab, g1, c_wqkv[j].astype(BF16), bd, gq, gk, rope64[0], rope64[1],
                                   hd, hd, False)
            bound = _score_bound(C_HEAD_DIM * jnp.max(gq * gq), C_HEAD_DIM * jnp.max(gk * gk),
                                 extra=LOG2E * jnp.max(jnp.abs(c_rpb[j])))
            o = _na_attention(bound, q, k, v, _na_bias(c_rpb[j]))
            wo = c_wo[j].astype(BF16)

        xs, hs, aff = _out_router(o, wo, xs, modtab, g2, router[i])
        slot, off, gate, win = _topk(aff)
        ys = _experts(hs, off, gate,
                      moe_wg, moe_wu, moe_wd, layer=i, with_ctx=not last)
        if last:
            return _combine_last(slot, ys[0], win, xs, modtab)
        xs = _combine(slot, ys[0], ys[1], win, xs, modtab)
```

```python
import functools

import numpy as np
import jax
import jax.numpy as jnp
from jax import lax
from jax.experimental import pallas as pl
from jax.experimental.pallas import tpu as pltpu

F32 = jnp.float32
BF16 = jnp.bfloat16

D = 1024
SEQ = 2048
CTX = 256
NT = CTX + SEQ
DEPTH = 4
GRID_W = 64
ROPE_THETA = 10000.0
EPS = 1e-6
NEG_INF = -1e30
LOG2E = 1.4426950408889634

TM = 256
NTILES = NT // TM
LAT_TILES = SEQ // TM

A_HEADS, A_KV_HEADS, A_HEAD_DIM = 16, 4, 64
B_HEADS, B_Q_RANK, B_KV_RANK, B_NOPE, B_ROPE, B_V = 16, 384, 256, 64, 32, 64
C_HEADS, C_HEAD_DIM, NA_ROWS, NA_COLS = 16, 64, 8, 16
N_EXPERTS, EXPERT_FF = 16, 1024
CAP_LAT = 2 * SEQ // N_EXPERTS
CAP_CTX = 2 * CTX // N_EXPERTS
CAP_ALL = CAP_LAT + CAP_CTX
HS_ROWS = D // 256
TILE_STRIDE = 296
LANES = 128
BF16_ROWS = 16
BAND_TILES = 3

VMEM_LIMIT = 56 * 1024 * 1024


def _cparams(n_axes):
    return pltpu.CompilerParams(dimension_semantics=("arbitrary",) * n_axes, vmem_limit_bytes=VMEM_LIMIT)


def _dot(a, b):
    return jnp.dot(a, b, preferred_element_type=F32)


def _dot_t(a, b):
    return lax.dot_general(a, b, (((1,), (1,)), ((), ())), preferred_element_type=F32)


def _silu(x):
    return x / (1.0 + jnp.exp(-x))


def _prenorm(x, gain, shift, scale):
    ms = jnp.mean(x * x, axis=-1, keepdims=True)
    return (x * lax.rsqrt(ms + EPS) * gain) * (1.0 + scale) + shift


def _seg_rms(z, bd, inv_n):
    ss = _dot((z * z).astype(BF16), bd)
    return lax.rsqrt(ss * inv_n + EPS)


def _rope(z, cosw, sinw, first_half, half):
    w = z.shape[-1]
    up = pltpu.roll(z, w - half, axis=1)
    dn = pltpu.roll(z, half, axis=1)
    return z * cosw + jnp.where(first_half, up, dn) * sinw


def _ada_kernel(c_ref, w_ref, b_ref, o_ref):
    s = _silu(c_ref[...]).astype(BF16)
    o_ref[0] = _dot(s, w_ref[0].astype(BF16)) + b_ref[0]


def _ada_all(cvec, ada_w, ada_b):
    rows = cvec.shape[0]
    tn = 512
    return pl.pallas_call(
        _ada_kernel,
        grid=(DEPTH, 6 * D // tn),
        in_specs=[pl.BlockSpec((rows, D), lambda i, j: (0, 0)),
                  pl.BlockSpec((1, D, tn), lambda i, j: (i, 0, j)),
                  pl.BlockSpec((1, 1, tn), lambda i, j: (i, 0, j))],
        out_specs=pl.BlockSpec((1, rows, tn), lambda i, j: (i, 0, j)),
        out_shape=jax.ShapeDtypeStruct((DEPTH, rows, 6 * D), F32),
        compiler_params=_cparams(2), name="adaln",
    )(cvec, ada_w, ada_b.reshape(DEPTH, 1, 6 * D))


def _samples_per_step(bsz, most=2):
    return max(n for n in (1, 2, 4) if n <= most and bsz % n == 0)


def _mod_spec(n=1):
    return pl.BlockSpec((n, 1, 6, D), lambda b, t: (b, jnp.minimum(t, 1), 0, 0))


def _row_spec(width, n=1):
    return pl.BlockSpec((n, TM, width), lambda b, t: (b, t, 0))


def _rows_of(ref, prep=lambda s, x: x):
    return jnp.concatenate([prep(s, ref[s]) for s in range(ref.shape[0])], axis=0)


def _store_rows(ref, cols, val):
    for s in range(ref.shape[0]):
        ref[s, :, cols] = val[s * TM:(s + 1) * TM]


def _const_spec(shape):
    return pl.BlockSpec(shape, lambda b, t: (0,) * len(shape))


def _tab_spec(width):
    return pl.BlockSpec((TM, width), lambda b, t: (t, 0))


def _stream_in(x, n):
    if isinstance(x, tuple):
        return ([pl.BlockSpec((n, TM, D), lambda b, t: (b, 0, 0)),
                 pl.BlockSpec((n, TM, D), lambda b, t: (b, jnp.maximum(t - 1, 0), 0))], list(x))
    return [_row_spec(D, n)], [x]


def _stream_tile(x_refs, s):
    if len(x_refs) == 2:
        return jnp.where(pl.program_id(1) == 0, x_refs[0][s], x_refs[1][s])
    return x_refs[0][s]


def _qkv_kernel(*refs, nq, nk, rope, n_x):
    x_refs = refs[:n_x]
    mod_ref, g1_ref, w_ref, bd_ref, gq_ref, gk_ref, cos_ref, sin_ref, q_ref, k_ref, v_ref = refs[n_x:]
    n = q_ref.shape[0]
    g1 = g1_ref[...]
    h = jnp.concatenate([_prenorm(_stream_tile(x_refs, s), g1, mod_ref[s, 0][0:1], mod_ref[s, 0][1:2])
                         for s in range(n)], axis=0).astype(BF16)
    acc = _dot(h, w_ref[...])
    bd = bd_ref[...]
    lane = lax.broadcasted_iota(jnp.int32, (1, 256), 1)
    first_half = (lane % 64) < 32
    cosw, sinw = jnp.concatenate([cos_ref[...]] * n, axis=0), jnp.concatenate([sin_ref[...]] * n, axis=0)

    def finish(z, gain):
        z = z * _seg_rms(z, bd, 1.0 / 64) * gain
        if rope:
            z = _rope(z, cosw, sinw, first_half, 32)
        return z.astype(BF16)

    for j in range(nq // 256):
        cols = slice(j * 256, (j + 1) * 256)
        _store_rows(q_ref, cols, finish(acc[:, cols], gq_ref[...]))
    for j in range(nk // 256):
        cols = slice(j * 256, (j + 1) * 256)
        _store_rows(k_ref, cols, finish(acc[:, nq + j * 256:nq + (j + 1) * 256], gk_ref[...]))
    _store_rows(v_ref, slice(None), acc[:, nq + nk:].astype(BF16))


def _qkv_project(x, modtab, g1, w, bd, gq, gk, cosw, sinw, nq, nk, rope):
    bsz = modtab.shape[0]
    n = _samples_per_step(bsz, most=4 if nq + 2 * nk <= 1536 else 2)
    x_specs, x_args = _stream_in(x, n)
    return pl.pallas_call(
        functools.partial(_qkv_kernel, nq=nq, nk=nk, rope=rope, n_x=len(x_args)),
        grid=(bsz // n, NTILES),
        in_specs=x_specs + [_mod_spec(n), _const_spec((1, D)), _const_spec((D, nq + 2 * nk)),
                            _const_spec((256, 256)), _const_spec((1, 256)), _const_spec((1, 256)),
                            _tab_spec(256), _tab_spec(256)],
        out_specs=[_row_spec(nq, n), _row_spec(nk, n), _row_spec(nk, n)],
        out_shape=[jax.ShapeDtypeStruct((bsz, NT, nq), BF16), jax.ShapeDtypeStruct((bsz, NT, nk), BF16),
                   jax.ShapeDtypeStruct((bsz, NT, nk), BF16)],
        compiler_params=_cparams(2), name="qkv_project",
    )(*x_args, modtab, g1, w, bd, gq, gk, cosw, sinw)


def _mla_kernel(x_ref, mod_ref, g1_ref, w1_ref, gql_ref, gkl_ref, gkr_ref, kcos_ref, ksin_ref,
                wuq_ref, wk_ref, wv_ref, bd_ref, qgain_ref, qinv_ref, qcos_ref, qsin_ref, kgain_ref, kmask_ref,
                q_ref, k_ref, v_ref):
    n = x_ref.shape[0]
    g1 = g1_ref[...]
    tile = lambda ref: jnp.concatenate([ref[...]] * n, axis=0)
    h = _rows_of(x_ref, lambda s, x: _prenorm(x, g1, mod_ref[s, 0][0:1], mod_ref[s, 0][1:2])).astype(BF16)
    a = _dot(h, w1_ref[...])
    cq = a[:, :B_Q_RANK]
    cq = (cq * lax.rsqrt(jnp.mean(cq * cq, axis=-1, keepdims=True) + EPS) * gql_ref[...]).astype(BF16)
    ckv = a[:, B_Q_RANK:B_Q_RANK + B_KV_RANK]
    ckv = (ckv * lax.rsqrt(jnp.mean(ckv * ckv, axis=-1, keepdims=True) + EPS) * gkl_ref[...]).astype(BF16)
    kr = a[:, B_Q_RANK + B_KV_RANK:]
    kr = kr * lax.rsqrt(jnp.sum(kr * kr, axis=-1, keepdims=True) * (1.0 / B_ROPE) + EPS) * gkr_ref[...]
    lane128 = lax.broadcasted_iota(jnp.int32, (1, LANES), 1)
    kr = _rope(kr, tile(kcos_ref), tile(ksin_ref), (lane128 % 32) < 16, 16).astype(BF16)

    bd = bd_ref[...]
    lane = lax.broadcasted_iota(jnp.int32, (1, 256), 1)
    first_half = (lane % 32) < 16
    qcos, qsin = tile(qcos_ref), tile(qsin_ref)
    q2 = _dot(cq, wuq_ref[...])
    k2 = _dot(jnp.concatenate([ckv, kr], axis=1), wk_ref[...])
    kmask = kmask_ref[...] > 0.0
    for j in range(q2.shape[1] // 256):
        sl = slice(j * 256, (j + 1) * 256)
        z = q2[:, sl]
        z = z * _seg_rms(z, bd, qinv_ref[...]) * qgain_ref[...]
        _store_rows(q_ref, sl, _rope(z, qcos, qsin, first_half, 16).astype(BF16))
        z = k2[:, sl]
        _store_rows(k_ref, sl, (z * jnp.where(kmask, _seg_rms(z, bd, 1.0 / 64) * kgain_ref[...], 1.0)).astype(BF16))
    _store_rows(v_ref, slice(None), _dot(ckv, wv_ref[...]).astype(BF16))


def _mla_project(x, modtab, g1, p):
    bsz = x.shape[0]
    n = _samples_per_step(bsz)
    hq = B_HEADS * LANES
    return pl.pallas_call(
        _mla_kernel,
        grid=(bsz // n, NTILES),
        in_specs=[_row_spec(D, n), _mod_spec(n), _const_spec((1, D)), _const_spec((D, 768)),
                  _const_spec((1, B_Q_RANK)), _const_spec((1, B_KV_RANK)), _const_spec((1, LANES)),
                  _tab_spec(LANES), _tab_spec(LANES),
                  _const_spec((B_Q_RANK, hq)), _const_spec((B_KV_RANK + LANES, hq)),
                  _const_spec((B_KV_RANK, B_HEADS * B_V)), _const_spec((256, 256)),
                  _const_spec((1, 256)), _const_spec((1, 256)), _tab_spec(256), _tab_spec(256),
                  _const_spec((1, 256)), _const_spec((1, 256))],
        out_specs=[_row_spec(hq, n), _row_spec(hq, n), _row_spec(B_HEADS * B_V, n)],
        out_shape=[jax.ShapeDtypeStruct((bsz, NT, hq), BF16), jax.ShapeDtypeStruct((bsz, NT, hq), BF16),
                   jax.ShapeDtypeStruct((bsz, NT, B_HEADS * B_V), BF16)],
        compiler_params=_cparams(2), name="mla_project",
    )(x, modtab, g1, p["w1"], p["gql"], p["gkl"], p["gkr"], p["kcos"], p["ksin"], p["wuq"], p["wk"], p["wv"],
      p["bd"], p["qgain"], p["qinv"], p["qcos"], p["qsin"], p["kgain"], p["kmask"])


def _softmax_pv(sa, sb, v, shift=True):
    outs = []
    for s in (sa, sb):
        p = jnp.exp2(s - jnp.max(s, axis=-1, keepdims=True)) if shift else jnp.exp2(s)
        l = jnp.sum(p, axis=-1, keepdims=True)
        outs.append(_dot(p.astype(BF16), v) / l)
    lane = lax.broadcasted_iota(jnp.int32, (1, LANES), 1)
    return jnp.where(lane < 64, outs[0], outs[1])


def _split_heads(q):
    lane = lax.broadcasted_iota(jnp.int32, (1, LANES), 1)
    zero = jnp.zeros_like(q)
    return jnp.where(lane < 64, q, zero), jnp.where(lane >= 64, q, zero)


PAIRS_PER_STEP = 4


DENSE_PAIRS_PER_STEP = 8


SCORE_BOUND = 60.0


def _score_bound(q_gain2, k_gain2, extra=0.0):
    return (1.05 * jnp.sqrt(q_gain2 * k_gain2) + extra).reshape(1).astype(F32)


def _pair_attn_kernel(bound_ref, q_ref, k_ref, v_ref, o_ref, *, wide, q_pairs_per_kv):
    t = pl.program_id(2)

    def run(nkeys, shift):
        for i in range(DENSE_PAIRS_PER_STEP):
            kv = i // q_pairs_per_kv
            v = v_ref[0, :nkeys, LANES * kv:LANES * (kv + 1)]
            if wide:
                q = q_ref[0, :, 256 * i:256 * (i + 1)]
                k = k_ref[0, :nkeys, 256 * kv:256 * (kv + 1)]
                sa, sb = _dot_t(q[:, :LANES], k[:, :LANES]), _dot_t(q[:, LANES:], k[:, LANES:])
            else:
                qa, qb = _split_heads(q_ref[0, :, LANES * i:LANES * (i + 1)])
                k = k_ref[0, :nkeys, LANES * kv:LANES * (kv + 1)]
                sa, sb = _dot_t(qa, k), _dot_t(qb, k)
            o_ref[0, :, LANES * i:LANES * (i + 1)] = _softmax_pv(sa, sb, v, shift).astype(BF16)

    small = bound_ref[0] <= SCORE_BOUND

    @pl.when(t == 0)
    def _():
        run(CTX, True)

    @pl.when((t > 0) & small)
    def _():
        run(NT, False)

    @pl.when((t > 0) & jnp.logical_not(small))
    def _():
        run(NT, True)


def _pair_attention(bound, q, k, v, *, wide, q_pairs_per_kv):
    bsz = q.shape[0]
    qw = 256 if wide else LANES
    n_steps = q.shape[2] // (qw * DENSE_PAIRS_PER_STEP)
    kv_pairs = DENSE_PAIRS_PER_STEP // q_pairs_per_kv
    return pl.pallas_call(
        functools.partial(_pair_attn_kernel, wide=wide, q_pairs_per_kv=q_pairs_per_kv),
        grid=(bsz, n_steps, NTILES),
        in_specs=[pl.BlockSpec(memory_space=pltpu.SMEM),
                  pl.BlockSpec((1, TM, qw * DENSE_PAIRS_PER_STEP), lambda b, p, t: (b, t, p)),
                  pl.BlockSpec((1, NT, qw * kv_pairs), lambda b, p, t: (b, 0, p)),
                  pl.BlockSpec((1, NT, LANES * kv_pairs), lambda b, p, t: (b, 0, p))],
        out_specs=pl.BlockSpec((1, TM, LANES * DENSE_PAIRS_PER_STEP), lambda b, p, t: (b, t, p)),
        out_shape=jax.ShapeDtypeStruct((bsz, NT, n_steps * DENSE_PAIRS_PER_STEP * LANES), BF16),
        compiler_params=_cparams(3), name="pair_attention",
    )(bound, q, k, v)


def _na_attn_kernel(bound_ref, q_ref, kc_ref, k0_ref, k1_ref, k2_ref, vc_ref, v0_ref, v1_ref, v2_ref, bias_ref,
                    o_ref):
    t = pl.program_id(1)
    small = bound_ref[0] <= SCORE_BOUND

    @pl.when(t == 0)
    def _():
        for i in range(PAIRS_PER_STEP):
            sl = slice(LANES * i, LANES * (i + 1))
            qa, qb = _split_heads(q_ref[0, :, sl])
            k = kc_ref[0, :, sl]
            o_ref[0, :, sl] = _softmax_pv(_dot_t(qa, k), _dot_t(qb, k), vc_ref[0, :, sl]).astype(BF16)

    def run(shift):
        for i in range(PAIRS_PER_STEP):
            sl = slice(LANES * i, LANES * (i + 1))
            qa, qb = _split_heads(q_ref[0, :, sl])
            k = jnp.concatenate([k0_ref[0, :, sl], k1_ref[0, :, sl], k2_ref[0, :, sl], kc_ref[0, :, sl]], axis=0)
            v = jnp.concatenate([v0_ref[0, :, sl], v1_ref[0, :, sl], v2_ref[0, :, sl], vc_ref[0, :, sl]], axis=0)
            sa = _dot_t(qa, k) + bias_ref[0, 2 * i]
            sb = _dot_t(qb, k) + bias_ref[0, 2 * i + 1]
            o_ref[0, :, sl] = _softmax_pv(sa, sb, v, shift).astype(BF16)

    @pl.when((t > 0) & small)
    def _():
        run(False)

    @pl.when((t > 0) & jnp.logical_not(small))
    def _():
        run(True)


def _na_band_tile(t):
    return 1 + jnp.clip(t - 2, 0, LAT_TILES - BAND_TILES)


def _na_attention(bound, q, k, v, bias):
    bsz = q.shape[0]
    n_steps = C_HEADS // (2 * PAIRS_PER_STEP)
    width = LANES * PAIRS_PER_STEP
    qspec = pl.BlockSpec((1, TM, width), lambda p, t, b: (b, t, p))
    cspec = pl.BlockSpec((1, TM, width), lambda p, t, b: (b, 0, p))
    bands = [pl.BlockSpec((1, TM, width), functools.partial(lambda p, t, b, i: (b, _na_band_tile(t) + i, p), i=i))
             for i in range(BAND_TILES)]
    variant = lambda t: jnp.where(t <= 1, 0, jnp.where(t == LAT_TILES, 2, 1))
    return pl.pallas_call(
        _na_attn_kernel,
        grid=(n_steps, NTILES, bsz),
        in_specs=[pl.BlockSpec(memory_space=pltpu.SMEM), qspec, cspec] + bands + [cspec] + bands
                 + [pl.BlockSpec((1, 2 * PAIRS_PER_STEP, TM, (BAND_TILES + 1) * TM),
                                 lambda p, t, b: (variant(t), p, 0, 0))],
        out_specs=qspec,
        out_shape=jax.ShapeDtypeStruct((bsz, NT, C_HEADS * C_HEAD_DIM), BF16),
        compiler_params=_cparams(3), name="na_attention",
    )(bound, q, k, k, k, k, v, v, v, v, bias)


def _out_router_kernel(*refs, n_x):
    x_refs = refs[:n_x]
    o_ref, wo_ref, mod_ref, g2_ref, r_ref, xo_ref, h_ref, aff_ref = refs[n_x:]
    attn = _dot(_rows_of(o_ref), wo_ref[...])
    hs = []
    for s in range(xo_ref.shape[0]):
        mod = mod_ref[s, 0]
        x = _stream_tile(x_refs, s) + mod[2:3] * attn[s * TM:(s + 1) * TM]
        xo_ref[s] = x
        h = _prenorm(x, g2_ref[...], mod[3:4], mod[4:5]).astype(BF16)
        for r, word in enumerate(_pack_rows(h)):
            h_ref[s, pl.ds(r, TM, stride=HS_ROWS), :] = word
        hs.append(h)
    logits = _dot(jnp.concatenate(hs, axis=0), r_ref[...])
    lane = lax.broadcasted_iota(jnp.int32, (1, LANES), 1)
    logits = jnp.where(lane < N_EXPERTS, logits, NEG_INF)
    e = jnp.exp(logits - jnp.max(logits, axis=-1, keepdims=True))
    _store_rows(aff_ref, slice(None), e / jnp.sum(e, axis=-1, keepdims=True))


def _out_router(o, wo, x, modtab, g2, router):
    bsz = modtab.shape[0]
    n = _samples_per_step(bsz, most=4)
    x_specs, x_args = _stream_in(x, n)
    return pl.pallas_call(
        functools.partial(_out_router_kernel, n_x=len(x_args)),
        grid=(bsz // n, NTILES),
        in_specs=x_specs + [_row_spec(D, n), _const_spec((D, D)), _mod_spec(n), _const_spec((1, D)),
                            _const_spec((D, LANES))],
        out_specs=[_row_spec(D, n), pl.BlockSpec((n, TM * HS_ROWS, LANES), lambda b, t: (b, t, 0)),
                   _row_spec(LANES, n)],
        out_shape=[jax.ShapeDtypeStruct((bsz, NT, D), F32),
                   jax.ShapeDtypeStruct((bsz, NT * HS_ROWS, LANES), jnp.int32),
                   jax.ShapeDtypeStruct((bsz, NT, LANES), F32)],
        input_output_aliases={0: 0} if len(x_args) == 1 else {},
        compiler_params=_cparams(2), name="out_router",
    )(*x_args, o, wo, modtab, g2, router)


def _excl_cumsum_lanes(m, triu):
    out = []
    offset = jnp.zeros((m.shape[0], 1), F32)
    for i in range(m.shape[1] // TM):
        blk = m[:, i * TM:(i + 1) * TM]
        out.append(_dot(blk.astype(BF16), triu) + offset)
        offset = offset + jnp.sum(blk, axis=1, keepdims=True)
    return jnp.concatenate(out, axis=1) if len(out) > 1 else out[0]


def _kth_largest_bits(bits_list, caps):
    def count(bits, cand):
        return jnp.sum(jnp.where(bits >= cand, 1.0, 0.0), axis=1, keepdims=True)

    def step(i, thrs):
        b1 = jnp.left_shift(jnp.int32(1), 29 - 2 * i)
        b0 = jnp.left_shift(jnp.int32(1), 28 - 2 * i)
        out = []
        for bits, cap, thr in zip(bits_list, caps, thrs):
            c1, c2, c3 = count(bits, thr | b1), count(bits, thr | b0), count(bits, thr | b1 | b0)
            with_b1 = jnp.where(c3 >= cap, thr | b1 | b0, thr | b1)
            without = jnp.where(c2 >= cap, thr | b0, thr)
            out.append(jnp.where(c1 >= cap, with_b1, without))
        return tuple(out)

    top = jnp.int32(1 << 30)
    init = tuple(jnp.where(count(bits, top) >= cap, top, jnp.zeros((N_EXPERTS, 1), jnp.int32))
                 for bits, cap in zip(bits_list, caps))
    return lax.fori_loop(0, 15, step, init)


def _topk_slots(bits, thr, cap, triu):
    gt = bits > thr
    eq = bits == thr
    need = cap - jnp.sum(jnp.where(gt, 1.0, 0.0), axis=1, keepdims=True)
    eq_rank = _excl_cumsum_lanes(jnp.where(eq, 1.0, 0.0), triu)
    sel = gt | (eq & (eq_rank < need))
    rank = _excl_cumsum_lanes(jnp.where(sel, 1.0, 0.0), triu)
    return jnp.where(sel, rank, -1.0), rank


OFF_LANE = 3 * N_EXPERTS


def _token_values(aff, first_row):
    n = aff.shape[0]
    hi = aff.astype(BF16).astype(F32)
    r1 = aff - hi
    mid = r1.astype(BF16).astype(F32)
    lo = (r1 - mid).astype(BF16).astype(F32)
    off = (lax.broadcasted_iota(jnp.int32, (n, 1), 0) + first_row) * HS_ROWS
    lane = lax.broadcasted_iota(jnp.int32, (1, LANES), 1)
    vals = (hi + pltpu.roll(mid, N_EXPERTS, axis=1) + pltpu.roll(lo, 2 * N_EXPERTS, axis=1)
            + jnp.where(lane == OFF_LANE, (off >> 7).astype(F32), 0.0)
            + jnp.where(lane == OFF_LANE + 1, (off & (LANES - 1)).astype(F32), 0.0))
    return vals.astype(BF16)


def _topk_kernel(aff_ref, slot_ref, off_ref, gate_ref, win_ref):
    r = lax.broadcasted_iota(jnp.int32, (TM, TM), 0)
    c = lax.broadcasted_iota(jnp.int32, (TM, TM), 1)
    triu = jnp.where(r < c, 1.0, 0.0).astype(BF16)
    lane = lax.broadcasted_iota(jnp.int32, (1, LANES), 1)
    pad_rows = lambda x, fill: jnp.concatenate(
        [x, jnp.full((LANES - x.shape[0], x.shape[1]), fill, x.dtype)], axis=0) if x.shape[0] < LANES else x
    aff_t = aff_ref[0].T[:N_EXPERTS]
    segments = ((0, CTX, CAP_CTX, CAP_LAT), (CTX, NT, CAP_LAT, 0))
    bits = [pltpu.bitcast(aff_t[:, lo:hi], jnp.int32) for lo, hi, _, _ in segments]
    thrs = _kth_largest_bits(bits, [cap for _, _, cap, _ in segments])
    for (lo, hi, cap, dst), seg_bits, thr in zip(segments, bits, thrs):
        slot, rank = _topk_slots(seg_bits, thr, cap, triu)
        slot_ref[0, lo:hi] = pad_rows(slot, -1.0).T.astype(jnp.int32)
        if lo == CTX:
            starts = jnp.where(lane == LAT_TILES, float(cap), 0.0)
            for j in range(LAT_TILES):
                starts = starts + jnp.where(lane == j, rank[:, TM * j:TM * j + 1], 0.0)
            run = pad_rows(starts, 0.0).T.astype(jnp.int32)
            first = jnp.minimum(run[:LAT_TILES] & -BF16_ROWS, CAP_LAT - WIN)
            ends = pltpu.roll(run[1:LAT_TILES + 1], N_EXPERTS, axis=1)
            win = jnp.where(lane < N_EXPERTS, first, ends)
            for j in range(LAT_TILES):
                win_ref[0, j] = win[j:j + 1]
        vals = _token_values(aff_ref[0, lo:hi], lo)
        slot_ids = lax.broadcasted_iota(jnp.int32, (cap, 1), 0).astype(F32)
        for e in range(N_EXPERTS):
            hit = jnp.where(slot_ids == slot[e:e + 1, :], 1.0, 0.0).astype(BF16)
            picked = _dot(hit, vals)
            mine = ((lane & (N_EXPERTS - 1)) == e) & (lane < OFF_LANE)
            gate_ref[0, e, dst:dst + cap, :] = jnp.sum(jnp.where(mine, picked, 0.0), axis=1, keepdims=True)
            by_lane = pad_rows(picked, 0.0).T
            off = by_lane[OFF_LANE:OFF_LANE + 1] * float(LANES) + by_lane[OFF_LANE + 1:OFF_LANE + 2]
            off_ref[0, e, :, dst:dst + cap] = off[:, :cap].astype(jnp.int32)


def _topk(aff):
    bsz = aff.shape[0]
    tspec = pl.BlockSpec((1, NT, LANES), lambda b: (b, 0, 0))
    return pl.pallas_call(
        _topk_kernel, grid=(bsz,),
        in_specs=[tspec],
        out_specs=[tspec, pl.BlockSpec((1, N_EXPERTS, 1, CAP_ALL), lambda b: (b, 0, 0, 0)),
                   pl.BlockSpec((1, N_EXPERTS, CAP_ALL, 1), lambda b: (b, 0, 0, 0)),
                   pl.BlockSpec((1, LAT_TILES, 1, LANES), lambda b: (b, 0, 0, 0))],
        out_shape=[jax.ShapeDtypeStruct((bsz, NT, LANES), jnp.int32),
                   jax.ShapeDtypeStruct((bsz, N_EXPERTS, 1, CAP_ALL), jnp.int32),
                   jax.ShapeDtypeStruct((bsz, N_EXPERTS, CAP_ALL, 1), F32),
                   jax.ShapeDtypeStruct((bsz, LAT_TILES, 1, LANES), jnp.int32)],
        compiler_params=_cparams(1), name="expert_topk",
    )(aff)


def _pack_rows(hb):
    words = []
    for r in range(HS_ROWS):
        lo = pltpu.bitcast(hb[:, 256 * r:256 * r + LANES].astype(F32), jnp.int32)
        hi = pltpu.bitcast(hb[:, 256 * r + LANES:256 * (r + 1)].astype(F32), jnp.int32)
        words.append(lax.shift_right_logical(lo, jnp.int32(16)) | hi)
    return words


def _unpack_rows(words):
    cols = []
    for w in words:
        cols.append(pltpu.bitcast(lax.shift_left(w, jnp.int32(16)), F32).astype(BF16))
        cols.append(pltpu.bitcast(w & jnp.int32(-65536), F32).astype(BF16))
    return jnp.concatenate(cols, axis=1)


def _expert_kernel(hs_ref, off_ref, gate_ref, wg_ref, wu_ref, wd_ref, *refs, n_slots):
    out_refs, (tile_ref, wg_s, wu_s, wd_s) = refs[:-4], refs[-4:]

    @pl.when(pl.program_id(1) == 0)
    def _():
        wg_s[...] = wg_ref[0, 0].astype(BF16)
        wu_s[...] = wu_ref[0, 0].astype(BF16)
        wd_s[...] = wd_ref[0, 0].astype(BF16)

    for j in range(n_slots):
        src = pl.multiple_of(off_ref[0, 0, 0, j], HS_ROWS)
        tile_ref[pl.ds(j, HS_ROWS, stride=TILE_STRIDE), :] = hs_ref[0, pl.ds(src, HS_ROWS), :]
    xg = _unpack_rows([tile_ref[r * TILE_STRIDE:r * TILE_STRIDE + n_slots, :] for r in range(HS_ROWS)])
    hid = (_silu(_dot(xg, wg_s[...])) * _dot(xg, wu_s[...])).astype(BF16)
    y = _dot(hid, wd_s[...]) * gate_ref[0, 0, :n_slots]
    out_refs[0][0] = y[:CAP_LAT].astype(BF16)
    if n_slots > CAP_LAT:
        out_refs[1][0] = y[CAP_LAT:].astype(BF16)


def _experts(hs, off, gate, wg, wu, wd, layer, with_ctx):
    bsz = hs.shape[0]
    n_slots = CAP_ALL if with_ctx else CAP_LAT
    wspec = pl.BlockSpec((1, 1, D, EXPERT_FF), lambda e, b: (layer, e, 0, 0))
    out_specs = [pl.BlockSpec((1, CAP_LAT, D), lambda e, b: (b, e, 0))]
    out_shape = [jax.ShapeDtypeStruct((bsz, N_EXPERTS * CAP_LAT, D), BF16)]
    if with_ctx:
        out_specs.append(pl.BlockSpec((1, CAP_CTX, D), lambda e, b: (b, e, 0)))
        out_shape.append(jax.ShapeDtypeStruct((bsz, N_EXPERTS * CAP_CTX, D), BF16))
    return pl.pallas_call(
        functools.partial(_expert_kernel, n_slots=n_slots),
        grid=(N_EXPERTS, bsz),
        in_specs=[pl.BlockSpec((1, NT * HS_ROWS, LANES), lambda e, b: (b, 0, 0)),
                  pl.BlockSpec((1, 1, 1, CAP_ALL), lambda e, b: (b, e, 0, 0), memory_space=pltpu.SMEM),
                  pl.BlockSpec((1, 1, CAP_ALL, 1), lambda e, b: (b, e, 0, 0)),
                  wspec, wspec, pl.BlockSpec((1, 1, EXPERT_FF, D), lambda e, b: (layer, e, 0, 0))],
        out_specs=out_specs, out_shape=out_shape,
        scratch_shapes=[pltpu.VMEM((HS_ROWS * TILE_STRIDE, LANES), jnp.int32),
                        pltpu.VMEM((D, EXPERT_FF), BF16), pltpu.VMEM((D, EXPERT_FF), BF16),
                        pltpu.VMEM((EXPERT_FF, D), BF16)],
        compiler_params=_cparams(2), name="experts",
    )(hs, off, gate, wg, wu, wd)


WIN = 64
WIN_GROUP = 256 // WIN


def _combine_lat(slot_ref, yl_ref, win_ref, x_ref, gate, xo_ref):
    slot = slot_ref[0]
    fits = win_ref[0, 0, 0, N_EXPERTS] - win_ref[0, 0, 0, 0] <= WIN
    for e in range(1, N_EXPERTS):
        fits = jnp.logical_and(fits, win_ref[0, 0, 0, N_EXPERTS + e] - win_ref[0, 0, 0, e] <= WIN)

    @pl.when(fits)
    def _():
        cols = lax.broadcasted_iota(jnp.int32, (TM, WIN * WIN_GROUP), 1)
        acc = None
        for g in range(N_EXPERTS // WIN_GROUP):
            hit = cols < 0
            rows = []
            for j in range(WIN_GROUP):
                e = WIN_GROUP * g + j
                start = win_ref[0, 0, 0, e]
                s = slot[:, e:e + 1]
                hit = hit | (jnp.where(s >= 0, s - start + WIN * j, -1) == cols)
                rows.append(yl_ref[0, pl.ds(pl.multiple_of(e * CAP_LAT + start, BF16_ROWS), WIN), :])
            part = _dot(jnp.where(hit, 1.0, 0.0).astype(BF16), jnp.concatenate(rows, axis=0))
            acc = part if acc is None else acc + part
        xo_ref[0] = x_ref[0] + gate * acc

    @pl.when(jnp.logical_not(fits))
    def _():
        cols = lax.broadcasted_iota(jnp.int32, (TM, CAP_LAT), 1)
        onehot = jnp.concatenate(
            [jnp.where(slot[:, e:e + 1] == cols, 1.0, 0.0).astype(BF16) for e in range(N_EXPERTS)], axis=1)
        xo_ref[0] = x_ref[0] + gate * _dot(onehot, yl_ref[0])


def _combine_kernel(slot_ref, yl_ref, yc_ref, win_ref, x_ref, mod_ref, xo_ref):
    t = pl.program_id(1)
    gate = mod_ref[0, 0][5:6]

    @pl.when(t == 0)
    def _():
        slot = slot_ref[0]
        cols = lax.broadcasted_iota(jnp.int32, (TM, N_EXPERTS * CAP_CTX), 1)
        hit = cols < 0
        for e in range(N_EXPERTS):
            s = slot[:, e:e + 1]
            hit = hit | (jnp.where(s >= 0, s + e * CAP_CTX, -1) == cols)
        xo_ref[0] = x_ref[0] + gate * _dot(jnp.where(hit, 1.0, 0.0).astype(BF16), yc_ref[0])

    @pl.when(t > 0)
    def _():
        _combine_lat(slot_ref, yl_ref, win_ref, x_ref, gate, xo_ref)


def _win_spec(tile_of_step):
    return pl.BlockSpec((1, 1, 1, LANES), lambda b, t: (b, tile_of_step(t), 0, 0), memory_space=pltpu.SMEM)


def _combine(slot, yl, yc, win, x, modtab):
    bsz = x.shape[0]
    return pl.pallas_call(
        _combine_kernel,
        grid=(bsz, NTILES),
        in_specs=[_row_spec(LANES), pl.BlockSpec((1, N_EXPERTS * CAP_LAT, D), lambda b, t: (b, 0, 0)),
                  pl.BlockSpec((1, N_EXPERTS * CAP_CTX, D), lambda b, t: (b, 0, 0)),
                  _win_spec(lambda t: jnp.maximum(t - 1, 0)), _row_spec(D), _mod_spec()],
        out_specs=_row_spec(D),
        out_shape=jax.ShapeDtypeStruct((bsz, NT, D), F32),
        input_output_aliases={4: 0},
        compiler_params=_cparams(2), name="moe_combine",
    )(slot, yl, yc, win, x, modtab)


def _combine_last_kernel(slot_ref, yl_ref, win_ref, x_ref, mod_ref, xo_ref):
    _combine_lat(slot_ref, yl_ref, win_ref, x_ref, mod_ref[0, 0][5:6], xo_ref)


def _combine_last(slot, yl, win, x, modtab):
    bsz = x.shape[0]
    lat = lambda width: pl.BlockSpec((1, TM, width), lambda b, t: (b, t + 1, 0))
    return pl.pallas_call(
        _combine_last_kernel,
        grid=(bsz, LAT_TILES),
        in_specs=[lat(LANES), pl.BlockSpec((1, N_EXPERTS * CAP_LAT, D), lambda b, t: (b, 0, 0)),
                  _win_spec(lambda t: t), lat(D), pl.BlockSpec((1, 1, 6, D), lambda b, t: (b, 1, 0, 0))],
        out_specs=pl.BlockSpec((1, TM, D), lambda b, t: (b, t, 0)),
        out_shape=jax.ShapeDtypeStruct((bsz, SEQ, D), F32),
        compiler_params=_cparams(2), name="moe_combine_last",
    )(slot, yl, win, x, modtab)


def _rope_tables(rot_dim, width, lane_lo, lane_hi):
    n_freq = rot_dim // 4
    half = rot_dim // 2
    inv = jnp.float32(ROPE_THETA) ** (-jnp.arange(n_freq, dtype=F32) / n_freq)
    t = jnp.arange(SEQ, dtype=jnp.int32)
    row = (t // GRID_W).astype(F32)
    col = (t % GRID_W).astype(F32)
    ang = jnp.concatenate([row[:, None] * inv, col[:, None] * inv], axis=-1)
    cos, sin = jnp.cos(ang), jnp.sin(ang)
    lane = np.arange(width)
    inside = (lane % LANES >= lane_lo) & (lane % LANES < lane_hi)
    idx = lane % half
    sign = np.where(lane % rot_dim < half, -1.0, 1.0).astype(np.float32)
    cosw = jnp.where(inside[None, :], cos[:, idx], 1.0)
    sinw = jnp.where(inside[None, :], sin[:, idx] * sign[None, :], 0.0)
    ones = jnp.ones((CTX, width), F32)
    return jnp.concatenate([ones, cosw], axis=0), jnp.concatenate([0.0 * ones, sinw], axis=0)


def _block_diag_ones():
    i = np.arange(256)
    return jnp.asarray((i[:, None] // 64) == (i[None, :] // 64), dtype=BF16)


_GQA_ORDER = np.array([8 * kp + 4 * odd + i for kp in range(2) for i in range(4) for odd in range(2)])


def _na_bias(rpb):
    out = []
    rows = SEQ // GRID_W
    qrows, krows = TM // GRID_W, BAND_TILES * TM // GRID_W
    col = np.arange(GRID_W)
    cs = np.clip(col - NA_COLS // 2, 0, GRID_W - NA_COLS)
    ok_c = (col[None, :] >= cs[:, None]) & (col[None, :] < cs[:, None] + NA_COLS)
    dc = np.clip(col[None, :] - col[:, None] + NA_COLS - 1, 0, 2 * NA_COLS - 2)
    pick_c = (dc[:, :, None] == np.arange(2 * NA_COLS - 1)).astype(np.float32)
    n_dc = 2 * NA_COLS - 1
    pick_c2 = np.zeros((GRID_W, 2, GRID_W, 2, n_dc), np.float32)
    for rr in range(2):
        pick_c2[:, rr, :, rr, :] = pick_c
    pick_c2 = pick_c2.reshape(GRID_W, 2 * GRID_W, 2 * n_dc)
    for r0, bs in ((0, 0), (8, 4), (rows - 4, rows - 12)):
        r, kr = r0 + np.arange(qrows), bs + np.arange(krows)
        rs = np.clip(r - NA_ROWS // 2, 0, rows - NA_ROWS)
        ok_r = (kr[None, :] >= rs[:, None]) & (kr[None, :] < rs[:, None] + NA_ROWS)
        dr = kr[None, :] - r[:, None] + NA_ROWS - 1
        pick_r = (dr[:, :, None] == np.arange(2 * NA_ROWS - 1)).astype(np.float32)
        t1 = jnp.einsum('ard,hde->hare', pick_r, rpb.astype(F32) * LOG2E, precision=lax.Precision.HIGHEST)
        t2 = t1.reshape(C_HEADS, qrows, krows // 2, 2 * n_dc)
        band = jnp.einsum('hapf,cuf->hacpu', t2, pick_c2, precision=lax.Precision.HIGHEST)
        ok = (ok_r[:, None, :, None] & ok_c[None, :, None, :]).reshape(1, qrows, GRID_W, krows // 2, 2 * GRID_W)
        band = jnp.where(ok, band, NEG_INF).reshape(C_HEADS, TM, BAND_TILES * TM)
        out.append(jnp.concatenate([band, jnp.zeros((C_HEADS, TM, TM), F32)], axis=-1))
    return jnp.stack(out)


def _gqa_params(w_qkv, qn, kn, wo):
    nq, nk = A_HEADS * A_HEAD_DIM, A_KV_HEADS * A_HEAD_DIM
    wq = w_qkv[:, :nq].reshape(D, A_HEADS, A_HEAD_DIM)[:, _GQA_ORDER].reshape(D, nq)
    w = jnp.concatenate([wq, w_qkv[:, nq:]], axis=1).astype(BF16)
    wo_p = wo.reshape(A_HEADS, A_HEAD_DIM, D)[_GQA_ORDER].reshape(nq, D).astype(BF16)
    scale = A_HEAD_DIM ** -0.5 * LOG2E
    return w, (jnp.tile(qn, 4) * scale)[None], jnp.tile(kn, 4)[None], wo_p


def _mla_params(w_dq, qn_lat, w_uq, w_dkv, kvn_lat, w_ukv, qn, kn, cossin_q, cossin_k):
    qd = B_NOPE + B_ROPE
    w1 = jnp.concatenate([w_dq, w_dkv, jnp.zeros((D, 768 - B_Q_RANK - B_KV_RANK - B_ROPE), F32)], axis=1)
    pad_q = jnp.zeros((B_Q_RANK, B_HEADS, LANES - qd), F32)
    wuq = jnp.concatenate([w_uq.reshape(B_Q_RANK, B_HEADS, qd), pad_q], axis=2).reshape(B_Q_RANK, B_HEADS * LANES)
    ukv = w_ukv.reshape(B_KV_RANK, B_HEADS, B_NOPE + B_V)
    wk_top = jnp.concatenate([ukv[:, :, :B_NOPE], jnp.zeros((B_KV_RANK, B_HEADS, LANES - B_NOPE), F32)], axis=2)
    route = np.zeros((LANES, B_HEADS, LANES), np.float32)
    route[np.arange(B_ROPE), :, B_NOPE + np.arange(B_ROPE)] = 1.0
    wk = jnp.concatenate([wk_top.reshape(B_KV_RANK, -1), jnp.asarray(route).reshape(LANES, -1)], axis=0)
    wv = ukv[:, :, B_NOPE:].reshape(B_KV_RANK, B_HEADS * B_V)
    scale = qd ** -0.5 * LOG2E
    zpad = jnp.zeros((LANES - qd,), F32)
    qgain = jnp.tile(jnp.concatenate([qn * scale, zpad]), 2)[None]
    qinv = jnp.tile(jnp.concatenate([jnp.full((B_NOPE,), 1.0 / B_NOPE), jnp.full((LANES - B_NOPE,), 1.0 / B_ROPE)]), 2)
    kgain = jnp.tile(jnp.concatenate([kn[:B_NOPE], jnp.zeros((LANES - B_NOPE,), F32)]), 2)[None]
    kmask = jnp.tile(jnp.concatenate([jnp.ones((B_NOPE,), F32), jnp.zeros((LANES - B_NOPE,), F32)]), 2)[None]
    gkr = jnp.concatenate([kn[B_NOPE:], jnp.zeros((LANES - B_ROPE,), F32)])[None]
    return dict(w1=w1.astype(BF16), gql=qn_lat[None], gkl=kvn_lat[None], gkr=gkr, kcos=cossin_k[0], ksin=cossin_k[1],
                wuq=wuq.astype(BF16), wk=wk.astype(BF16), wv=wv.astype(BF16), bd=_block_diag_ones(),
                qgain=qgain, qinv=qinv[None].astype(F32), qcos=cossin_q[0], qsin=cossin_q[1], kgain=kgain, kmask=kmask)


def kernel(x, c, ctx, c_ctx, ada_w, ada_b, norm1_w, norm2_w, a_wqkv, a_qnorm, a_knorm, a_wo, b_wdq, b_qnorm_lat, b_wuq, b_wdkv, b_kvnorm_lat, b_wukv, b_qnorm, b_knorm, b_wo, c_wqkv, c_qnorm, c_knorm, c_rpb, c_wo, moe_router, moe_wg, moe_wu, moe_wd):
    bsz = x.shape[0]
    assert x.shape[1:] == (SEQ, D) and ctx.shape[1:] == (CTX, D)
    mod_rows = -(-(bsz + 1) // 16) * 16
    cvec = jnp.concatenate([c, c_ctx[None], jnp.zeros((mod_rows - bsz - 1, D), F32)], axis=0)
    mods = _ada_all(cvec, ada_w, ada_b)
    xs = (ctx, x)

    bd = _block_diag_ones()
    rope64 = _rope_tables(A_HEAD_DIM, 256, 0, LANES)
    rope_mla_q = _rope_tables(B_ROPE, 256, B_NOPE, B_NOPE + B_ROPE)
    rope_mla_k = _rope_tables(B_ROPE, LANES, 0, B_ROPE)
    router = jnp.pad(moe_router, ((0, 0), (0, 0), (0, LANES - N_EXPERTS))).astype(BF16)

    for i in range(DEPTH):
        last = i == DEPTH - 1
        kind, j = i % 3, i // 3
        m = mods[i]
        m_lat = m[:bsz].reshape(bsz, 6, D)
        m_ctx = jnp.broadcast_to(m[bsz].reshape(1, 6, D), (bsz, 6, D))
        modtab = jnp.stack([m_ctx, m_lat], axis=1)
        g1, g2 = norm1_w[i][None], norm2_w[i][None]

        if kind == 0:
            w, gq, gk, wo = _gqa_params(a_wqkv[j], a_qnorm[j], a_knorm[j], a_wo[j])
            q, k, v = _qkv_project(xs, modtab, g1, w, bd, gq, gk, rope64[0], rope64[1],
                                   A_HEADS * A_HEAD_DIM, A_KV_HEADS * A_HEAD_DIM, True)
            bound = _score_bound(A_HEAD_DIM * jnp.max(gq * gq), A_HEAD_DIM * jnp.max(gk * gk))
            o = _pair_attention(bound, q, k, v, wide=False, q_pairs_per_kv=4)
        elif kind == 1:
            p = _mla_params(b_wdq[j], b_qnorm_lat[j], b_wuq[j], b_wdkv[j], b_kvnorm_lat[j], b_wukv[j],
                            b_qnorm[j], b_knorm[j], rope_mla_q, rope_mla_k)
            q, k, v = _mla_project(xs, modtab, g1, p)
            sq_max = lambda g: jnp.max(g * g)
            qg, kg = p["qgain"][0, :LANES], b_knorm[j]
            bound = _score_bound(B_NOPE * sq_max(qg[:B_NOPE]) + B_ROPE * sq_max(qg[B_NOPE:]),
                                 B_NOPE * sq_max(kg[:B_NOPE]) + B_ROPE * sq_max(kg[B_NOPE:]))
            o = _pair_attention(bound, q, k, v, wide=True, q_pairs_per_kv=1)
            wo = b_wo[j].astype(BF16)
        else:
            hd = C_HEADS * C_HEAD_DIM
            gq = (jnp.tile(c_qnorm[j], 4) * (C_HEAD_DIM ** -0.5 * LOG2E))[None]
            gk = jnp.tile(c_knorm[j], 4)[None]
            q, k, v = _qkv_project(xs, modtab, g1, c_wqkv[j].astype(BF16), bd, gq, gk, rope64[0], rope64[1],
                                   hd, hd, False)
            bound = _score_bound(C_HEAD_DIM * jnp.max(gq * gq), C_HEAD_DIM * jnp.max(gk * gk),
                                 extra=LOG2E * jnp.max(jnp.abs(c_rpb[j])))
            o = _na_attention(bound, q, k, v, _na_bias(c_rpb[j]))
            wo = c_wo[j].astype(BF16)

        xs, hs, aff = _out_router(o, wo, xs, modtab, g2, router[i])
        slot, off, gate, win = _topk(aff)
        ys = _experts(hs, off, gate,
                      moe_wg, moe_wu, moe_wd, layer=i, with_ctx=not last)
        if last:
            return _combine_last(slot, ys[0], win, xs, modtab)
        xs = _combine(slot, ys[0], ys[1], win, xs, modtab)
```
